```python
import jax, jax.numpy as jnp
from jax import lax
import numpy as np

D_MODEL = 1024
BATCH = 2
SEQ = 8192
DEPTH = 2
DEC_BATCH = 128
DEC_SEQ = 4
PAST_LEN = 2048
PAGE_SIZE = 128

N_MIXERS = 2
RET_HEADS = 4
RET_DK = 256
RET_DV = 512
RET_CHUNK = 128
ATT_GROUPS = ((128, 1), (512, 4), (2048, 16))
N_GROUPS = len(ATT_GROUPS)
ATT_HEADS_PER_GROUP = 8
ATT_HEAD_DIM = 64
ATT_HEADS = N_GROUPS * ATT_HEADS_PER_GROUP
D_FF = 2816
ALPHA = (2 * DEPTH) ** 0.25
BETA = (8 * DEPTH) ** -0.25
NORM_EPS = 1e-5

kernel_name = "hybrid_retention_dilated_macaron_step"


def layer_norm(x, g, b):
    xf = x.astype(jnp.float32)
    mu = jnp.mean(xf, -1, keepdims=True)
    var = jnp.mean(jnp.square(xf - mu), -1, keepdims=True)
    return ((xf - mu) * lax.rsqrt(var + NORM_EPS) * g + b).astype(x.dtype)


def swiglu(u, w_in, w_out):
    a, b = jnp.split(u @ w_in, 2, axis=-1)
    return (jax.nn.silu(a) * b) @ w_out


def retention_log_gamma():
    return jnp.log1p(-(2.0 ** (-5.0 - jnp.arange(RET_HEADS, dtype=jnp.float32))))


def alibi_slopes():
    h = jnp.arange(1, ATT_HEADS + 1, dtype=jnp.float32)
    return (2.0 ** (-8.0 * h / ATT_HEADS)).reshape(N_GROUPS, ATT_HEADS_PER_GROUP)


def retention_scan(q, k, v, s0, chunk):
    n, l, h, _ = q.shape
    nc = l // chunk

    def split(t):
        return t.reshape(n, nc, chunk, h, t.shape[-1]).transpose(1, 0, 2, 3, 4)

    log_g = retention_log_gamma()
    pos = jnp.arange(chunk, dtype=jnp.float32)
    diff = pos[:, None] - pos[None, :]
    causal = diff >= 0
    inner_decay = jnp.where(causal, jnp.exp(jnp.where(causal, diff, 0.0) * log_g[:, None, None]), 0.0)
    q_decay = jnp.exp((pos[:, None] + 1.0) * log_g)
    k_decay = jnp.exp((chunk - 1.0 - pos[:, None]) * log_g)
    chunk_decay = jnp.exp(chunk * log_g)

    def step(s, qkv):
        qc, kc, vc = qkv
        scores = jnp.einsum('nqhd,nkhd->nhqk', qc, kc) * inner_decay
        o = (jnp.einsum('nhqk,nkhe->nqhe', scores, vc)
             + jnp.einsum('nqhd,nhde->nqhe', qc, s) * q_decay[None, :, :, None])
        s = (s * chunk_decay[None, :, None, None]
             + jnp.einsum('nkhd,nkhe->nhde', kc * k_decay[None, :, :, None], vc))
        return s, o

    s_final, o = lax.scan(step, s0, (split(q), split(k), split(v)))
    o = o.transpose(1, 0, 2, 3, 4).reshape(n, l, h, v.shape[-1])
    return o, s_final


def retention_mixer(u, w_in, gn_g, gn_b, w_out, s0, chunk):
    n, l, _ = u.shape
    qk_w = RET_HEADS * RET_DK
    v_w = RET_HEADS * RET_DV
    q, k, v, g = jnp.split(u @ w_in, [qk_w, 2 * qk_w, 2 * qk_w + v_w], axis=-1)
    f32 = jnp.float32
    q = q.reshape(n, l, RET_HEADS, RET_DK).astype(f32)
    k = k.reshape(n, l, RET_HEADS, RET_DK).astype(f32) * (RET_DK ** -0.5)
    v = v.reshape(n, l, RET_HEADS, RET_DV).astype(f32)
    o, s_new = retention_scan(q, k, v, s0.astype(f32), chunk)
    mu = jnp.mean(o, -1, keepdims=True)
    var = jnp.mean(jnp.square(o - mu), -1, keepdims=True)
    o = ((o - mu) * lax.rsqrt(var + NORM_EPS)).reshape(n, l, v_w) * gn_g + gn_b
    y = (jax.nn.silu(g) * o.astype(u.dtype)) @ w_out
    return y, s_new.astype(u.dtype)


def softmax_with_lse(s, valid):
    s = jnp.where(valid, s, -jnp.inf)
    m = jnp.max(s, -1, keepdims=True)
    e = jnp.exp(s - m)
    den = jnp.sum(e, -1, keepdims=True)
    return e / den, (m + jnp.log(den))[..., 0]


def banded_dilated_attention(q, k, v, dil, nback, slopes):
    n, s_len, h, dh = q.shape
    L = s_len // dil
    nn = n * dil

    def to_classes(t):
        return t.reshape(n, L, dil, h, dh).transpose(0, 2, 1, 3, 4).reshape(nn, L, h, dh)

    qr, kr, vr = to_classes(q), to_classes(k), to_classes(v)
    blk = nback
    nb = -(-L // blk)
    lp = nb * blk
    qb = jnp.pad(qr, ((0, 0), (0, lp - L), (0, 0), (0, 0))).reshape(nn, nb, blk, h, dh)

    def windows(t):
        tp = jnp.pad(t, ((0, 0), (blk, lp - L), (0, 0), (0, 0))).reshape(nn, nb + 1, blk, h, dh)
        return jnp.concatenate([tp[:, :-1], tp[:, 1:]], axis=2)

    kw, vw = windows(kr), windows(vr)
    steps = blk + jnp.arange(blk)[:, None] - jnp.arange(2 * blk)[None, :]
    kpos = jnp.arange(nb)[:, None] * blk + jnp.arange(2 * blk)[None, :] - blk
    valid = ((steps >= 0) & (steps <= nback))[None] & (kpos >= 0)[:, None, :]
    scores = jnp.einsum('nbqhd,nbkhd->nbhqk', qb, kw).astype(jnp.float32) * (dh ** -0.5)
    scores = scores - (slopes * dil)[:, None, None] * steps.astype(jnp.float32)
    p, lse = softmax_with_lse(scores, valid[None, :, None])
    o = jnp.einsum('nbhqk,nbkhd->nbqhd', p.astype(vw.dtype), vw).reshape(nn, lp, h, dh)[:, :L]
    lse = lse.transpose(0, 1, 3, 2).reshape(nn, lp, h)[:, :L]
    o = o.reshape(n, dil, L, h, dh).transpose(0, 2, 1, 3, 4).reshape(n, s_len, h, dh)
    lse = lse.reshape(n, dil, L, h).transpose(0, 2, 1, 3).reshape(n, s_len, h)
    return o, lse


def gathered_dilated_attention(q, k_new, v_new, kv_buf, dil, nback, slopes):
    n, t, h, dh = q.shape
    lb = kv_buf.shape[1]
    k_all = jnp.concatenate([kv_buf[:, :, 0], k_new], axis=1)
    v_all = jnp.concatenate([kv_buf[:, :, 1], v_new], axis=1)
    j = jnp.arange(nback + 1)
    idx = lb + jnp.arange(t)[:, None] - j[None, :] * dil
    valid = idx >= 0
    idx = jnp.maximum(idx, 0)
    kg = k_all[:, idx]
    vg = v_all[:, idx]
    scores = jnp.einsum('nqhd,nqkhd->nhqk', q, kg).astype(jnp.float32) * (dh ** -0.5)
    scores = scores - (slopes * dil)[:, None, None] * j.astype(jnp.float32)
    p, lse = softmax_with_lse(scores, valid[None, None])
    o = jnp.einsum('nhqk,nqkhd->nqhd', p.astype(vg.dtype), vg)
    return o, lse.transpose(0, 2, 1)


def dilated_mixer(u, w_in, w_out, kv_bufs):
    n, l, _ = u.shape
    qkv = (u @ w_in).reshape(n, l, 3, N_GROUPS, ATT_HEADS_PER_GROUP, ATT_HEAD_DIM)
    slopes = alibi_slopes()
    outs, lses, new_kv = [], [], []
    for g, (window, dil) in enumerate(ATT_GROUPS):
        nback = window // dil
        qg, kg, vg = qkv[:, :, 0, g], qkv[:, :, 1, g], qkv[:, :, 2, g]
        if kv_bufs is None:
            o, lse = banded_dilated_attention(qg, kg, vg, dil, nback, slopes[g])
            keep = min(window, l)
            new_kv.append(jnp.stack([kg[:, l - keep:], vg[:, l - keep:]], axis=2))
        else:
            o, lse = gathered_dilated_attention(qg, kg, vg, kv_bufs[g], dil, nback, slopes[g])
            new_kv.append(jnp.stack([kg, vg], axis=2))
        outs.append(o)
        lses.append(lse)
    alpha = jax.nn.softmax(jnp.stack(lses, 0), axis=0)
    o = jnp.einsum('gnsh,gnshd->nshd', alpha, jnp.stack(outs, 0).astype(jnp.float32))
    y = o.reshape(n, l, ATT_HEADS_PER_GROUP * ATT_HEAD_DIM).astype(u.dtype) @ w_out
    return y, new_kv


def run_trunk(x, c, ret_state, kv_bufs, ret_chunk, w_ada, b_ada, ln_g, ln_b, w_ffn_in, w_ffn_out,
              w_ret_in, ret_gn_g, ret_gn_b, w_ret_out, w_att_in, w_att_out):
    n = x.shape[0]
    ret_new, kv_new = None, None
    for i in range(DEPTH):
        ada = (jax.nn.silu(c) @ w_ada[i] + b_ada[i]).reshape(n, 1, 3, 3, D_MODEL)

        def modulate(h, j):
            return h * (1.0 + ada[:, :, j, 1]) + ada[:, :, j, 0]

        h = swiglu(modulate(x, 0), w_ffn_in[i, 0], w_ffn_out[i, 0])
        x = layer_norm(ALPHA * x + 0.5 * (1.0 + ada[:, :, 0, 2]) * h, ln_g[i, 0], ln_b[i, 0])
        if i % N_MIXERS == 0:
            mix, ret_new = retention_mixer(modulate(x, 1), w_ret_in, ret_gn_g, ret_gn_b, w_ret_out,
                                           ret_state, ret_chunk)
        else:
            mix, kv_new = dilated_mixer(modulate(x, 1), w_att_in, w_att_out, kv_bufs)
        x = layer_norm(ALPHA * x + (1.0 + ada[:, :, 1, 2]) * mix, ln_g[i, 1], ln_b[i, 1])
        h = swiglu(modulate(x, 2), w_ffn_in[i, 1], w_ffn_out[i, 1])
        x = layer_norm(ALPHA * x + 0.5 * (1.0 + ada[:, :, 2, 2]) * h, ln_g[i, 2], ln_b[i, 2])
    return x, ret_new, kv_new


def setup_inputs(seed: int = 0) -> dict:
    key = jax.random.key(seed)
    ks = jax.random.split(key, 24)
    f32 = jnp.float32
    d = D_MODEL

    def nrm(k, shape, scale):
        return jax.random.normal(k, shape, f32) * scale

    ret_in_w = RET_HEADS * (2 * RET_DK + 2 * RET_DV)
    ret_v_w = RET_HEADS * RET_DV
    att_in_w = 3 * ATT_HEADS * ATT_HEAD_DIM
    att_out_w = ATT_HEADS_PER_GROUP * ATT_HEAD_DIM
    kv_shape = lambda w: (DEC_BATCH, min(w, PAST_LEN), 2, ATT_HEADS_PER_GROUP, ATT_HEAD_DIM)
    return {
        "x_prompt": nrm(ks[0], (BATCH, SEQ, d), 1.0),
        "x_sample": nrm(ks[1], (DEC_BATCH, DEC_SEQ, d), 1.0),
        "c_prompt": nrm(ks[2], (BATCH, d), 1.0),
        "c_sample": nrm(ks[3], (DEC_BATCH, d), 1.0),
        "state_ret": nrm(ks[4], (DEC_BATCH, RET_HEADS, RET_DK, RET_DV), 0.5),
        "cache_kv_w128": nrm(ks[5], kv_shape(ATT_GROUPS[0][0]), 1.0),
        "cache_kv_w512": nrm(ks[6], kv_shape(ATT_GROUPS[1][0]), 1.0),
        "cache_kv_w2048": nrm(ks[7], kv_shape(ATT_GROUPS[2][0]), 1.0),
        "w_ada": nrm(ks[8], (DEPTH, d, 9 * d), 0.2 * d ** -0.5),
        "b_ada": nrm(ks[9], (DEPTH, 9 * d), 0.01),
        "ln_g": 1.0 + nrm(ks[10], (DEPTH, 3, d), 0.02),
        "ln_b": nrm(ks[11], (DEPTH, 3, d), 0.01),
        "w_ffn_in": nrm(ks[12], (DEPTH, 2, d, 2 * D_FF), d ** -0.5),
        "w_ffn_out": nrm(ks[13], (DEPTH, 2, D_FF, d), BETA * D_FF ** -0.5),
        "w_ret_in": nrm(ks[14], (d, ret_in_w), d ** -0.5),
        "ret_gn_g": 1.0 + nrm(ks[15], (ret_v_w,), 0.02),
        "ret_gn_b": nrm(ks[16], (ret_v_w,), 0.01),
        "w_ret_out": nrm(ks[17], (ret_v_w, d), BETA * ret_v_w ** -0.5),
        "w_att_in": nrm(ks[18], (d, att_in_w), d ** -0.5),
        "w_att_out": nrm(ks[19], (att_out_w, d), BETA * att_out_w ** -0.5),
    }


def reference(x_prompt, x_sample, c_prompt, c_sample, state_ret, cache_kv_w128, cache_kv_w512,
              cache_kv_w2048, w_ada, b_ada, ln_g, ln_b, w_ffn_in, w_ffn_out, w_ret_in, ret_gn_g,
              ret_gn_b, w_ret_out, w_att_in, w_att_out):
    s0_prompt = jnp.zeros((x_prompt.shape[0], RET_HEADS, RET_DK, RET_DV), x_prompt.dtype)
    y_prompt, ret_p, kv_p = run_trunk(
        x_prompt, c_prompt, s0_prompt, None, RET_CHUNK, w_ada, b_ada, ln_g, ln_b, w_ffn_in,
        w_ffn_out, w_ret_in, ret_gn_g, ret_gn_b, w_ret_out, w_att_in, w_att_out)
    y_sample, ret_s, kv_s = run_trunk(
        x_sample, c_sample, state_ret, (cache_kv_w128, cache_kv_w512, cache_kv_w2048),
        x_sample.shape[1], w_ada, b_ada, ln_g, ln_b, w_ffn_in, w_ffn_out, w_ret_in, ret_gn_g,
        ret_gn_b, w_ret_out, w_att_in, w_att_out)
    return (y_prompt, y_sample, ret_p, ret_s, kv_p[0], kv_s[0], kv_p[1], kv_s[1], kv_p[2], kv_s[2])
```

```python
import functools

import numpy as np
import jax
import jax.numpy as jnp
from jax import lax
from jax.experimental import pallas as pl
from jax.experimental.pallas import tpu as pltpu

F32 = jnp.float32
BF16 = jnp.bfloat16

D_MODEL = 1024
DEPTH = 2
D_FF = 2816
RET_HEADS = 4
RET_DK = 256
RET_DV = 512
RET_CHUNK = 128
ATT_GROUPS = ((128, 1), (512, 4), (2048, 16))
N_GROUPS = 3
ATT_HPG = 8
ATT_DH = 64
ATT_W = ATT_HPG * ATT_DH
NBACK = 128
ALPHA = (2 * DEPTH) ** 0.25
NORM_EPS = 1e-5
ATT_SCALE = ATT_DH ** -0.5

V7X_VMEM_LIMIT_BYTES = 60000 * 1024
LANES = 128
SUBLANES = 8

N_SAMPLE = 128
T_SAMPLE = 4
ADA_ROWS = 136
PROMPT_ROW_BLOCK = N_SAMPLE // SUBLANES


def _params(*semantics):
    return pltpu.CompilerParams(dimension_semantics=semantics,
                                vmem_limit_bytes=V7X_VMEM_LIMIT_BYTES)


def _resident(block, index_map):
    return pl.BlockSpec(block, index_map, pipeline_mode=pl.Buffered(1))


def _layer_norm(z, g, b):
    mu = jnp.mean(z, -1, keepdims=True)
    zc = z - mu
    var = jnp.mean(zc * zc, -1, keepdims=True)
    return zc * lax.rsqrt(var + NORM_EPS) * g + b


def _log_gamma():
    return np.log1p(-(2.0 ** (-5.0 - np.arange(RET_HEADS, dtype=np.float64))))


def _alibi_slopes():
    h = np.arange(1, N_GROUPS * ATT_HPG + 1, dtype=np.float64)
    return (2.0 ** (-8.0 * h / (N_GROUPS * ATT_HPG))).reshape(N_GROUPS, ATT_HPG)


def _ada_body(c_ref, w_ref, b_ref, o_ref):
    s = jax.nn.silu(c_ref[...]).astype(BF16)
    w = w_ref[0].astype(BF16)
    o_ref[0, 0] = jnp.dot(s, w, preferred_element_type=F32) + b_ref[0, 0]


def _ada_table(c_all, w_ada, b_ada):
    d = D_MODEL
    return pl.pallas_call(
        _ada_body,
        grid=(DEPTH, 9),
        in_specs=[
            _resident((ADA_ROWS, d), lambda i, j: (0, 0)),
            pl.BlockSpec((1, d, d), lambda i, j: (i, 0, j)),
            pl.BlockSpec((1, 1, 1, d), lambda i, j: (i, j, 0, 0)),
        ],
        out_specs=pl.BlockSpec((1, 1, ADA_ROWS, d), lambda i, j: (i, j, 0, 0)),
        out_shape=jax.ShapeDtypeStruct((DEPTH, 9, ADA_ROWS, d), F32),
        compiler_params=_params("arbitrary", "arbitrary"),
        name="ada_table",
    )(c_all, w_ada, b_ada.reshape(DEPTH, 9, 1, d))


class _Trunk:
    def __init__(self, tokens, tile, per_row, tiles_per_seq):
        self.tokens = tokens
        self.tile = tile
        self.per_row = per_row
        self.tiles_per_seq = tiles_per_seq

    def mod_spec(self, layer, sub):
        if self.per_row:
            return pl.BlockSpec((1, 3, self.tile, D_MODEL), lambda i: (layer, sub, 0, 0))
        return pl.BlockSpec((1, 3, SUBLANES, D_MODEL),
                            lambda i: (layer, sub, PROMPT_ROW_BLOCK, 0))

    def mod_rows(self, mod_ref, k):
        if self.per_row:
            return mod_ref[0, k]
        n = pl.program_id(0) // self.tiles_per_seq
        return mod_ref[0, k, pl.ds(n, 1), :]


def _ffn_body(x_ref, mod_ref, win_ref, wout_ref, g_ref, b_ref, o_ref, *, trunk):
    x = x_ref[...]
    shift = trunk.mod_rows(mod_ref, 0)
    scale = trunk.mod_rows(mod_ref, 1)
    gate = trunk.mod_rows(mod_ref, 2)
    u = (x * (1.0 + scale) + shift).astype(BF16)
    h = jnp.dot(u, win_ref[...], preferred_element_type=F32)
    a = h[:, :D_FF]
    b = h[:, D_FF:]
    act = (jax.nn.silu(a) * b).astype(BF16)
    y = jnp.dot(act, wout_ref[...], preferred_element_type=F32)
    z = ALPHA * x + (0.5 * (1.0 + gate)) * y
    o_ref[...] = _layer_norm(z, g_ref[...], b_ref[...])


def _ffn(x, ada, w_in, w_out, ln_g, ln_b, trunk, layer, sub):
    d = D_MODEL
    tm = trunk.tile
    return pl.pallas_call(
        functools.partial(_ffn_body, trunk=trunk),
        grid=(trunk.tokens // tm,),
        in_specs=[
            pl.BlockSpec((tm, d), lambda i: (i, 0)),
            trunk.mod_spec(layer, sub),
            _resident((d, 2 * D_FF), lambda i: (0, 0)),
            _resident((D_FF, d), lambda i: (0, 0)),
            _resident((1, d), lambda i: (0, 0)),
            _resident((1, d), lambda i: (0, 0)),
        ],
        out_specs=pl.BlockSpec((tm, d), lambda i: (i, 0)),
        out_shape=jax.ShapeDtypeStruct((trunk.tokens, d), F32),
        compiler_params=_params("arbitrary"),
        name="ffn",
    )(x, ada, w_in, w_out, ln_g.reshape(1, d), ln_b.reshape(1, d))


def _inproj_body(x_ref, mod_ref, w_ref, o_ref, *tail_refs, trunk, tail_start):
    x = x_ref[...]
    u = (x * (1.0 + trunk.mod_rows(mod_ref, 1)) + trunk.mod_rows(mod_ref, 0)).astype(BF16)
    y = jnp.dot(u, w_ref[...], preferred_element_type=F32)
    o_ref[...] = y.astype(o_ref.dtype)
    if tail_refs:
        tail_refs[0][0] = y[:, tail_start:]


def _inproj(x, ada, w, trunk, layer, out_dtype, tail=None):
    d = D_MODEL
    tm = trunk.tile
    n_out = w.shape[1]
    out_shape = [jax.ShapeDtypeStruct((trunk.tokens, n_out), out_dtype)]
    out_specs = [pl.BlockSpec((tm, n_out), lambda i: (i, 0))]
    tail_start = 0
    if tail is not None:
        tail_start, tail_rows = tail
        tps = trunk.tiles_per_seq
        first = tps - tail_rows // tm
        n_seq = trunk.tokens // (tps * tm)
        out_shape.append(jax.ShapeDtypeStruct((n_seq, tail_rows, n_out - tail_start), F32))
        out_specs.append(pl.BlockSpec(
            (1, tm, n_out - tail_start),
            lambda i: (i // tps, jnp.maximum(i % tps - first, 0), 0)))
    res = pl.pallas_call(
        functools.partial(_inproj_body, trunk=trunk, tail_start=tail_start),
        grid=(trunk.tokens // tm,),
        in_specs=[
            pl.BlockSpec((tm, d), lambda i: (i, 0)),
            trunk.mod_spec(layer, 1),
            _resident((d, n_out), lambda i: (0, 0)),
        ],
        out_specs=out_specs,
        out_shape=out_shape,
        compiler_params=_params("arbitrary"),
        name="inproj",
    )(x, ada, w)
    return res if tail is not None else res[0]


def _merge_groups(o_refs, lse_refs):
    lses = [r[...] for r in lse_refs]
    m = jnp.maximum(jnp.maximum(lses[0], lses[1]), lses[2])
    ws = [jnp.exp(l - m) for l in lses]
    den = ws[0] + ws[1] + ws[2]
    num = ws[0] * o_refs[0][...] + ws[1] * o_refs[1][...] + ws[2] * o_refs[2][...]
    return num / den


def _outproj_body(*refs, trunk, merge):
    if merge:
        o_refs, lse_refs = refs[0:3], refs[3:6]
        a = _merge_groups(o_refs, lse_refs).astype(BF16)
        refs = refs[6:]
    else:
        a = refs[0][...].astype(BF16)
        refs = refs[1:]
    x_ref, mod_ref, w_ref, g_ref, b_ref, o_ref = refs
    x = x_ref[...]
    gate = trunk.mod_rows(mod_ref, 2)
    y = jnp.dot(a, w_ref[...], preferred_element_type=F32)
    z = ALPHA * x + (1.0 + gate) * y
    o_ref[...] = _layer_norm(z, g_ref[...], b_ref[...])


def _outproj(acts, x, ada, w, ln_g, ln_b, trunk, layer):
    d = D_MODEL
    tm = trunk.tile
    merge = len(acts) > 1
    k_in = w.shape[0]
    return pl.pallas_call(
        functools.partial(_outproj_body, trunk=trunk, merge=merge),
        grid=(trunk.tokens // tm,),
        in_specs=[pl.BlockSpec((tm, k_in), lambda i: (i, 0)) for _ in acts] + [
            pl.BlockSpec((tm, d), lambda i: (i, 0)),
            trunk.mod_spec(layer, 1),
            _resident((k_in, d), lambda i: (0, 0)),
            _resident((1, d), lambda i: (0, 0)),
            _resident((1, d), lambda i: (0, 0)),
        ],
        out_specs=pl.BlockSpec((tm, d), lambda i: (i, 0)),
        out_shape=jax.ShapeDtypeStruct((trunk.tokens, d), F32),
        compiler_params=_params("arbitrary"),
        name="outproj",
    )(*acts, x, ada, w, ln_g.reshape(1, d), ln_b.reshape(1, d))


def _group_norm_gate(o, g, gn_g, gn_b):
    mu = jnp.mean(o, -1, keepdims=True)
    oc = o - mu
    var = jnp.mean(oc * oc, -1, keepdims=True)
    on = oc * lax.rsqrt(var + NORM_EPS) * gn_g + gn_b
    return jax.nn.silu(g) * on


RET_TILE = 512


def _ret_decay_tables(chunk):
    lg = _log_gamma()
    pos = np.arange(chunk, dtype=np.float64)
    diff = pos[:, None] - pos[None, :]
    inner = np.where(diff >= 0, np.exp(np.maximum(diff, 0.0)[None] * lg[:, None, None]), 0.0)
    qd = np.exp((pos[None, :] + 1.0) * lg[:, None])
    kd = np.exp((chunk - 1.0 - pos[None, :]) * lg[:, None])
    cd = np.exp(chunk * lg)
    return inner, qd, kd, cd


def _ret_prompt_body(q_ref, k_ref, v_ref, g_ref, inner_ref, qd_ref, kd_ref, gng_ref, gnb_ref,
                     o_ref, sfin_ref, s_scr, *, cdec):
    t = pl.program_id(1)

    @pl.when(t == 0)
    def _():
        s_scr[...] = jnp.zeros_like(s_scr)

    for c in range(RET_TILE // RET_CHUNK):
        rows = slice(c * RET_CHUNK, (c + 1) * RET_CHUNK)
        for h in range(RET_HEADS):
            qk = slice(h * RET_DK, (h + 1) * RET_DK)
            vv = slice(h * RET_DV, (h + 1) * RET_DV)
            q = q_ref[rows, qk]
            k = k_ref[rows, qk] * (RET_DK ** -0.5)
            v = v_ref[rows, vv]
            scores = lax.dot_general(q, k, (((1,), (1,)), ((), ())),
                                     preferred_element_type=F32) * inner_ref[h]
            s_old = s_scr[h]
            o = (jnp.dot(scores.astype(BF16), v, preferred_element_type=F32)
                 + jnp.dot(q, s_old.astype(BF16), preferred_element_type=F32) * qd_ref[h])
            kdec = (k.astype(F32) * kd_ref[h]).astype(BF16)
            upd = lax.dot_general(kdec, v, (((0,), (0,)), ((), ())), preferred_element_type=F32)
            s_scr[h] = s_old * cdec[h] + upd
            gated = _group_norm_gate(o, g_ref[rows, vv].astype(F32), gng_ref[:, vv], gnb_ref[:, vv])
            o_ref[rows, vv] = gated.astype(o_ref.dtype)

    @pl.when(t == pl.num_programs(1) - 1)
    def _():
        sfin_ref[0] = s_scr[...]


def _ret_prompt(qkvg, gn_g, gn_b, n_seq, seq):
    inner, qd, kd, cd = _ret_decay_tables(RET_CHUNK)
    inner = jnp.asarray(inner, F32)
    qd_b = jnp.asarray(np.broadcast_to(qd[:, :, None], (RET_HEADS, RET_CHUNK, RET_DV)), F32)
    kd_b = jnp.asarray(np.broadcast_to(kd[:, :, None], (RET_HEADS, RET_CHUNK, RET_DK)), F32)
    cdec = tuple(float(np.float32(c)) for c in cd)
    tiles = seq // RET_TILE
    vw = RET_HEADS * RET_DV
    qw = RET_HEADS * RET_DK
    return pl.pallas_call(
        functools.partial(_ret_prompt_body, cdec=cdec),
        grid=(n_seq, tiles),
        in_specs=[
            pl.BlockSpec((RET_TILE, qw), lambda n, t: (n * tiles + t, 0)),
            pl.BlockSpec((RET_TILE, qw), lambda n, t: (n * tiles + t, 1)),
            pl.BlockSpec((RET_TILE, vw), lambda n, t: (n * tiles + t, 1)),
            pl.BlockSpec((RET_TILE, vw), lambda n, t: (n * tiles + t, 2)),
            _resident((RET_HEADS, RET_CHUNK, RET_CHUNK), lambda n, t: (0, 0, 0)),
            _resident((RET_HEADS, RET_CHUNK, RET_DV), lambda n, t: (0, 0, 0)),
            _resident((RET_HEADS, RET_CHUNK, RET_DK), lambda n, t: (0, 0, 0)),
            _resident((1, vw), lambda n, t: (0, 0)),
            _resident((1, vw), lambda n, t: (0, 0)),
        ],
        out_specs=[
            pl.BlockSpec((RET_TILE, vw), lambda n, t: (n * tiles + t, 0)),
            pl.BlockSpec((1, RET_HEADS, RET_DK, RET_DV), lambda n, t: (n, 0, 0, 0)),
        ],
        out_shape=[
            jax.ShapeDtypeStruct((n_seq * seq, vw), BF16),
            jax.ShapeDtypeStruct((n_seq, RET_HEADS, RET_DK, RET_DV), F32),
        ],
        scratch_shapes=[pltpu.VMEM((RET_HEADS, RET_DK, RET_DV), F32)],
        compiler_params=_params("arbitrary", "arbitrary"),
        name="retention_prompt",
    )(qkvg, qkvg, qkvg, qkvg, inner, qd_b, kd_b, gn_g.reshape(1, vw), gn_b.reshape(1, vw))


RET_SEQ_GROUP = SUBLANES
RET_S_ROWS = T_SAMPLE * RET_SEQ_GROUP


def _ret_sample_body(q_ref, k_ref, v_ref, g_ref, s_ref, w_ref, qd_ref, kd_ref, cd_ref,
                     gng_ref, gnb_ref, o_ref, so_ref):
    q = q_ref[...].reshape(RET_S_ROWS, RET_DK).astype(BF16)
    k = k_ref[...].reshape(RET_S_ROWS, RET_DK) * (RET_DK ** -0.5)
    v = v_ref[...].reshape(RET_S_ROWS, RET_DV).astype(BF16)
    g = g_ref[...].reshape(RET_S_ROWS, RET_DV)
    scores = lax.dot_general(q, k.astype(BF16), (((1,), (1,)), ((), ())),
                             preferred_element_type=F32) * w_ref[0]
    o_intra = jnp.dot(scores.astype(BF16), v, preferred_element_type=F32)
    kdec = k * kd_ref[0]
    seq_of_row = lax.broadcasted_iota(jnp.int32, (RET_S_ROWS, 1), 0) % RET_SEQ_GROUP
    o_inter = jnp.zeros((RET_S_ROWS, RET_DV), F32)
    for j in range(RET_SEQ_GROUP):
        mine = seq_of_row == j
        s_old = s_ref[j, 0]
        r = jnp.dot(q, s_old.astype(BF16), preferred_element_type=F32)
        o_inter = jnp.where(mine, r, o_inter)
        kj = jnp.where(mine, kdec, 0.0).astype(BF16)
        upd = lax.dot_general(kj, v, (((0,), (0,)), ((), ())), preferred_element_type=F32)
        so_ref[j, 0] = s_old * cd_ref[0] + upd
    o = o_intra + o_inter * qd_ref[0]
    gated = _group_norm_gate(o, g, gng_ref[...], gnb_ref[...])
    o_ref[...] = gated.reshape(T_SAMPLE, RET_SEQ_GROUP, RET_DV).astype(o_ref.dtype)


def _ret_sample(qkvg, state, gn_g, gn_b):
    inner, qd, kd, cd = _ret_decay_tables(T_SAMPLE)
    row_t = np.arange(RET_S_ROWS) // RET_SEQ_GROUP
    row_j = np.arange(RET_S_ROWS) % RET_SEQ_GROUP
    same = (row_j[:, None] == row_j[None, :])
    w = np.where(same[None], inner[:, row_t[:, None], row_t[None, :]], 0.0)
    w = jnp.asarray(w, F32)
    qd_b = jnp.asarray(np.broadcast_to(qd[:, row_t, None], (RET_HEADS, RET_S_ROWS, RET_DV)), F32)
    kd_b = jnp.asarray(np.broadcast_to(kd[:, row_t, None], (RET_HEADS, RET_S_ROWS, RET_DK)), F32)
    cd_b = jnp.asarray(np.broadcast_to(cd[:, None, None], (RET_HEADS, 1, RET_DV)), F32)
    vw = RET_HEADS * RET_DV
    nq = RET_HEADS
    groups = N_SAMPLE // RET_SEQ_GROUP
    tg = (T_SAMPLE, RET_SEQ_GROUP)
    return pl.pallas_call(
        _ret_sample_body,
        grid=(groups, RET_HEADS),
        in_specs=[
            pl.BlockSpec(tg + (RET_DK,), lambda m, h: (0, m, h)),
            pl.BlockSpec(tg + (RET_DK,), lambda m, h: (0, m, nq + h)),
            pl.BlockSpec(tg + (RET_DV,), lambda m, h: (0, m, nq + h)),
            pl.BlockSpec(tg + (RET_DV,), lambda m, h: (0, m, 2 * nq + h)),
            pl.BlockSpec((RET_SEQ_GROUP, 1, RET_DK, RET_DV), lambda m, h: (m, h, 0, 0)),
            pl.BlockSpec((1, RET_S_ROWS, RET_S_ROWS), lambda m, h: (h, 0, 0)),
            pl.BlockSpec((1, RET_S_ROWS, RET_DV), lambda m, h: (h, 0, 0)),
            pl.BlockSpec((1, RET_S_ROWS, RET_DK), lambda m, h: (h, 0, 0)),
            pl.BlockSpec((1, 1, RET_DV), lambda m, h: (h, 0, 0)),
            pl.BlockSpec((1, RET_DV), lambda m, h: (0, h)),
            pl.BlockSpec((1, RET_DV), lambda m, h: (0, h)),
        ],
        out_specs=[
            pl.BlockSpec(tg + (RET_DV,), lambda m, h: (0, m, h)),
            pl.BlockSpec((RET_SEQ_GROUP, 1, RET_DK, RET_DV), lambda m, h: (m, h, 0, 0)),
        ],
        out_shape=[
            jax.ShapeDtypeStruct((T_SAMPLE, N_SAMPLE, vw), BF16),
            jax.ShapeDtypeStruct(state.shape, F32),
        ],
        compiler_params=_params("arbitrary", "arbitrary"),
        name="retention_sample",
    )(qkvg, qkvg, qkvg, qkvg, state, w, qd_b, kd_b, cd_b,
      gn_g.reshape(1, vw), gn_b.reshape(1, vw))


ATT_BQ = 256
ATT_SLAB = 256
HEADS_PER_SLAB = ATT_SLAB // ATT_DH


def _band_body(q_ref, kc_ref, kp_ref, vc_ref, vp_ref, o_ref, lse_ref, *, dil, slopes):
    i = pl.program_id(2)
    qi = lax.broadcasted_iota(jnp.int32, (NBACK, 2 * NBACK), 0)
    ki = lax.broadcasted_iota(jnp.int32, (NBACK, 2 * NBACK), 1)
    steps_i = NBACK + qi - ki
    steps = steps_i.astype(F32)
    band = (steps_i >= 0) & (steps_i <= NBACK)
    first = band & (ki >= jnp.where(i > 0, 0, NBACK))
    lane_head = lax.broadcasted_iota(jnp.int32, (1, ATT_SLAB), 1) // ATT_DH
    head_masks = [lane_head == hh for hh in range(HEADS_PER_SLAB)]
    head_masks_bf = [jnp.where(mk, 1.0, 0.0).astype(BF16) for mk in head_masks]
    for s in range(ATT_BQ // NBACK):
        rows = slice(s * NBACK, (s + 1) * NBACK)
        q = q_ref[0, rows, :]
        if s == 0:
            k_win = jnp.concatenate([kp_ref[0], kc_ref[0, 0:NBACK, :]], axis=0)
            v_win = jnp.concatenate([vp_ref[0], vc_ref[0, 0:NBACK, :]], axis=0)
            valid = first
        else:
            k_win = kc_ref[0, (s - 1) * NBACK:(s + 1) * NBACK, :]
            v_win = vc_ref[0, (s - 1) * NBACK:(s + 1) * NBACK, :]
            valid = band
        for sl in range(ATT_W // ATT_SLAB):
            lanes = slice(sl * ATT_SLAB, (sl + 1) * ATT_SLAB)
            qs, ks, vs = q[:, lanes], k_win[:, lanes], v_win[:, lanes]
            o_acc = jnp.zeros((NBACK, ATT_SLAB), F32)
            lse_acc = jnp.zeros((NBACK, ATT_SLAB), F32)
            for hh in range(HEADS_PER_SLAB):
                mine = head_masks[hh]
                qm = qs * head_masks_bf[hh]
                sc = lax.dot_general(qm, ks, (((1,), (1,)), ((), ())), preferred_element_type=F32)
                slope = slopes[sl * HEADS_PER_SLAB + hh] * dil
                sc = sc * ATT_SCALE - slope * steps
                sc = jnp.where(valid, sc, -jnp.inf)
                m = jnp.max(sc, -1, keepdims=True)
                e = jnp.exp(sc - m)
                den = jnp.sum(e, -1, keepdims=True)
                p = (e / den).astype(BF16)
                oh = jnp.dot(p, vs, preferred_element_type=F32)
                o_acc = jnp.where(mine, oh, o_acc)
                lse_acc = jnp.where(mine, m + jnp.log(den), lse_acc)
            o_ref[0, rows, lanes] = o_acc
            lse_ref[0, rows, lanes] = lse_acc


def _band_attention(qkv, group, n_seq, seq):
    _, dil = ATT_GROUPS[group]
    length = seq // dil
    n_col = 3 * N_GROUPS
    view = qkv.reshape(n_seq, length, dil * n_col * ATT_W)
    nsub = ATT_BQ // NBACK
    slopes = tuple(float(s) for s in _alibi_slopes()[group])
    blocks = length // ATT_BQ
    qcol = lambda r: r * n_col + group
    kcol = lambda r: r * n_col + N_GROUPS + group
    vcol = lambda r: r * n_col + 2 * N_GROUPS + group
    prev = lambda i: jnp.maximum(i * nsub - 1, 0)
    o, lse = pl.pallas_call(
        functools.partial(_band_body, dil=float(dil), slopes=slopes),
        grid=(n_seq, dil, blocks),
        in_specs=[
            pl.BlockSpec((1, ATT_BQ, ATT_W), lambda n, r, i: (n, i, qcol(r))),
            pl.BlockSpec((1, ATT_BQ, ATT_W), lambda n, r, i: (n, i, kcol(r))),
            pl.BlockSpec((1, NBACK, ATT_W), lambda n, r, i: (n, prev(i), kcol(r))),
            pl.BlockSpec((1, ATT_BQ, ATT_W), lambda n, r, i: (n, i, vcol(r))),
            pl.BlockSpec((1, NBACK, ATT_W), lambda n, r, i: (n, prev(i), vcol(r))),
        ],
        out_specs=[
            pl.BlockSpec((1, ATT_BQ, ATT_W), lambda n, r, i: (n, i, r)),
            pl.BlockSpec((1, ATT_BQ, ATT_W), lambda n, r, i: (n, i, r)),
        ],
        out_shape=[
            jax.ShapeDtypeStruct((n_seq, length, dil * ATT_W), F32),
            jax.ShapeDtypeStruct((n_seq, length, dil * ATT_W), F32),
        ],
        compiler_params=_params("arbitrary", "arbitrary", "arbitrary"),
        name="band_attention",
    )(view, view, view, view, view)
    return o.reshape(n_seq * seq, ATT_W), lse.reshape(n_seq * seq, ATT_W)


ATT_S_KEYS = NBACK + SUBLANES
HEAD_REP = LANES // ATT_HPG


def _att_sample_tables():
    slopes = _alibi_slopes()
    p = np.arange(ATT_S_KEYS)
    bias = np.full((N_GROUPS, T_SAMPLE, ATT_S_KEYS, LANES), -np.inf)
    head_of_lane = np.arange(LANES) // HEAD_REP
    for g, (_, dil) in enumerate(ATT_GROUPS):
        for t in range(T_SAMPLE):
            if dil == 1:
                j = NBACK + t - p
                valid = (j >= 0) & (j <= NBACK) & (p < NBACK + T_SAMPLE)
            else:
                j = np.where(p < NBACK, NBACK - p, 0)
                valid = (p < NBACK) | (p == NBACK + t)
            b = -(slopes[g][head_of_lane][None, :] * dil) * j[:, None]
            bias[g, t] = np.where(valid[:, None], b, -np.inf)
    return bias.reshape(N_GROUPS * T_SAMPLE, ATT_S_KEYS, LANES)


def _att_sample_body(qkv_ref, c0_ref, c1_ref, c2_ref, bias_ref, e_ref, et_ref, o_ref):
    qkv = qkv_ref[0]
    caches = (c0_ref, c1_ref, c2_ref)
    pad = jnp.zeros((SUBLANES - T_SAMPLE, ATT_W), F32)
    probs, lses, values = [], [], []
    for g, (_, dil) in enumerate(ATT_GROUPS):
        q = qkv[:, g * ATT_W:(g + 1) * ATT_W]
        k_new = jnp.concatenate([qkv[:, (N_GROUPS + g) * ATT_W:(N_GROUPS + g + 1) * ATT_W], pad], 0)
        v_new = jnp.concatenate(
            [qkv[:, (2 * N_GROUPS + g) * ATT_W:(2 * N_GROUPS + g + 1) * ATT_W], pad], 0)
        for t in range(T_SAMPLE):
            r = 0 if dil == 1 else t
            k_all = jnp.concatenate(
                [caches[g][0, :, (2 * r) * ATT_W:(2 * r + 1) * ATT_W], k_new], axis=0)
            v_all = jnp.concatenate(
                [caches[g][0, :, (2 * r + 1) * ATT_W:(2 * r + 2) * ATT_W], v_new], axis=0)
            prod = (k_all * q[t:t + 1, :]).astype(BF16)
            sc = jnp.dot(prod, e_ref[...], preferred_element_type=F32) * ATT_SCALE
            sc = sc + bias_ref[g * T_SAMPLE + t]
            m = jnp.max(sc, 0, keepdims=True)
            e = jnp.exp(sc - m)
            den = jnp.sum(e, 0, keepdims=True)
            probs.append(e / den)
            lses.append(m + jnp.log(den))
            values.append(v_all)
    outs = []
    for t in range(T_SAMPLE):
        ls = [lses[g * T_SAMPLE + t] for g in range(N_GROUPS)]
        m = jnp.maximum(jnp.maximum(ls[0], ls[1]), ls[2])
        ws = [jnp.exp(l - m) for l in ls]
        den = ws[0] + ws[1] + ws[2]
        acc = jnp.zeros((1, ATT_W), F32)
        for g in range(N_GROUPS):
            i = g * T_SAMPLE + t
            pw = (probs[i] * (ws[g] / den)).astype(BF16)
            pexp = jnp.dot(pw, et_ref[...], preferred_element_type=F32)
            acc = acc + jnp.sum(pexp * values[i], 0, keepdims=True)
        outs.append(acc)
    o_ref[0] = jnp.concatenate(outs, axis=0)


def _att_sample(qkv, caches):
    views = []
    specs = []
    for g, (window, dil) in enumerate(ATT_GROUPS):
        views.append(caches[g].reshape(N_SAMPLE, NBACK, dil * 2 * ATT_W))
        used = min(dil, T_SAMPLE)
        specs.append(pl.BlockSpec((1, NBACK, used * 2 * ATT_W), lambda n: (n, 0, 0)))
    bias = jnp.asarray(_att_sample_tables(), F32)
    lane_head = np.arange(ATT_W) // ATT_DH
    rep_head = np.arange(LANES) // HEAD_REP
    e = (lane_head[:, None] == rep_head[None, :]).astype(np.float32)
    et = e.T / HEAD_REP
    return pl.pallas_call(
        _att_sample_body,
        grid=(N_SAMPLE,),
        in_specs=[pl.BlockSpec((1, T_SAMPLE, 3 * N_GROUPS * ATT_W), lambda n: (n, 0, 0))] + specs + [
            _resident(bias.shape, lambda n: (0, 0, 0)),
            _resident((ATT_W, LANES), lambda n: (0, 0)),
            _resident((LANES, ATT_W), lambda n: (0, 0)),
        ],
        out_specs=pl.BlockSpec((1, T_SAMPLE, ATT_W), lambda n: (n, 0, 0)),
        out_shape=jax.ShapeDtypeStruct((N_SAMPLE, T_SAMPLE, ATT_W), F32),
        compiler_params=_params("arbitrary"),
        name="attention_sample",
    )(qkv, *views, bias, jnp.asarray(e, BF16), jnp.asarray(et, BF16))


def _split_kv(kv, keep):
    n, rows, _ = kv.shape
    out = []
    for g in range(N_GROUPS):
        k = kv[:, rows - keep[g]:, g * ATT_W:(g + 1) * ATT_W]
        v = kv[:, rows - keep[g]:, (N_GROUPS + g) * ATT_W:(N_GROUPS + g + 1) * ATT_W]
        out.append(jnp.stack([k, v], axis=2).reshape(n, keep[g], 2, ATT_HPG, ATT_DH))
    return out


def kernel(x_prompt, x_sample, c_prompt, c_sample, state_ret, cache_kv_w128, cache_kv_w512,
           cache_kv_w2048, w_ada, b_ada, ln_g, ln_b, w_ffn_in, w_ffn_out, w_ret_in, ret_gn_g,
           ret_gn_b, w_ret_out, w_att_in, w_att_out):
    d = D_MODEL
    n_p, seq, _ = x_prompt.shape
    w_ffn_in_b = w_ffn_in.astype(BF16)
    w_ffn_out_b = w_ffn_out.astype(BF16)
    w_ret_in_b = w_ret_in.astype(BF16)
    w_ret_out_b = w_ret_out.astype(BF16)
    w_att_in_b = w_att_in.astype(BF16)
    w_att_out_b = w_att_out.astype(BF16)

    c_all = jnp.concatenate(
        [c_sample, c_prompt, jnp.zeros((ADA_ROWS - N_SAMPLE - n_p, d), F32)], axis=0)
    ada = _ada_table(c_all, w_ada, b_ada)

    prompt = _Trunk(n_p * seq, 512, False, seq // 512)
    sample = _Trunk(N_SAMPLE * T_SAMPLE, N_SAMPLE, True, 1)
    xp = x_prompt.reshape(n_p * seq, d)
    xs = x_sample.transpose(1, 0, 2).reshape(T_SAMPLE * N_SAMPLE, d)

    def ffn(x, trunk, layer, which):
        sub = 0 if which == 0 else 2
        return _ffn(x, ada, w_ffn_in_b[layer, which], w_ffn_out_b[layer, which],
                    ln_g[layer, sub], ln_b[layer, sub], trunk, layer, sub)

    xp = ffn(xp, prompt, 0, 0)
    xs = ffn(xs, sample, 0, 0)

    qkvg_p = _inproj(xp, ada, w_ret_in_b, prompt, 0, BF16)
    gated_p, ret_p = _ret_prompt(qkvg_p, ret_gn_g, ret_gn_b, n_p, seq)
    xp = _outproj((gated_p,), xp, ada, w_ret_out_b, ln_g[0, 1], ln_b[0, 1], prompt, 0)

    qkvg_s = _inproj(xs, ada, w_ret_in_b, sample, 0, F32)
    gated_s, ret_s = _ret_sample(qkvg_s.reshape(T_SAMPLE, N_SAMPLE, -1), state_ret,
                                 ret_gn_g, ret_gn_b)
    xs = _outproj((gated_s.reshape(T_SAMPLE * N_SAMPLE, -1),), xs, ada, w_ret_out_b,
                  ln_g[0, 1], ln_b[0, 1], sample, 0)

    xp = ffn(xp, prompt, 0, 1)
    xs = ffn(xs, sample, 0, 1)

    xp = ffn(xp, prompt, 1, 0)
    xs = ffn(xs, sample, 1, 0)

    keep_p = [min(w, seq) for w, _ in ATT_GROUPS]
    qkv_p, kv_tail = _inproj(xp, ada, w_att_in_b, prompt, 1, BF16,
                             tail=(N_GROUPS * ATT_W, max(keep_p)))
    outs = [_band_attention(qkv_p, g, n_p, seq) for g in range(N_GROUPS)]
    acts = tuple(o for o, _ in outs) + tuple(l for _, l in outs)
    xp = _outproj(acts, xp, ada, w_att_out_b, ln_g[1, 1], ln_b[1, 1], prompt, 1)
    kv_p = _split_kv(kv_tail, keep_p)

    qkv_s = _inproj(xs, ada, w_att_in_b, sample, 1, F32)
    qkv_s = qkv_s.reshape(T_SAMPLE, N_SAMPLE, -1).transpose(1, 0, 2)
    att_s = _att_sample(qkv_s, (cache_kv_w128, cache_kv_w512, cache_kv_w2048))
    att_s = att_s.transpose(1, 0, 2).reshape(T_SAMPLE * N_SAMPLE, ATT_W)
    xs = _outproj((att_s,), xs, ada, w_att_out_b, ln_g[1, 1], ln_b[1, 1], sample, 1)
    kv_s = _split_kv(qkv_s[:, :, N_GROUPS * ATT_W:], [T_SAMPLE] * N_GROUPS)

    xp = ffn(xp, prompt, 1, 1)
    xs = ffn(xs, sample, 1, 1)

    y_prompt = xp.reshape(n_p, seq, d)
    y_sample = xs.reshape(T_SAMPLE, N_SAMPLE, d).transpose(1, 0, 2)
    return (y_prompt, y_sample, ret_p, ret_s, kv_p[0], kv_s[0], kv_p[1], kv_s[1], kv_p[2], kv_s[2])
```

```python
import functools

import numpy as np
import jax
import jax.numpy as jnp
from jax import lax
from jax.experimental import pallas as pl
from jax.experimental.pallas import tpu as pltpu

F32 = jnp.float32
BF16 = jnp.bfloat16

D_MODEL = 1024
DEPTH = 2
D_FF = 2816
RET_HEADS = 4
RET_DK = 256
RET_DV = 512
RET_CHUNK = 128
ATT_GROUPS = ((128, 1), (512, 4), (2048, 16))
N_GROUPS = 3
ATT_HPG = 8
ATT_DH = 64
ATT_W = ATT_HPG * ATT_DH
NBACK = 128
ALPHA = (2 * DEPTH) ** 0.25
NORM_EPS = 1e-5
ATT_SCALE = ATT_DH ** -0.5

V7X_VMEM_LIMIT_BYTES = 60000 * 1024
LANES = 128
SUBLANES = 8

N_SAMPLE = 128
T_SAMPLE = 4
ADA_ROWS = 136
PROMPT_ROW_BLOCK = N_SAMPLE // SUBLANES

NT_DIMS = (((1,), (1,)), ((), ()))
TN_DIMS = (((0,), (0,)), ((), ()))


def _params(*semantics):
    return pltpu.CompilerParams(dimension_semantics=semantics,
                                vmem_limit_bytes=V7X_VMEM_LIMIT_BYTES)


def _resident(block, index_map):
    return pl.BlockSpec(block, index_map, pipeline_mode=pl.Buffered(1))


def _layer_norm(z, g, b):
    mu = jnp.mean(z, -1, keepdims=True)
    zc = z - mu
    var = jnp.mean(zc * zc, -1, keepdims=True)
    return zc * lax.rsqrt(var + NORM_EPS) * g + b


def _log_gamma():
    return np.log1p(-(2.0 ** (-5.0 - np.arange(RET_HEADS, dtype=np.float64))))


def _alibi_slopes():
    h = np.arange(1, N_GROUPS * ATT_HPG + 1, dtype=np.float64)
    return (2.0 ** (-8.0 * h / (N_GROUPS * ATT_HPG))).reshape(N_GROUPS, ATT_HPG)


def _ada_body(c_ref, w_ref, b_ref, o_ref):
    s = jax.nn.silu(c_ref[...]).astype(BF16)
    w = w_ref[0].astype(BF16)
    o_ref[0, 0] = jnp.dot(s, w, preferred_element_type=F32) + b_ref[0, 0]


def _ada_table(c_all, w_ada, b_ada):
    d = D_MODEL
    return pl.pallas_call(
        _ada_body,
        grid=(DEPTH, 9),
        in_specs=[
            _resident((ADA_ROWS, d), lambda i, j: (0, 0)),
            pl.BlockSpec((1, d, d), lambda i, j: (i, 0, j)),
            pl.BlockSpec((1, 1, 1, d), lambda i, j: (i, j, 0, 0)),
        ],
        out_specs=pl.BlockSpec((1, 1, ADA_ROWS, d), lambda i, j: (i, j, 0, 0)),
        out_shape=jax.ShapeDtypeStruct((DEPTH, 9, ADA_ROWS, d), F32),
        compiler_params=_params("arbitrary", "arbitrary"),
        name="ada_table",
    )(c_all, w_ada, b_ada.reshape(DEPTH, 9, 1, d))


class _Trunk:
    def __init__(self, tokens, tile, per_row, tiles_per_seq):
        self.tokens = tokens
        self.tile = tile
        self.per_row = per_row
        self.tiles_per_seq = tiles_per_seq

    def mod_spec(self, layer, sub):
        if self.per_row:
            return pl.BlockSpec((1, 3, self.tile, D_MODEL), lambda i: (layer, sub, 0, 0))
        return pl.BlockSpec((1, 3, SUBLANES, D_MODEL),
                            lambda i: (layer, sub, PROMPT_ROW_BLOCK, 0))

    def mod_rows(self, mod_ref, k):
        if self.per_row:
            return mod_ref[0, k]
        n = pl.program_id(0) // self.tiles_per_seq
        return mod_ref[0, k, pl.ds(n, 1), :]


def _ffn_body(x_ref, mod_ref, win_ref, wout_ref, g_ref, b_ref, o_ref, *, trunk):
    x = x_ref[...]
    shift = trunk.mod_rows(mod_ref, 0)
    scale = trunk.mod_rows(mod_ref, 1)
    gate = trunk.mod_rows(mod_ref, 2)
    u = (x * (1.0 + scale) + shift).astype(BF16)
    h = jnp.dot(u, win_ref[...], preferred_element_type=F32)
    a = h[:, :D_FF]
    b = h[:, D_FF:]
    act = (jax.nn.silu(a) * b).astype(BF16)
    y = jnp.dot(act, wout_ref[...], preferred_element_type=F32)
    z = ALPHA * x + (0.5 * (1.0 + gate)) * y
    o_ref[...] = _layer_norm(z, g_ref[...], b_ref[...])


def _ffn(x, ada, w_in, w_out, ln_g, ln_b, trunk, layer, sub):
    d = D_MODEL
    tm = trunk.tile
    return pl.pallas_call(
        functools.partial(_ffn_body, trunk=trunk),
        grid=(trunk.tokens // tm,),
        in_specs=[
            pl.BlockSpec((tm, d), lambda i: (i, 0)),
            trunk.mod_spec(layer, sub),
            _resident((d, 2 * D_FF), lambda i: (0, 0)),
            _resident((D_FF, d), lambda i: (0, 0)),
            _resident((1, d), lambda i: (0, 0)),
            _resident((1, d), lambda i: (0, 0)),
        ],
        out_specs=pl.BlockSpec((tm, d), lambda i: (i, 0)),
        out_shape=jax.ShapeDtypeStruct((trunk.tokens, d), F32),
        compiler_params=_params("arbitrary"),
        name="ffn",
    )(x, ada, w_in, w_out, ln_g.reshape(1, d), ln_b.reshape(1, d))


def _inproj_body(x_ref, mod_ref, w_ref, o_ref, *, trunk):
    x = x_ref[...]
    u = (x * (1.0 + trunk.mod_rows(mod_ref, 1)) + trunk.mod_rows(mod_ref, 0)).astype(BF16)
    o_ref[...] = jnp.dot(u, w_ref[...], preferred_element_type=F32).astype(o_ref.dtype)


def _inproj(x, ada, w, trunk, layer, out_dtype):
    d = D_MODEL
    tm = trunk.tile
    n_out = w.shape[1]
    return pl.pallas_call(
        functools.partial(_inproj_body, trunk=trunk),
        grid=(trunk.tokens // tm,),
        in_specs=[
            pl.BlockSpec((tm, d), lambda i: (i, 0)),
            trunk.mod_spec(layer, 1),
            _resident((d, n_out), lambda i: (0, 0)),
        ],
        out_specs=pl.BlockSpec((tm, n_out), lambda i: (i, 0)),
        out_shape=jax.ShapeDtypeStruct((trunk.tokens, n_out), out_dtype),
        compiler_params=_params("arbitrary"),
        name="inproj",
    )(x, ada, w)


def _att_inproj_body(x_ref, mod_ref, w_ref, a0_ref, a1_ref, a2_ref, tail_ref, y_scr, *, trunk):
    x = x_ref[...]
    u = (x * (1.0 + trunk.mod_rows(mod_ref, 1)) + trunk.mod_rows(mod_ref, 0)).astype(BF16)
    y = jnp.dot(u, w_ref[...], preferred_element_type=F32)
    tail_ref[0] = y[:, N_GROUPS * ATT_W:]
    for cb in range(y.shape[1] // LANES):
        y_scr[cb] = y[:, cb * LANES:(cb + 1) * LANES]
    slabs = ATT_W // LANES
    for g, a_ref in enumerate((a0_ref, a1_ref, a2_ref)):
        dil = ATT_GROUPS[g][1]
        rows = trunk.tile // dil
        for r in range(dil):
            sel = pl.ds(r, rows, stride=dil) if dil > 1 else slice(None)
            for part in range(3):
                for cb in range(slabs):
                    src = ((part * N_GROUPS + g) * slabs + cb)
                    dst = (part * slabs + cb) * LANES
                    a_ref[0, r, :, dst:dst + LANES] = y_scr[src, sel, :].astype(BF16)


def _att_inproj_prompt(x, ada, w, trunk, layer, tail_rows):
    d = D_MODEL
    tm = trunk.tile
    n_out = w.shape[1]
    tps = trunk.tiles_per_seq
    n_seq = trunk.tokens // (tps * tm)
    first = tps - tail_rows // tm
    out_shape, out_specs = [], []
    for _, dil in ATT_GROUPS:
        out_shape.append(jax.ShapeDtypeStruct((n_seq, dil, tps * tm // dil, 3 * ATT_W), BF16))
        out_specs.append(pl.BlockSpec((1, dil, tm // dil, 3 * ATT_W),
                                      lambda i: (i // tps, 0, i % tps, 0)))
    kv_w = n_out - N_GROUPS * ATT_W
    out_shape.append(jax.ShapeDtypeStruct((n_seq, tail_rows, kv_w), F32))
    out_specs.append(pl.BlockSpec((1, tm, kv_w),
                                  lambda i: (i // tps, jnp.maximum(i % tps - first, 0), 0)))
    return pl.pallas_call(
        functools.partial(_att_inproj_body, trunk=trunk),
        grid=(trunk.tokens // tm,),
        in_specs=[
            pl.BlockSpec((tm, d), lambda i: (i, 0)),
            trunk.mod_spec(layer, 1),
            _resident((d, n_out), lambda i: (0, 0)),
        ],
        out_specs=out_specs,
        out_shape=out_shape,
        scratch_shapes=[pltpu.VMEM((n_out // LANES, tm, LANES), F32)],
        compiler_params=_params("arbitrary"),
        name="att_inproj",
    )(x, ada, w)


def _outproj_tail(a, x_ref, mod_ref, w_ref, g_ref, b_ref, o_ref, trunk):
    x = x_ref[...]
    gate = trunk.mod_rows(mod_ref, 2)
    y = jnp.dot(a.astype(BF16), w_ref[...], preferred_element_type=F32)
    z = ALPHA * x + (1.0 + gate) * y
    o_ref[...] = _layer_norm(z, g_ref[...], b_ref[...])


def _outproj_body(a_ref, x_ref, mod_ref, w_ref, g_ref, b_ref, o_ref, *, trunk):
    _outproj_tail(a_ref[...], x_ref, mod_ref, w_ref, g_ref, b_ref, o_ref, trunk)


def _outproj(a, x, ada, w, ln_g, ln_b, trunk, layer):
    d = D_MODEL
    tm = trunk.tile
    k_in = w.shape[0]
    return pl.pallas_call(
        functools.partial(_outproj_body, trunk=trunk),
        grid=(trunk.tokens // tm,),
        in_specs=[
            pl.BlockSpec((tm, k_in), lambda i: (i, 0)),
            pl.BlockSpec((tm, d), lambda i: (i, 0)),
            trunk.mod_spec(layer, 1),
            _resident((k_in, d), lambda i: (0, 0)),
            _resident((1, d), lambda i: (0, 0)),
            _resident((1, d), lambda i: (0, 0)),
        ],
        out_specs=pl.BlockSpec((tm, d), lambda i: (i, 0)),
        out_shape=jax.ShapeDtypeStruct((trunk.tokens, d), F32),
        compiler_params=_params("arbitrary"),
        name="outproj",
    )(a, x, ada, w, ln_g.reshape(1, d), ln_b.reshape(1, d))


def _att_outproj_body(o0_ref, o1_ref, o2_ref, l0_ref, l1_ref, l2_ref, x_ref, mod_ref, w_ref,
                      g_ref, b_ref, out_ref, tok_scr, *, trunk):
    def token_major(ref, g, slot):
        dil = ATT_GROUPS[g][1]
        if dil == 1:
            return ref[0, 0]
        rows = trunk.tile // dil
        slabs = ATT_W // LANES
        for r in range(dil):
            for cb in range(slabs):
                tok_scr[slot, cb, pl.ds(r, rows, stride=dil), :] = (
                    ref[0, r, :, cb * LANES:(cb + 1) * LANES])
        return jnp.concatenate([tok_scr[slot, cb] for cb in range(slabs)], axis=1)

    os = [token_major(r, g, g) for g, r in enumerate((o0_ref, o1_ref, o2_ref))]
    ls = [token_major(r, g, N_GROUPS + g) for g, r in enumerate((l0_ref, l1_ref, l2_ref))]
    m = jnp.maximum(jnp.maximum(ls[0], ls[1]), ls[2])
    ws = [jnp.exp(l - m) for l in ls]
    a = (ws[0] * os[0] + ws[1] * os[1] + ws[2] * os[2]) / (ws[0] + ws[1] + ws[2])
    _outproj_tail(a, x_ref, mod_ref, w_ref, g_ref, b_ref, out_ref, trunk)


def _att_outproj_prompt(outs, lses, x, ada, w, ln_g, ln_b, trunk, layer):
    d = D_MODEL
    tm = trunk.tile
    tps = trunk.tiles_per_seq
    k_in = w.shape[0]
    cls_specs = [pl.BlockSpec((1, dil, tm // dil, ATT_W), lambda i: (i // tps, 0, i % tps, 0))
                 for _, dil in ATT_GROUPS]
    return pl.pallas_call(
        functools.partial(_att_outproj_body, trunk=trunk),
        grid=(trunk.tokens // tm,),
        in_specs=cls_specs + cls_specs + [
            pl.BlockSpec((tm, d), lambda i: (i, 0)),
            trunk.mod_spec(layer, 1),
            _resident((k_in, d), lambda i: (0, 0)),
            _resident((1, d), lambda i: (0, 0)),
            _resident((1, d), lambda i: (0, 0)),
        ],
        out_specs=pl.BlockSpec((tm, d), lambda i: (i, 0)),
        out_shape=jax.ShapeDtypeStruct((trunk.tokens, d), F32),
        scratch_shapes=[pltpu.VMEM((2 * N_GROUPS, ATT_W // LANES, tm, LANES), F32)],
        compiler_params=_params("arbitrary"),
        name="att_outproj",
    )(*outs, *lses, x, ada, w, ln_g.reshape(1, d), ln_b.reshape(1, d))


def _group_norm_gate(o, g, gn_g, gn_b):
    mu = jnp.mean(o, -1, keepdims=True)
    oc = o - mu
    var = jnp.mean(oc * oc, -1, keepdims=True)
    on = oc * lax.rsqrt(var + NORM_EPS) * gn_g + gn_b
    return jax.nn.silu(g) * on


RET_TILE = 512


def _ret_decay_tables(chunk):
    lg = _log_gamma()
    pos = np.arange(chunk, dtype=np.float64)
    diff = pos[:, None] - pos[None, :]
    inner = np.where(diff >= 0, np.exp(np.maximum(diff, 0.0)[None] * lg[:, None, None]), 0.0)
    qd = np.exp((pos[None, :] + 1.0) * lg[:, None])
    kd = np.exp((chunk - 1.0 - pos[None, :]) * lg[:, None])
    cd = np.exp(chunk * lg)
    return inner, qd, kd, cd


def _ret_prompt_body(q_ref, k_ref, v_ref, g_ref, inner_ref, qd_ref, kd_ref, gng_ref, gnb_ref,
                     o_ref, sfin_ref, s_scr, *, cdec):
    t = pl.program_id(1)

    @pl.when(t == 0)
    def _():
        s_scr[...] = jnp.zeros_like(s_scr)

    for c in range(RET_TILE // RET_CHUNK):
        rows = slice(c * RET_CHUNK, (c + 1) * RET_CHUNK)
        for h in range(RET_HEADS):
            qk = slice(h * RET_DK, (h + 1) * RET_DK)
            vv = slice(h * RET_DV, (h + 1) * RET_DV)
            q = q_ref[rows, qk]
            k = k_ref[rows, qk] * (RET_DK ** -0.5)
            v = v_ref[rows, vv]
            scores = lax.dot_general(q, k, NT_DIMS, preferred_element_type=F32) * inner_ref[h]
            s_old = s_scr[h]
            o = (jnp.dot(scores.astype(BF16), v, preferred_element_type=F32)
                 + jnp.dot(q, s_old.astype(BF16), preferred_element_type=F32) * qd_ref[h])
            kdec = (k.astype(F32) * kd_ref[h]).astype(BF16)
            upd = lax.dot_general(kdec, v, TN_DIMS, preferred_element_type=F32)
            s_scr[h] = s_old * cdec[h] + upd
            gated = _group_norm_gate(o, g_ref[rows, vv].astype(F32), gng_ref[:, vv], gnb_ref[:, vv])
            o_ref[rows, vv] = gated.astype(o_ref.dtype)

    @pl.when(t == pl.num_programs(1) - 1)
    def _():
        sfin_ref[0] = s_scr[...]


def _ret_prompt(qkvg, gn_g, gn_b, n_seq, seq):
    inner, qd, kd, cd = _ret_decay_tables(RET_CHUNK)
    inner = jnp.asarray(inner, F32)
    qd_b = jnp.asarray(np.broadcast_to(qd[:, :, None], (RET_HEADS, RET_CHUNK, RET_DV)), F32)
    kd_b = jnp.asarray(np.broadcast_to(kd[:, :, None], (RET_HEADS, RET_CHUNK, RET_DK)), F32)
    cdec = tuple(float(np.float32(c)) for c in cd)
    tiles = seq // RET_TILE
    vw = RET_HEADS * RET_DV
    qw = RET_HEADS * RET_DK
    return pl.pallas_call(
        functools.partial(_ret_prompt_body, cdec=cdec),
        grid=(n_seq, tiles),
        in_specs=[
            pl.BlockSpec((RET_TILE, qw), lambda n, t: (n * tiles + t, 0)),
            pl.BlockSpec((RET_TILE, qw), lambda n, t: (n * tiles + t, 1)),
            pl.BlockSpec((RET_TILE, vw), lambda n, t: (n * tiles + t, 1)),
            pl.BlockSpec((RET_TILE, vw), lambda n, t: (n * tiles + t, 2)),
            _resident((RET_HEADS, RET_CHUNK, RET_CHUNK), lambda n, t: (0, 0, 0)),
            _resident((RET_HEADS, RET_CHUNK, RET_DV), lambda n, t: (0, 0, 0)),
            _resident((RET_HEADS, RET_CHUNK, RET_DK), lambda n, t: (0, 0, 0)),
            _resident((1, vw), lambda n, t: (0, 0)),
            _resident((1, vw), lambda n, t: (0, 0)),
        ],
        out_specs=[
            pl.BlockSpec((RET_TILE, vw), lambda n, t: (n * tiles + t, 0)),
            pl.BlockSpec((1, RET_HEADS, RET_DK, RET_DV), lambda n, t: (n, 0, 0, 0)),
        ],
        out_shape=[
            jax.ShapeDtypeStruct((n_seq * seq, vw), BF16),
            jax.ShapeDtypeStruct((n_seq, RET_HEADS, RET_DK, RET_DV), F32),
        ],
        scratch_shapes=[pltpu.VMEM((RET_HEADS, RET_DK, RET_DV), F32)],
        compiler_params=_params("arbitrary", "arbitrary"),
        name="retention_prompt",
    )(qkvg, qkvg, qkvg, qkvg, inner, qd_b, kd_b, gn_g.reshape(1, vw), gn_b.reshape(1, vw))


RET_SEQ_GROUP = SUBLANES
RET_S_ROWS = T_SAMPLE * RET_SEQ_GROUP


def _ret_sample_body(q_ref, k_ref, v_ref, g_ref, s_ref, w_ref, qd_ref, kd_ref, cd_ref,
                     gng_ref, gnb_ref, o_ref, so_ref):
    q = q_ref[...].reshape(RET_S_ROWS, RET_DK).astype(BF16)
    k = k_ref[...].reshape(RET_S_ROWS, RET_DK) * (RET_DK ** -0.5)
    v = v_ref[...].reshape(RET_S_ROWS, RET_DV).astype(BF16)
    g = g_ref[...].reshape(RET_S_ROWS, RET_DV)
    scores = lax.dot_general(q, k.astype(BF16), NT_DIMS, preferred_element_type=F32) * w_ref[0]
    o_intra = jnp.dot(scores.astype(BF16), v, preferred_element_type=F32)
    kdec = k * kd_ref[0]
    seq_of_row = lax.broadcasted_iota(jnp.int32, (RET_S_ROWS, 1), 0) % RET_SEQ_GROUP
    o_inter = jnp.zeros((RET_S_ROWS, RET_DV), F32)
    for j in range(RET_SEQ_GROUP):
        mine = seq_of_row == j
        s_old = s_ref[j, 0]
        r = jnp.dot(q, s_old.astype(BF16), preferred_element_type=F32)
        o_inter = jnp.where(mine, r, o_inter)
        kj = jnp.where(mine, kdec, 0.0).astype(BF16)
        upd = lax.dot_general(kj, v, TN_DIMS, preferred_element_type=F32)
        so_ref[j, 0] = s_old * cd_ref[0] + upd
    o = o_intra + o_inter * qd_ref[0]
    gated = _group_norm_gate(o, g, gng_ref[...], gnb_ref[...])
    o_ref[...] = gated.reshape(T_SAMPLE, RET_SEQ_GROUP, RET_DV).astype(o_ref.dtype)


def _ret_sample(qkvg, state, gn_g, gn_b):
    inner, qd, kd, cd = _ret_decay_tables(T_SAMPLE)
    row_t = np.arange(RET_S_ROWS) // RET_SEQ_GROUP
    row_j = np.arange(RET_S_ROWS) % RET_SEQ_GROUP
    same = (row_j[:, None] == row_j[None, :])
    w = np.where(same[None], inner[:, row_t[:, None], row_t[None, :]], 0.0)
    w = jnp.asarray(w, F32)
    qd_b = jnp.asarray(np.broadcast_to(qd[:, row_t, None], (RET_HEADS, RET_S_ROWS, RET_DV)), F32)
    kd_b = jnp.asarray(np.broadcast_to(kd[:, row_t, None], (RET_HEADS, RET_S_ROWS, RET_DK)), F32)
    cd_b = jnp.asarray(np.broadcast_to(cd[:, None, None], (RET_HEADS, 1, RET_DV)), F32)
    vw = RET_HEADS * RET_DV
    nq = RET_HEADS
    groups = N_SAMPLE // RET_SEQ_GROUP
    tg = (T_SAMPLE, RET_SEQ_GROUP)
    return pl.pallas_call(
        _ret_sample_body,
        grid=(groups, RET_HEADS),
        in_specs=[
            pl.BlockSpec(tg + (RET_DK,), lambda m, h: (0, m, h)),
            pl.BlockSpec(tg + (RET_DK,), lambda m, h: (0, m, nq + h)),
            pl.BlockSpec(tg + (RET_DV,), lambda m, h: (0, m, nq + h)),
            pl.BlockSpec(tg + (RET_DV,), lambda m, h: (0, m, 2 * nq + h)),
            pl.BlockSpec((RET_SEQ_GROUP, 1, RET_DK, RET_DV), lambda m, h: (m, h, 0, 0)),
            pl.BlockSpec((1, RET_S_ROWS, RET_S_ROWS), lambda m, h: (h, 0, 0)),
            pl.BlockSpec((1, RET_S_ROWS, RET_DV), lambda m, h: (h, 0, 0)),
            pl.BlockSpec((1, RET_S_ROWS, RET_DK), lambda m, h: (h, 0, 0)),
            pl.BlockSpec((1, 1, RET_DV), lambda m, h: (h, 0, 0)),
            pl.BlockSpec((1, RET_DV), lambda m, h: (0, h)),
            pl.BlockSpec((1, RET_DV), lambda m, h: (0, h)),
        ],
        out_specs=[
            pl.BlockSpec(tg + (RET_DV,), lambda m, h: (0, m, h)),
            pl.BlockSpec((RET_SEQ_GROUP, 1, RET_DK, RET_DV), lambda m, h: (m, h, 0, 0)),
        ],
        out_shape=[
            jax.ShapeDtypeStruct((T_SAMPLE, N_SAMPLE, vw), BF16),
            jax.ShapeDtypeStruct(state.shape, F32),
        ],
        compiler_params=_params("arbitrary", "arbitrary"),
        name="retention_sample",
    )(qkvg, qkvg, qkvg, qkvg, state, w, qd_b, kd_b, cd_b,
      gn_g.reshape(1, vw), gn_b.reshape(1, vw))


ATT_BQ = 256
ATT_SLAB = 256
HEADS_PER_SLAB = ATT_SLAB // ATT_DH


def _band_window(cur_ref, prev_ref, s):
    if s == 0:
        return jnp.concatenate([prev_ref[0, 0], cur_ref[0, 0, 0:NBACK, :]], axis=0)
    return cur_ref[0, 0, (s - 1) * NBACK:(s + 1) * NBACK, :]


def _band_body(q_ref, kc_ref, kp_ref, vc_ref, vp_ref, bias_ref, o_ref, lse_ref, s_scr, p_scr):
    i = pl.program_id(2)
    ki = lax.broadcasted_iota(jnp.int32, (NBACK, 2 * NBACK), 1)
    has_prev = ki >= jnp.where(i > 0, 0, NBACK)
    lane_head = lax.broadcasted_iota(jnp.int32, (1, ATT_SLAB), 1) // ATT_DH
    qmasks = [jnp.where(lane_head == hh, ATT_SCALE, 0.0).astype(BF16)
              for hh in range(HEADS_PER_SLAB)]
    nsub = ATT_BQ // NBACK
    slabs = ATT_W // ATT_SLAB
    for s in range(nsub):
        q = q_ref[0, 0, s * NBACK:(s + 1) * NBACK, :]
        k_win = _band_window(kc_ref, kp_ref, s)
        for sl in range(slabs):
            lanes = slice(sl * ATT_SLAB, (sl + 1) * ATT_SLAB)
            for hh in range(HEADS_PER_SLAB):
                idx = (s * slabs + sl) * HEADS_PER_SLAB + hh
                s_scr[idx] = lax.dot_general(q[:, lanes] * qmasks[hh], k_win[:, lanes], NT_DIMS,
                                             preferred_element_type=F32)
    for s in range(nsub):
        rows = slice(s * NBACK, (s + 1) * NBACK)
        for h in range(ATT_HPG):
            idx = s * ATT_HPG + h
            sc = s_scr[idx] + bias_ref[h]
            if s == 0:
                sc = jnp.where(has_prev, sc, -jnp.inf)
            m = jnp.max(sc, -1, keepdims=True)
            e = jnp.exp(sc - m)
            den = jnp.sum(e, -1, keepdims=True)
            p_scr[idx] = (e / den).astype(BF16)
            lse_ref[0, 0, rows, h * ATT_DH:(h + 1) * ATT_DH] = jnp.broadcast_to(
                m + jnp.log(den), (NBACK, ATT_DH))
    for s in range(nsub):
        rows = slice(s * NBACK, (s + 1) * NBACK)
        v_win = _band_window(vc_ref, vp_ref, s)
        for sl in range(slabs):
            lanes = slice(sl * ATT_SLAB, (sl + 1) * ATT_SLAB)
            for hh in range(HEADS_PER_SLAB):
                idx = (s * slabs + sl) * HEADS_PER_SLAB + hh
                oh = jnp.dot(p_scr[idx], v_win[:, lanes], preferred_element_type=F32)
                lo = sl * ATT_SLAB + hh * ATT_DH
                o_ref[0, 0, rows, lo:lo + ATT_DH] = oh[:, hh * ATT_DH:(hh + 1) * ATT_DH]


def _band_bias(group):
    _, dil = ATT_GROUPS[group]
    steps = NBACK + np.arange(NBACK)[:, None] - np.arange(2 * NBACK)[None, :]
    valid = (steps >= 0) & (steps <= NBACK)
    slopes = _alibi_slopes()[group]
    bias = -(slopes[:, None, None] * dil) * steps[None]
    return np.where(valid[None], bias, -np.inf)


def _band_attention(qkv_cls, group):
    _, dil = ATT_GROUPS[group]
    n_seq, _, length, _ = qkv_cls.shape
    nsub = ATT_BQ // NBACK
    blocks = length // ATT_BQ
    pairs = nsub * ATT_HPG
    prev = lambda i: jnp.maximum(i * nsub - 1, 0)
    cur = lambda col: pl.BlockSpec((1, 1, ATT_BQ, ATT_W), lambda n, r, i: (n, r, i, col))
    halo = lambda col: pl.BlockSpec((1, 1, NBACK, ATT_W), lambda n, r, i: (n, r, prev(i), col))
    out_sds = jax.ShapeDtypeStruct((n_seq, dil, length, ATT_W), F32)
    return pl.pallas_call(
        _band_body,
        grid=(n_seq, dil, blocks),
        in_specs=[cur(0), cur(1), halo(1), cur(2), halo(2),
                  _resident((ATT_HPG, NBACK, 2 * NBACK), lambda n, r, i: (0, 0, 0))],
        out_specs=[cur(0), cur(0)],
        out_shape=[out_sds, out_sds],
        scratch_shapes=[pltpu.VMEM((pairs, NBACK, 2 * NBACK), F32),
                        pltpu.VMEM((pairs, NBACK, 2 * NBACK), BF16)],
        compiler_params=_params("arbitrary", "arbitrary", "arbitrary"),
        name="band_attention",
    )(qkv_cls, qkv_cls, qkv_cls, qkv_cls, qkv_cls, jnp.asarray(_band_bias(group), F32))


SA_PASSES = tuple((0, (t,)) for t in range(T_SAMPLE)) + ((1, tuple(range(T_SAMPLE))),
                                                        (2, tuple(range(T_SAMPLE))))
SA_MAX_LB = max(w for w, _ in ATT_GROUPS)


def _att_sample_tables():
    slopes = _alibi_slopes()
    sels, biases = [], []
    bias_new = np.full((N_GROUPS, T_SAMPLE * ATT_HPG, LANES), -np.inf)
    cs = np.zeros((len(SA_PASSES), SUBLANES, LANES))
    lane = np.arange(LANES)
    for g, (lb, dil) in enumerate(ATT_GROUPS):
        passes = [p for p in SA_PASSES if p[0] == g]
        sel = np.zeros((ATT_HPG, T_SAMPLE, len(passes), ATT_HPG, LANES))
        for pi, (_, t_set) in enumerate(passes):
            for t in t_set:
                owns = np.ones(LANES, bool) if dil == 1 else (lane % dil == t)
                for h in range(ATT_HPG):
                    sel[h, t, pi, h, owns] = 1.0
        sels.append(sel.reshape(ATT_HPG * T_SAMPLE, len(passes) * ATT_HPG * LANES))
        pos = np.arange(lb)
        bias = np.full((T_SAMPLE, ATT_HPG, lb), -np.inf)
        for t in range(T_SAMPLE):
            if dil == 1:
                j = lb + t - pos
                valid = j <= NBACK
            else:
                j = NBACK - pos // dil
                valid = (pos % dil) == t
            bias[t] = np.where(valid[None, :], -(slopes[g][:, None] * dil) * j[None, :], -np.inf)
            for h in range(ATT_HPG):
                for t2 in range(T_SAMPLE):
                    if (dil == 1 and t2 <= t) or t2 == t:
                        bias_new[g, t * ATT_HPG + h, t2] = -slopes[g][h] * dil * (t - t2)
        biases.append(bias)
    for pi, (g, t_set) in enumerate(SA_PASSES):
        dil = ATT_GROUPS[g][1]
        for t in t_set:
            cs[pi, t] = 1.0 if dil == 1 else (lane % dil == t)
    bd = (np.arange(ATT_W)[None, :] // ATT_DH == np.arange(ATT_HPG)[:, None]).astype(np.float64)
    return sels, biases, bias_new, cs, bd


def _att_sample_body(qd_ref, qrow_ref, kn_ref, vn_ref, c0_ref, c1_ref, c2_ref,
                     sel0_ref, sel1_ref, sel2_ref, b0_ref, b1_ref, b2_ref, bn_ref, cs_ref, bd_ref,
                     o_ref, s_scr, e_scr):
    caches = (c0_ref, c1_ref, c2_ref)
    sels = (sel0_ref, sel1_ref, sel2_ref)
    biases = (b0_ref, b1_ref, b2_ref)
    bd = bd_ref[...]
    zpad = jnp.zeros((LANES - SUBLANES, ATT_W), F32)
    e_new, dens, lses = {}, {}, {}
    pid = 0
    for g, (lb, dil) in enumerate(ATT_GROUPS):
        chunks = lb // LANES
        passes = [p for p in SA_PASSES if p[0] == g]
        qsel = jnp.dot(qd_ref[0, g].astype(BF16), sels[g][...], preferred_element_type=F32)
        qrows = qrow_ref[0, g]
        qbd = jnp.concatenate(
            [jnp.broadcast_to(qrows[t:t + 1], (ATT_HPG, ATT_W)) * bd for t in range(T_SAMPLE)], 0)
        kn = jnp.concatenate([kn_ref[0, g], zpad], 0)
        s_new = lax.dot_general(qbd.astype(BF16), kn.astype(BF16), NT_DIMS,
                                preferred_element_type=F32) * ATT_SCALE + bn_ref[g]
        pass_of = {}
        for local, (_, t_set) in enumerate(passes):
            for h in range(ATT_HPG):
                col = (local * ATT_HPG + h) * LANES
                qs = qsel[:, col:col + LANES]
                for c in range(chunks):
                    lanes = slice(c * LANES, (c + 1) * LANES)
                    prod = caches[g][0, 0, h, :, lanes] * qs
                    part = prod.reshape(ATT_DH // SUBLANES, SUBLANES, LANES).sum(0)
                    s_scr[pid, h:h + 1, lanes] = part.sum(0, keepdims=True)
            for t in t_set:
                pass_of[t] = pid
            pid += 1
        for t in range(T_SAMPLE):
            sc = s_scr[pass_of[t], :, :lb] * ATT_SCALE + biases[g][t]
            sn = s_new[t * ATT_HPG:(t + 1) * ATT_HPG]
            m = jnp.maximum(jnp.max(sc, -1, keepdims=True), jnp.max(sn, -1, keepdims=True))
            e = jnp.exp(sc - m)
            en = jnp.exp(sn - m)
            den = jnp.sum(e, -1, keepdims=True) + jnp.sum(en, -1, keepdims=True)
            e_scr[g * T_SAMPLE + t, :, :lb] = e
            e_new[g, t], dens[g, t], lses[g, t] = en, den, m + jnp.log(den)
    wts = {}
    for t in range(T_SAMPLE):
        ls = [lses[g, t] for g in range(N_GROUPS)]
        m = jnp.maximum(jnp.maximum(ls[0], ls[1]), ls[2])
        ws = [jnp.exp(l - m) for l in ls]
        tot = ws[0] + ws[1] + ws[2]
        for g in range(N_GROUPS):
            wts[g, t] = ws[g] / (tot * dens[g, t])
    o = jnp.zeros((SUBLANES, ATT_W), F32)
    o_new = jnp.zeros((T_SAMPLE * ATT_HPG, ATT_W), F32)
    pid = 0
    for g, (lb, dil) in enumerate(ATT_GROUPS):
        chunks = lb // LANES
        vn = jnp.concatenate([vn_ref[0, g], zpad], 0)
        pn = jnp.concatenate([e_new[g, t] * wts[g, t] for t in range(T_SAMPLE)], 0)
        o_new = o_new + jnp.dot(pn.astype(BF16), vn.astype(BF16), preferred_element_type=F32)
        for _, t_set in [p for p in SA_PASSES if p[0] == g]:
            pc = None
            for t in t_set:
                term = e_scr[g * T_SAMPLE + t, :, :lb] * wts[g, t]
                pc = term if pc is None else pc + term
            folded = []
            for h in range(ATT_HPG):
                acc = None
                for c in range(chunks):
                    lanes = slice(c * LANES, (c + 1) * LANES)
                    term = caches[g][0, 1, h, :, lanes] * pc[h:h + 1, lanes]
                    acc = term if acc is None else acc + term
                folded.append(acc)
            wf = jnp.concatenate(folded, 0)
            o = o + lax.dot_general(cs_ref[pid].astype(BF16), wf.astype(BF16), NT_DIMS,
                                    preferred_element_type=F32)
            pid += 1
    extra = [jnp.sum(o_new[t * ATT_HPG:(t + 1) * ATT_HPG] * bd, 0, keepdims=True)
             for t in range(T_SAMPLE)]
    o_ref[0] = o[:T_SAMPLE] + jnp.concatenate(extra, 0)


def _att_sample(q, k_new, v_new, caches):
    sels, biases, bias_new, cs, bd = _att_sample_tables()
    n = N_SAMPLE
    qd = q.transpose(1, 2, 4, 3, 0).reshape(n, N_GROUPS, ATT_DH, ATT_HPG * T_SAMPLE)

    def rows(a):
        a = a.transpose(1, 2, 0, 3, 4).reshape(n, N_GROUPS, T_SAMPLE, ATT_W)
        return jnp.pad(a, ((0, 0), (0, 0), (0, SUBLANES - T_SAMPLE), (0, 0)))

    views = [c.transpose(0, 2, 3, 4, 1) for c in caches]
    const = lambda a, dt: jnp.asarray(a, dt)
    tables = ([const(s, BF16) for s in sels] + [const(b, F32) for b in biases]
              + [const(bias_new, F32), const(cs, F32), const(bd, F32)])
    row_spec = pl.BlockSpec((1, N_GROUPS, SUBLANES, ATT_W), lambda i: (i, 0, 0, 0))
    return pl.pallas_call(
        _att_sample_body,
        grid=(n,),
        in_specs=[pl.BlockSpec((1, N_GROUPS, ATT_DH, ATT_HPG * T_SAMPLE), lambda i: (i, 0, 0, 0)),
                  row_spec, row_spec, row_spec]
        + [pl.BlockSpec((1, 2, ATT_HPG, ATT_DH, lb), lambda i: (i, 0, 0, 0, 0))
           for lb, _ in ATT_GROUPS]
        + [_resident(t.shape, lambda i, nd=t.ndim: (0,) * nd) for t in tables],
        out_specs=pl.BlockSpec((1, T_SAMPLE, ATT_W), lambda i: (i, 0, 0)),
        out_shape=jax.ShapeDtypeStruct((n, T_SAMPLE, ATT_W), F32),
        scratch_shapes=[pltpu.VMEM((len(SA_PASSES), ATT_HPG, SA_MAX_LB), F32),
                        pltpu.VMEM((N_GROUPS * T_SAMPLE, ATT_HPG, SA_MAX_LB), F32)],
        compiler_params=_params("arbitrary"),
        name="attention_sample",
    )(qd, rows(q), rows(k_new), rows(v_new), *views, *tables)


def kernel(x_prompt, x_sample, c_prompt, c_sample, state_ret, cache_kv_w128, cache_kv_w512,
           cache_kv_w2048, w_ada, b_ada, ln_g, ln_b, w_ffn_in, w_ffn_out, w_ret_in, ret_gn_g,
           ret_gn_b, w_ret_out, w_att_in, w_att_out):
    d = D_MODEL
    n_p, seq, _ = x_prompt.shape
    w_ffn_in_b = w_ffn_in.astype(BF16)
    w_ffn_out_b = w_ffn_out.astype(BF16)
    w_ret_in_b = w_ret_in.astype(BF16)
    w_ret_out_b = w_ret_out.astype(BF16)
    w_att_in_b = w_att_in.astype(BF16)
    w_att_out_b = w_att_out.astype(BF16)

    c_all = jnp.concatenate(
        [c_sample, c_prompt, jnp.zeros((ADA_ROWS - N_SAMPLE - n_p, d), F32)], axis=0)
    ada = _ada_table(c_all, w_ada, b_ada)

    prompt = _Trunk(n_p * seq, 512, False, seq // 512)
    sample = _Trunk(N_SAMPLE * T_SAMPLE, N_SAMPLE, True, 1)
    xp = x_prompt.reshape(n_p * seq, d)
    xs = x_sample.transpose(1, 0, 2).reshape(T_SAMPLE * N_SAMPLE, d)

    def ffn(x, trunk, layer, which):
        sub = 0 if which == 0 else 2
        return _ffn(x, ada, w_ffn_in_b[layer, which], w_ffn_out_b[layer, which],
                    ln_g[layer, sub], ln_b[layer, sub], trunk, layer, sub)

    xp = ffn(xp, prompt, 0, 0)
    xs = ffn(xs, sample, 0, 0)

    qkvg_p = _inproj(xp, ada, w_ret_in_b, prompt, 0, BF16)
    gated_p, ret_p = _ret_prompt(qkvg_p, ret_gn_g, ret_gn_b, n_p, seq)
    xp = _outproj(gated_p, xp, ada, w_ret_out_b, ln_g[0, 1], ln_b[0, 1], prompt, 0)

    qkvg_s = _inproj(xs, ada, w_ret_in_b, sample, 0, F32)
    gated_s, ret_s = _ret_sample(qkvg_s.reshape(T_SAMPLE, N_SAMPLE, -1), state_ret,
                                 ret_gn_g, ret_gn_b)
    xs = _outproj(gated_s.reshape(T_SAMPLE * N_SAMPLE, -1), xs, ada, w_ret_out_b,
                  ln_g[0, 1], ln_b[0, 1], sample, 0)

    xp = ffn(xp, prompt, 0, 1)
    xs = ffn(xs, sample, 0, 1)

    xp = ffn(xp, prompt, 1, 0)
    xs = ffn(xs, sample, 1, 0)

    keep_p = [min(w, seq) for w, _ in ATT_GROUPS]
    *qkv_cls, kv_tail = _att_inproj_prompt(xp, ada, w_att_in_b, prompt, 1, max(keep_p))
    band = [_band_attention(qkv_cls[g], g) for g in range(N_GROUPS)]
    xp = _att_outproj_prompt([o for o, _ in band], [l for _, l in band], xp, ada, w_att_out_b,
                             ln_g[1, 1], ln_b[1, 1], prompt, 1)
    kv_tail = kv_tail.reshape(n_p, max(keep_p), 2, N_GROUPS, ATT_HPG, ATT_DH)
    kv_p = [kv_tail[:, max(keep_p) - keep_p[g]:, :, g] for g in range(N_GROUPS)]

    qkv_s = _inproj(xs, ada, w_att_in_b, sample, 1, F32)
    qkv_s = qkv_s.reshape(T_SAMPLE, N_SAMPLE, 3, N_GROUPS, ATT_HPG, ATT_DH)
    q_s, k_s, v_s = qkv_s[:, :, 0], qkv_s[:, :, 1], qkv_s[:, :, 2]
    att_s = _att_sample(q_s, k_s, v_s, (cache_kv_w128, cache_kv_w512, cache_kv_w2048))
    att_s = att_s.transpose(1, 0, 2).reshape(T_SAMPLE * N_SAMPLE, ATT_W)
    xs = _outproj(att_s, xs, ada, w_att_out_b, ln_g[1, 1], ln_b[1, 1], sample, 1)
    kv_s = [jnp.stack([k_s[:, :, g], v_s[:, :, g]], axis=2).transpose(1, 0, 2, 3, 4)
            for g in range(N_GROUPS)]

    xp = ffn(xp, prompt, 1, 1)
    xs = ffn(xs, sample, 1, 1)

    y_prompt = xp.reshape(n_p, seq, d)
    y_sample = xs.reshape(T_SAMPLE, N_SAMPLE, d).transpose(1, 0, 2)
    return (y_prompt, y_sample, ret_p, ret_s, kv_p[0], kv_s[0], kv_p[1], kv_s[1], kv_p[2], kv_s[2])
```

```python
import functools

import numpy as np
import jax
import jax.numpy as jnp
from jax import lax
from jax.experimental import pallas as pl
from jax.experimental.pallas import tpu as pltpu

F32 = jnp.float32
BF16 = jnp.bfloat16

D_MODEL = 1024
DEPTH = 2
D_FF = 2816
RET_HEADS = 4
RET_DK = 256
RET_DV = 512
RET_CHUNK = 128
ATT_GROUPS = ((128, 1), (512, 4), (2048, 16))
N_GROUPS = 3
ATT_HPG = 8
ATT_DH = 64
ATT_W = ATT_HPG * ATT_DH
NBACK = 128
ALPHA = (2 * DEPTH) ** 0.25
NORM_EPS = 1e-5
ATT_SCALE = ATT_DH ** -0.5

V7X_VMEM_LIMIT_BYTES = 60000 * 1024
LANES = 128
SUBLANES = 8

N_SAMPLE = 128
T_SAMPLE = 4
ADA_ROWS = 136
PROMPT_ROW_BLOCK = N_SAMPLE // SUBLANES

NT_DIMS = (((1,), (1,)), ((), ()))
TN_DIMS = (((0,), (0,)), ((), ()))


def _params(*semantics):
    return pltpu.CompilerParams(dimension_semantics=semantics,
                                vmem_limit_bytes=V7X_VMEM_LIMIT_BYTES)


def _resident(block, index_map):
    return pl.BlockSpec(block, index_map, pipeline_mode=pl.Buffered(1))


def _layer_norm(z, g, b):
    mu = jnp.mean(z, -1, keepdims=True)
    zc = z - mu
    var = jnp.mean(zc * zc, -1, keepdims=True)
    return zc * lax.rsqrt(var + NORM_EPS) * g + b


def _log_gamma():
    return np.log1p(-(2.0 ** (-5.0 - np.arange(RET_HEADS, dtype=np.float64))))


def _alibi_slopes():
    h = np.arange(1, N_GROUPS * ATT_HPG + 1, dtype=np.float64)
    return (2.0 ** (-8.0 * h / (N_GROUPS * ATT_HPG))).reshape(N_GROUPS, ATT_HPG)


def _ada_body(c_ref, w_ref, b_ref, o_ref):
    s = jax.nn.silu(c_ref[...]).astype(BF16)
    w = w_ref[0].astype(BF16)
    o_ref[0, 0] = jnp.dot(s, w, preferred_element_type=F32) + b_ref[0, 0]


def _ada_table(c_all, w_ada, b_ada):
    d = D_MODEL
    return pl.pallas_call(
        _ada_body,
        grid=(DEPTH, 9),
        in_specs=[
            _resident((ADA_ROWS, d), lambda i, j: (0, 0)),
            pl.BlockSpec((1, d, d), lambda i, j: (i, 0, j)),
            pl.BlockSpec((1, 1, 1, d), lambda i, j: (i, j, 0, 0)),
        ],
        out_specs=pl.BlockSpec((1, 1, ADA_ROWS, d), lambda i, j: (i, j, 0, 0)),
        out_shape=jax.ShapeDtypeStruct((DEPTH, 9, ADA_ROWS, d), F32),
        compiler_params=_params("arbitrary", "arbitrary"),
        name="ada_table",
    )(c_all, w_ada, b_ada.reshape(DEPTH, 9, 1, d))


class _Trunk:
    def __init__(self, tokens, tile, per_row, tiles_per_seq):
        self.tokens = tokens
        self.tile = tile
        self.per_row = per_row
        self.tiles_per_seq = tiles_per_seq

    def mod_spec(self, layer, sub):
        if self.per_row:
            return pl.BlockSpec((1, 3, self.tile, D_MODEL), lambda i: (layer, sub, 0, 0))
        return pl.BlockSpec((1, 3, SUBLANES, D_MODEL),
                            lambda i: (layer, sub, PROMPT_ROW_BLOCK, 0))

    def mod_rows(self, mod_ref, k):
        if self.per_row:
            return mod_ref[0, k]
        n = pl.program_id(0) // self.tiles_per_seq
        return mod_ref[0, k, pl.ds(n, 1), :]


def _ffn_body(x_ref, mod_ref, win_ref, wout_ref, g_ref, b_ref, o_ref, *, trunk):
    x = x_ref[...]
    shift = trunk.mod_rows(mod_ref, 0)
    scale = trunk.mod_rows(mod_ref, 1)
    gate = trunk.mod_rows(mod_ref, 2)
    u = (x * (1.0 + scale) + shift).astype(BF16)
    h = jnp.dot(u, win_ref[...], preferred_element_type=F32)
    a = h[:, :D_FF]
    b = h[:, D_FF:]
    act = (jax.nn.silu(a) * b).astype(BF16)
    y = jnp.dot(act, wout_ref[...], preferred_element_type=F32)
    z = ALPHA * x + (0.5 * (1.0 + gate)) * y
    o_ref[...] = _layer_norm(z, g_ref[...], b_ref[...])


class _Part:
    def __init__(self, body, args, in_specs, out_shape, out_specs, scratch=()):
        self.body, self.args, self.in_specs = body, list(args), list(in_specs)
        self.out_shape, self.out_specs, self.scratch = list(out_shape), list(out_specs), list(scratch)


def _run(parts, steps, name):
    n_in = [len(p.args) for p in parts]
    n_out = [len(p.out_shape) for p in parts]
    n_scr = [len(p.scratch) for p in parts]

    def body(*refs):
        ins, outs, scr = refs[:sum(n_in)], refs[sum(n_in):sum(n_in) + sum(n_out)], refs[sum(n_in) + sum(n_out):]
        a = b = c = 0
        for k, p in enumerate(parts):
            p.body(*ins[a:a + n_in[k]], *outs[b:b + n_out[k]], *scr[c:c + n_scr[k]])
            a, b, c = a + n_in[k], b + n_out[k], c + n_scr[k]

    res = pl.pallas_call(
        body,
        grid=(steps,),
        in_specs=[s for p in parts for s in p.in_specs],
        out_specs=[s for p in parts for s in p.out_specs],
        out_shape=[s for p in parts for s in p.out_shape],
        scratch_shapes=[s for p in parts for s in p.scratch],
        compiler_params=_params("arbitrary"),
        name=name,
    )(*[a for p in parts for a in p.args])
    out, b = [], 0
    for k in range(len(parts)):
        out.append(res[b:b + n_out[k]])
        b += n_out[k]
    return out


def _ffn_part(x, ada, w_in, w_out, ln_g, ln_b, trunk, layer, sub):
    d = D_MODEL
    tm = trunk.tile
    return _Part(
        functools.partial(_ffn_body, trunk=trunk),
        (x, ada, w_in, w_out, ln_g.reshape(1, d), ln_b.reshape(1, d)),
        [
            pl.BlockSpec((tm, d), lambda i: (i, 0)),
            trunk.mod_spec(layer, sub),
            _resident((d, 2 * D_FF), lambda i: (0, 0)),
            _resident((D_FF, d), lambda i: (0, 0)),
            _resident((1, d), lambda i: (0, 0)),
            _resident((1, d), lambda i: (0, 0)),
        ],
        [jax.ShapeDtypeStruct((trunk.tokens, d), F32)],
        [pl.BlockSpec((tm, d), lambda i: (i, 0))])


def _inproj_body(x_ref, mod_ref, w_ref, o_ref, *, trunk):
    x = x_ref[...]
    u = (x * (1.0 + trunk.mod_rows(mod_ref, 1)) + trunk.mod_rows(mod_ref, 0)).astype(BF16)
    o_ref[...] = jnp.dot(u, w_ref[...], preferred_element_type=F32).astype(o_ref.dtype)


def _inproj(x, ada, w, trunk, layer, out_dtype):
    d = D_MODEL
    tm = trunk.tile
    n_out = w.shape[1]
    return pl.pallas_call(
        functools.partial(_inproj_body, trunk=trunk),
        grid=(trunk.tokens // tm,),
        in_specs=[
            pl.BlockSpec((tm, d), lambda i: (i, 0)),
            trunk.mod_spec(layer, 1),
            _resident((d, n_out), lambda i: (0, 0)),
        ],
        out_specs=pl.BlockSpec((tm, n_out), lambda i: (i, 0)),
        out_shape=jax.ShapeDtypeStruct((trunk.tokens, n_out), out_dtype),
        compiler_params=_params("arbitrary"),
        name="inproj",
    )(x, ada, w)


def _att_inproj_body(x_ref, mod_ref, w_ref, a0_ref, a1_ref, a2_ref, tail_ref, y_scr, *, trunk):
    x = x_ref[...]
    u = (x * (1.0 + trunk.mod_rows(mod_ref, 1)) + trunk.mod_rows(mod_ref, 0)).astype(BF16)
    y = jnp.dot(u, w_ref[...], preferred_element_type=F32)
    tail_ref[0] = y[:, N_GROUPS * ATT_W:]
    for cb in range(y.shape[1] // LANES):
        y_scr[cb] = y[:, cb * LANES:(cb + 1) * LANES]
    slabs = ATT_W // LANES
    for g, a_ref in enumerate((a0_ref, a1_ref, a2_ref)):
        dil = ATT_GROUPS[g][1]
        rows = trunk.tile // dil
        for r in range(dil):
            sel = pl.ds(r, rows, stride=dil) if dil > 1 else slice(None)
            for part in range(3):
                for cb in range(slabs):
                    src = ((part * N_GROUPS + g) * slabs + cb)
                    dst = (part * slabs + cb) * LANES
                    a_ref[0, r, :, dst:dst + LANES] = y_scr[src, sel, :].astype(BF16)


def _att_inproj_prompt(x, ada, w, trunk, layer, tail_rows):
    d = D_MODEL
    tm = trunk.tile
    n_out = w.shape[1]
    tps = trunk.tiles_per_seq
    n_seq = trunk.tokens // (tps * tm)
    first = tps - tail_rows // tm
    out_shape, out_specs = [], []
    for _, dil in ATT_GROUPS:
        out_shape.append(jax.ShapeDtypeStruct((n_seq, dil, tps * tm // dil, 3 * ATT_W), BF16))
        out_specs.append(pl.BlockSpec((1, dil, tm // dil, 3 * ATT_W),
                                      lambda i: (i // tps, 0, i % tps, 0)))
    kv_w = n_out - N_GROUPS * ATT_W
    out_shape.append(jax.ShapeDtypeStruct((n_seq, tail_rows, kv_w), F32))
    out_specs.append(pl.BlockSpec((1, tm, kv_w),
                                  lambda i: (i // tps, jnp.maximum(i % tps - first, 0), 0)))
    return pl.pallas_call(
        functools.partial(_att_inproj_body, trunk=trunk),
        grid=(trunk.tokens // tm,),
        in_specs=[
            pl.BlockSpec((tm, d), lambda i: (i, 0)),
            trunk.mod_spec(layer, 1),
            _resident((d, n_out), lambda i: (0, 0)),
        ],
        out_specs=out_specs,
        out_shape=out_shape,
        scratch_shapes=[pltpu.VMEM((n_out // LANES, tm, LANES), F32)],
        compiler_params=_params("arbitrary"),
        name="att_inproj",
    )(x, ada, w)


def _outproj_tail(a, x_ref, mod_ref, w_ref, g_ref, b_ref, o_ref, trunk):
    x = x_ref[...]
    gate = trunk.mod_rows(mod_ref, 2)
    y = jnp.dot(a.astype(BF16), w_ref[...], preferred_element_type=F32)
    z = ALPHA * x + (1.0 + gate) * y
    o_ref[...] = _layer_norm(z, g_ref[...], b_ref[...])


def _outproj_body(a_ref, x_ref, mod_ref, w_ref, g_ref, b_ref, o_ref, *, trunk):
    _outproj_tail(a_ref[...], x_ref, mod_ref, w_ref, g_ref, b_ref, o_ref, trunk)


def _outproj(a, x, ada, w, ln_g, ln_b, trunk, layer):
    d = D_MODEL
    tm = trunk.tile
    k_in = w.shape[0]
    return pl.pallas_call(
        functools.partial(_outproj_body, trunk=trunk),
        grid=(trunk.tokens // tm,),
        in_specs=[
            pl.BlockSpec((tm, k_in), lambda i: (i, 0)),
            pl.BlockSpec((tm, d), lambda i: (i, 0)),
            trunk.mod_spec(layer, 1),
            _resident((k_in, d), lambda i: (0, 0)),
            _resident((1, d), lambda i: (0, 0)),
            _resident((1, d), lambda i: (0, 0)),
        ],
        out_specs=pl.BlockSpec((tm, d), lambda i: (i, 0)),
        out_shape=jax.ShapeDtypeStruct((trunk.tokens, d), F32),
        compiler_params=_params("arbitrary"),
        name="outproj",
    )(a, x, ada, w, ln_g.reshape(1, d), ln_b.reshape(1, d))


def _att_outproj_body(o0_ref, o1_ref, o2_ref, l0_ref, l1_ref, l2_ref, x_ref, mod_ref, w_ref,
                      g_ref, b_ref, out_ref, tok_scr, *, trunk):
    def token_major(ref, g, slot):
        dil = ATT_GROUPS[g][1]
        if dil == 1:
            return ref[0, 0]
        rows = trunk.tile // dil
        slabs = ATT_W // LANES
        for r in range(dil):
            for cb in range(slabs):
                tok_scr[slot, cb, pl.ds(r, rows, stride=dil), :] = (
                    ref[0, r, :, cb * LANES:(cb + 1) * LANES])
        return jnp.concatenate([tok_scr[slot, cb] for cb in range(slabs)], axis=1)

    os = [token_major(r, g, g) for g, r in enumerate((o0_ref, o1_ref, o2_ref))]
    ls = [token_major(r, g, N_GROUPS + g) for g, r in enumerate((l0_ref, l1_ref, l2_ref))]
    m = jnp.maximum(jnp.maximum(ls[0], ls[1]), ls[2])
    ws = [jnp.exp(l - m) for l in ls]
    a = (ws[0] * os[0] + ws[1] * os[1] + ws[2] * os[2]) / (ws[0] + ws[1] + ws[2])
    _outproj_tail(a, x_ref, mod_ref, w_ref, g_ref, b_ref, out_ref, trunk)


def _att_outproj_prompt(outs, lses, x, ada, w, ln_g, ln_b, trunk, layer):
    d = D_MODEL
    tm = trunk.tile
    tps = trunk.tiles_per_seq
    k_in = w.shape[0]
    cls_specs = [pl.BlockSpec((1, dil, tm // dil, ATT_W), lambda i: (i // tps, 0, i % tps, 0))
                 for _, dil in ATT_GROUPS]
    return pl.pallas_call(
        functools.partial(_att_outproj_body, trunk=trunk),
        grid=(trunk.tokens // tm,),
        in_specs=cls_specs + cls_specs + [
            pl.BlockSpec((tm, d), lambda i: (i, 0)),
            trunk.mod_spec(layer, 1),
            _resident((k_in, d), lambda i: (0, 0)),
            _resident((1, d), lambda i: (0, 0)),
            _resident((1, d), lambda i: (0, 0)),
        ],
        out_specs=pl.BlockSpec((tm, d), lambda i: (i, 0)),
        out_shape=jax.ShapeDtypeStruct((trunk.tokens, d), F32),
        scratch_shapes=[pltpu.VMEM((2 * N_GROUPS, ATT_W // LANES, tm, LANES), F32)],
        compiler_params=_params("arbitrary"),
        name="att_outproj",
    )(*outs, *lses, x, ada, w, ln_g.reshape(1, d), ln_b.reshape(1, d))


def _group_norm_gate(o, g, gn_g, gn_b):
    mu = jnp.mean(o, -1, keepdims=True)
    oc = o - mu
    var = jnp.mean(oc * oc, -1, keepdims=True)
    on = oc * lax.rsqrt(var + NORM_EPS) * gn_g + gn_b
    return jax.nn.silu(g) * on


RET_TILE = 512


def _ret_decay_tables(chunk):
    lg = _log_gamma()
    pos = np.arange(chunk, dtype=np.float64)
    diff = pos[:, None] - pos[None, :]
    inner = np.where(diff >= 0, np.exp(np.maximum(diff, 0.0)[None] * lg[:, None, None]), 0.0)
    qd = np.exp((pos[None, :] + 1.0) * lg[:, None])
    kd = np.exp((chunk - 1.0 - pos[None, :]) * lg[:, None])
    cd = np.exp(chunk * lg)
    return inner, qd, kd, cd


def _ret_prompt_body(q_ref, k_ref, v_ref, g_ref, inner_ref, qd_ref, kd_ref, gng_ref, gnb_ref,
                     o_ref, sfin_ref, s_scr, *, cdec):
    t = pl.program_id(1)

    @pl.when(t == 0)
    def _():
        s_scr[...] = jnp.zeros_like(s_scr)

    for c in range(RET_TILE // RET_CHUNK):
        rows = slice(c * RET_CHUNK, (c + 1) * RET_CHUNK)
        for h in range(RET_HEADS):
            qk = slice(h * RET_DK, (h + 1) * RET_DK)
            vv = slice(h * RET_DV, (h + 1) * RET_DV)
            q = q_ref[rows, qk]
            k = k_ref[rows, qk] * (RET_DK ** -0.5)
            v = v_ref[rows, vv]
            scores = lax.dot_general(q, k, NT_DIMS, preferred_element_type=F32) * inner_ref[h]
            s_old = s_scr[h]
            o = (jnp.dot(scores.astype(BF16), v, preferred_element_type=F32)
                 + jnp.dot(q, s_old.astype(BF16), preferred_element_type=F32) * qd_ref[h])
            kdec = (k.astype(F32) * kd_ref[h]).astype(BF16)
            upd = lax.dot_general(kdec, v, TN_DIMS, preferred_element_type=F32)
            s_scr[h] = s_old * cdec[h] + upd
            gated = _group_norm_gate(o, g_ref[rows, vv].astype(F32), gng_ref[:, vv], gnb_ref[:, vv])
            o_ref[rows, vv] = gated.astype(o_ref.dtype)

    @pl.when(t == pl.num_programs(1) - 1)
    def _():
        sfin_ref[0] = s_scr[...]


def _ret_prompt(qkvg, gn_g, gn_b, n_seq, seq):
    inner, qd, kd, cd = _ret_decay_tables(RET_CHUNK)
    inner = jnp.asarray(inner, F32)
    qd_b = jnp.asarray(np.broadcast_to(qd[:, :, None], (RET_HEADS, RET_CHUNK, RET_DV)), F32)
    kd_b = jnp.asarray(np.broadcast_to(kd[:, :, None], (RET_HEADS, RET_CHUNK, RET_DK)), F32)
    cdec = tuple(float(np.float32(c)) for c in cd)
    tiles = seq // RET_TILE
    vw = RET_HEADS * RET_DV
    qw = RET_HEADS * RET_DK
    return pl.pallas_call(
        functools.partial(_ret_prompt_body, cdec=cdec),
        grid=(n_seq, tiles),
        in_specs=[
            pl.BlockSpec((RET_TILE, qw), lambda n, t: (n * tiles + t, 0)),
            pl.BlockSpec((RET_TILE, qw), lambda n, t: (n * tiles + t, 1)),
            pl.BlockSpec((RET_TILE, vw), lambda n, t: (n * tiles + t, 1)),
            pl.BlockSpec((RET_TILE, vw), lambda n, t: (n * tiles + t, 2)),
            _resident((RET_HEADS, RET_CHUNK, RET_CHUNK), lambda n, t: (0, 0, 0)),
            _resident((RET_HEADS, RET_CHUNK, RET_DV), lambda n, t: (0, 0, 0)),
            _resident((RET_HEADS, RET_CHUNK, RET_DK), lambda n, t: (0, 0, 0)),
            _resident((1, vw), lambda n, t: (0, 0)),
            _resident((1, vw), lambda n, t: (0, 0)),
        ],
        out_specs=[
            pl.BlockSpec((RET_TILE, vw), lambda n, t: (n * tiles + t, 0)),
            pl.BlockSpec((1, RET_HEADS, RET_DK, RET_DV), lambda n, t: (n, 0, 0, 0)),
        ],
        out_shape=[
            jax.ShapeDtypeStruct((n_seq * seq, vw), BF16),
            jax.ShapeDtypeStruct((n_seq, RET_HEADS, RET_DK, RET_DV), F32),
        ],
        scratch_shapes=[pltpu.VMEM((RET_HEADS, RET_DK, RET_DV), F32)],
        compiler_params=_params("arbitrary", "arbitrary"),
        name="retention_prompt",
    )(qkvg, qkvg, qkvg, qkvg, inner, qd_b, kd_b, gn_g.reshape(1, vw), gn_b.reshape(1, vw))


RET_SEQ_GROUP = SUBLANES
RET_S_ROWS = T_SAMPLE * RET_SEQ_GROUP


def _ret_sample_body(q_ref, k_ref, v_ref, g_ref, s_ref, w_ref, qd_ref, kd_ref, cd_ref,
                     gng_ref, gnb_ref, o_ref, so_ref):
    q = q_ref[...].reshape(RET_S_ROWS, RET_DK).astype(BF16)
    k = k_ref[...].reshape(RET_S_ROWS, RET_DK) * (RET_DK ** -0.5)
    v = v_ref[...].reshape(RET_S_ROWS, RET_DV).astype(BF16)
    g = g_ref[...].reshape(RET_S_ROWS, RET_DV)
    scores = lax.dot_general(q, k.astype(BF16), NT_DIMS, preferred_element_type=F32) * w_ref[0]
    o_intra = jnp.dot(scores.astype(BF16), v, preferred_element_type=F32)
    kdec = k * kd_ref[0]
    seq_of_row = lax.broadcasted_iota(jnp.int32, (RET_S_ROWS, 1), 0) % RET_SEQ_GROUP
    o_inter = jnp.zeros((RET_S_ROWS, RET_DV), F32)
    for j in range(RET_SEQ_GROUP):
        mine = seq_of_row == j
        s_old = s_ref[j, 0]
        r = jnp.dot(q, s_old.astype(BF16), preferred_element_type=F32)
        o_inter = jnp.where(mine, r, o_inter)
        kj = jnp.where(mine, kdec, 0.0).astype(BF16)
        upd = lax.dot_general(kj, v, TN_DIMS, preferred_element_type=F32)
        so_ref[j, 0] = s_old * cd_ref[0] + upd
    o = o_intra + o_inter * qd_ref[0]
    gated = _group_norm_gate(o, g, gng_ref[...], gnb_ref[...])
    o_ref[...] = gated.reshape(T_SAMPLE, RET_SEQ_GROUP, RET_DV).astype(o_ref.dtype)


RET_S_STEPS = (N_SAMPLE // RET_SEQ_GROUP) * RET_HEADS


def _ret_sample_part(qkvg, state, gn_g, gn_b):
    inner, qd, kd, cd = _ret_decay_tables(T_SAMPLE)
    row_t = np.arange(RET_S_ROWS) // RET_SEQ_GROUP
    row_j = np.arange(RET_S_ROWS) % RET_SEQ_GROUP
    same = (row_j[:, None] == row_j[None, :])
    w = np.where(same[None], inner[:, row_t[:, None], row_t[None, :]], 0.0)
    w = jnp.asarray(w, F32)
    qd_b = jnp.asarray(np.broadcast_to(qd[:, row_t, None], (RET_HEADS, RET_S_ROWS, RET_DV)), F32)
    kd_b = jnp.asarray(np.broadcast_to(kd[:, row_t, None], (RET_HEADS, RET_S_ROWS, RET_DK)), F32)
    cd_b = jnp.asarray(np.broadcast_to(cd[:, None, None], (RET_HEADS, 1, RET_DV)), F32)
    vw = RET_HEADS * RET_DV
    nq = RET_HEADS
    tg = (T_SAMPLE, RET_SEQ_GROUP)
    nh = RET_HEADS
    return _Part(
        _ret_sample_body,
        (qkvg, qkvg, qkvg, qkvg, state, w, qd_b, kd_b, cd_b,
         gn_g.reshape(1, vw), gn_b.reshape(1, vw)),
        [
            pl.BlockSpec(tg + (RET_DK,), lambda i: (0, i // nh, i % nh)),
            pl.BlockSpec(tg + (RET_DK,), lambda i: (0, i // nh, nq + i % nh)),
            pl.BlockSpec(tg + (RET_DV,), lambda i: (0, i // nh, nq + i % nh)),
            pl.BlockSpec(tg + (RET_DV,), lambda i: (0, i // nh, 2 * nq + i % nh)),
            pl.BlockSpec((RET_SEQ_GROUP, 1, RET_DK, RET_DV), lambda i: (i // nh, i % nh, 0, 0)),
            pl.BlockSpec((1, RET_S_ROWS, RET_S_ROWS), lambda i: (i % nh, 0, 0)),
            pl.BlockSpec((1, RET_S_ROWS, RET_DV), lambda i: (i % nh, 0, 0)),
            pl.BlockSpec((1, RET_S_ROWS, RET_DK), lambda i: (i % nh, 0, 0)),
            pl.BlockSpec((1, 1, RET_DV), lambda i: (i % nh, 0, 0)),
            pl.BlockSpec((1, RET_DV), lambda i: (0, i % nh)),
            pl.BlockSpec((1, RET_DV), lambda i: (0, i % nh)),
        ],
        [
            jax.ShapeDtypeStruct((T_SAMPLE, N_SAMPLE, vw), BF16),
            jax.ShapeDtypeStruct(state.shape, F32),
        ],
        [
            pl.BlockSpec(tg + (RET_DV,), lambda i: (0, i // nh, i % nh)),
            pl.BlockSpec((RET_SEQ_GROUP, 1, RET_DK, RET_DV), lambda i: (i // nh, i % nh, 0, 0)),
        ])


ATT_BQ = 256
ATT_SLAB = 256
HEADS_PER_SLAB = ATT_SLAB // ATT_DH


def _band_window(cur_ref, prev_ref, s):
    if s == 0:
        return jnp.concatenate([prev_ref[0, 0], cur_ref[0, 0, 0:NBACK, :]], axis=0)
    return cur_ref[0, 0, (s - 1) * NBACK:(s + 1) * NBACK, :]


def _band_body(q_ref, kc_ref, kp_ref, vc_ref, vp_ref, bias_ref, o_ref, lse_ref, s_scr, p_scr):
    i = pl.program_id(2)
    ki = lax.broadcasted_iota(jnp.int32, (NBACK, 2 * NBACK), 1)
    has_prev = ki >= jnp.where(i > 0, 0, NBACK)
    lane_head = lax.broadcasted_iota(jnp.int32, (1, ATT_SLAB), 1) // ATT_DH
    qmasks = [jnp.where(lane_head == hh, ATT_SCALE, 0.0).astype(BF16)
              for hh in range(HEADS_PER_SLAB)]
    nsub = ATT_BQ // NBACK
    slabs = ATT_W // ATT_SLAB
    for s in range(nsub):
        q = q_ref[0, 0, s * NBACK:(s + 1) * NBACK, :]
        k_win = _band_window(kc_ref, kp_ref, s)
        for sl in range(slabs):
            lanes = slice(sl * ATT_SLAB, (sl + 1) * ATT_SLAB)
            for hh in range(HEADS_PER_SLAB):
                idx = (s * slabs + sl) * HEADS_PER_SLAB + hh
                s_scr[idx] = lax.dot_general(q[:, lanes] * qmasks[hh], k_win[:, lanes], NT_DIMS,
                                             preferred_element_type=F32)
    for s in range(nsub):
        rows = slice(s * NBACK, (s + 1) * NBACK)
        for h in range(ATT_HPG):
            idx = s * ATT_HPG + h
            sc = s_scr[idx] + bias_ref[h]
            if s == 0:
                sc = jnp.where(has_prev, sc, -jnp.inf)
            m = jnp.max(sc, -1, keepdims=True)
            e = jnp.exp(sc - m)
            den = jnp.sum(e, -1, keepdims=True)
            p_scr[idx] = (e / den).astype(BF16)
            lse_ref[0, 0, rows, h * ATT_DH:(h + 1) * ATT_DH] = jnp.broadcast_to(
                m + jnp.log(den), (NBACK, ATT_DH))
    for s in range(nsub):
        rows = slice(s * NBACK, (s + 1) * NBACK)
        v_win = _band_window(vc_ref, vp_ref, s)
        for sl in range(slabs):
            lanes = slice(sl * ATT_SLAB, (sl + 1) * ATT_SLAB)
            for hh in range(HEADS_PER_SLAB):
                idx = (s * slabs + sl) * HEADS_PER_SLAB + hh
                oh = jnp.dot(p_scr[idx], v_win[:, lanes], preferred_element_type=F32)
                lo = sl * ATT_SLAB + hh * ATT_DH
                o_ref[0, 0, rows, lo:lo + ATT_DH] = oh[:, hh * ATT_DH:(hh + 1) * ATT_DH]


def _band_bias(group):
    _, dil = ATT_GROUPS[group]
    steps = NBACK + np.arange(NBACK)[:, None] - np.arange(2 * NBACK)[None, :]
    valid = (steps >= 0) & (steps <= NBACK)
    slopes = _alibi_slopes()[group]
    bias = -(slopes[:, None, None] * dil) * steps[None]
    return np.where(valid[None], bias, -np.inf)


def _band_attention(qkv_cls, group):
    _, dil = ATT_GROUPS[group]
    n_seq, _, length, _ = qkv_cls.shape
    nsub = ATT_BQ // NBACK
    blocks = length // ATT_BQ
    pairs = nsub * ATT_HPG
    prev = lambda i: jnp.maximum(i * nsub - 1, 0)
    cur = lambda col: pl.BlockSpec((1, 1, ATT_BQ, ATT_W), lambda n, r, i: (n, r, i, col))
    halo = lambda col: pl.BlockSpec((1, 1, NBACK, ATT_W), lambda n, r, i: (n, r, prev(i), col))
    out_sds = jax.ShapeDtypeStruct((n_seq, dil, length, ATT_W), F32)
    return pl.pallas_call(
        _band_body,
        grid=(n_seq, dil, blocks),
        in_specs=[cur(0), cur(1), halo(1), cur(2), halo(2),
                  _resident((ATT_HPG, NBACK, 2 * NBACK), lambda n, r, i: (0, 0, 0))],
        out_specs=[cur(0), cur(0)],
        out_shape=[out_sds, out_sds],
        scratch_shapes=[pltpu.VMEM((pairs, NBACK, 2 * NBACK), F32),
                        pltpu.VMEM((pairs, NBACK, 2 * NBACK), BF16)],
        compiler_params=_params("arbitrary", "arbitrary", "arbitrary"),
        name="band_attention",
    )(qkv_cls, qkv_cls, qkv_cls, qkv_cls, qkv_cls, jnp.asarray(_band_bias(group), F32))


SA_PASSES = tuple((0, (t,)) for t in range(T_SAMPLE)) + ((1, tuple(range(T_SAMPLE))),
                                                        (2, tuple(range(T_SAMPLE))))
SA_MAX_LB = max(w for w, _ in ATT_GROUPS)


def _att_sample_tables():
    slopes = _alibi_slopes()
    sels, biases = [], []
    bias_new = np.full((N_GROUPS, T_SAMPLE * ATT_HPG, LANES), -np.inf)
    cs = np.zeros((len(SA_PASSES), SUBLANES, LANES))
    lane = np.arange(LANES)
    for g, (lb, dil) in enumerate(ATT_GROUPS):
        passes = [p for p in SA_PASSES if p[0] == g]
        sel = np.zeros((ATT_HPG, T_SAMPLE, len(passes), ATT_HPG, LANES))
        for pi, (_, t_set) in enumerate(passes):
            for t in t_set:
                owns = np.ones(LANES, bool) if dil == 1 else (lane % dil == t)
                for h in range(ATT_HPG):
                    sel[h, t, pi, h, owns] = 1.0
        sels.append(sel.reshape(ATT_HPG * T_SAMPLE, len(passes) * ATT_HPG * LANES))
        pos = np.arange(lb)
        bias = np.full((T_SAMPLE, ATT_HPG, lb), -np.inf)
        for t in range(T_SAMPLE):
            if dil == 1:
                j = lb + t - pos
                valid = j <= NBACK
            else:
                j = NBACK - pos // dil
                valid = (pos % dil) == t
            bias[t] = np.where(valid[None, :], -(slopes[g][:, None] * dil) * j[None, :], -np.inf)
            for h in range(ATT_HPG):
                for t2 in range(T_SAMPLE):
                    if (dil == 1 and t2 <= t) or t2 == t:
                        bias_new[g, t * ATT_HPG + h, t2] = -slopes[g][h] * dil * (t - t2)
        biases.append(bias)
    for pi, (g, t_set) in enumerate(SA_PASSES):
        dil = ATT_GROUPS[g][1]
        for t in t_set:
            cs[pi, t] = 1.0 if dil == 1 else (lane % dil == t)
    bd = (np.arange(ATT_W)[None, :] // ATT_DH == np.arange(ATT_HPG)[:, None]).astype(np.float64)
    return sels, biases, bias_new, cs, bd


SA_SEQ_PER_STEP = 2


def _att_scores_body(qd_ref, qrow_ref, kn_ref, c0_ref, c1_ref, c2_ref,
                     sel0_ref, sel1_ref, sel2_ref, b0_ref, b1_ref, b2_ref, bn_ref, bd_ref,
                     p0_ref, p1_ref, p2_ref, pn_ref, s_scr, e_scr):
    caches = (c0_ref, c1_ref, c2_ref)
    sels = (sel0_ref, sel1_ref, sel2_ref)
    biases = (b0_ref, b1_ref, b2_ref)
    p_refs = (p0_ref, p1_ref, p2_ref)
    bd = bd_ref[...]
    zpad = jnp.zeros((LANES - SUBLANES, ATT_W), F32)
    for j in range(SA_SEQ_PER_STEP):
        e_new, dens, lses = {}, {}, {}
        pid = 0
        for g, (lb, dil) in enumerate(ATT_GROUPS):
            chunks = lb // LANES
            passes = [p for p in SA_PASSES if p[0] == g]
            qsel = jnp.dot(qd_ref[j, g].astype(BF16), sels[g][...], preferred_element_type=F32)
            qrows = qrow_ref[j, g]
            qbd = jnp.concatenate(
                [jnp.broadcast_to(qrows[t:t + 1], (ATT_HPG, ATT_W)) * bd
                 for t in range(T_SAMPLE)], 0)
            kn = jnp.concatenate([kn_ref[j, g], zpad], 0)
            s_new = lax.dot_general(qbd.astype(BF16), kn.astype(BF16), NT_DIMS,
                                    preferred_element_type=F32) * ATT_SCALE + bn_ref[g]
            pass_of = {}
            for local, (_, t_set) in enumerate(passes):
                for h in range(ATT_HPG):
                    col = (local * ATT_HPG + h) * LANES
                    qs = qsel[:, col:col + LANES]
                    for c in range(chunks):
                        lanes = slice(c * LANES, (c + 1) * LANES)
                        prod = caches[g][j, 0, h, :, lanes] * qs
                        part = prod.reshape(ATT_DH // SUBLANES, SUBLANES, LANES).sum(0)
                        s_scr[pid, h:h + 1, lanes] = part.sum(0, keepdims=True)
                for t in t_set:
                    pass_of[t] = pid
                pid += 1
            for t in range(T_SAMPLE):
                sc = s_scr[pass_of[t], :, :lb] * ATT_SCALE + biases[g][t]
                sn = s_new[t * ATT_HPG:(t + 1) * ATT_HPG]
                m = jnp.maximum(jnp.max(sc, -1, keepdims=True), jnp.max(sn, -1, keepdims=True))
                e = jnp.exp(sc - m)
                en = jnp.exp(sn - m)
                den = jnp.sum(e, -1, keepdims=True) + jnp.sum(en, -1, keepdims=True)
                e_scr[g * T_SAMPLE + t, :, :lb] = e
                e_new[g, t], dens[g, t], lses[g, t] = en, den, m + jnp.log(den)
        wts = {}
        for t in range(T_SAMPLE):
            ls = [lses[g, t] for g in range(N_GROUPS)]
            m = jnp.maximum(jnp.maximum(ls[0], ls[1]), ls[2])
            ws = [jnp.exp(l - m) for l in ls]
            tot = ws[0] + ws[1] + ws[2]
            for g in range(N_GROUPS):
                wts[g, t] = ws[g] / (tot * dens[g, t])
        for g, (lb, dil) in enumerate(ATT_GROUPS):
            pn_ref[j, g] = jnp.concatenate(
                [e_new[g, t] * wts[g, t] for t in range(T_SAMPLE)], 0)
            for local, (_, t_set) in enumerate([p for p in SA_PASSES if p[0] == g]):
                pc = None
                for t in t_set:
                    term = e_scr[g * T_SAMPLE + t, :, :lb] * wts[g, t]
                    pc = term if pc is None else pc + term
                p_refs[g][j, local] = pc


def _att_values_body(p0_ref, p1_ref, p2_ref, pn_ref, vn_ref, c0_ref, c1_ref, c2_ref,
                     cs_ref, bd_ref, o_ref):
    caches = (c0_ref, c1_ref, c2_ref)
    p_refs = (p0_ref, p1_ref, p2_ref)
    bd = bd_ref[...]
    zpad = jnp.zeros((LANES - SUBLANES, ATT_W), F32)
    for j in range(SA_SEQ_PER_STEP):
        o = jnp.zeros((SUBLANES, ATT_W), F32)
        o_new = jnp.zeros((T_SAMPLE * ATT_HPG, ATT_W), F32)
        pid = 0
        for g, (lb, dil) in enumerate(ATT_GROUPS):
            chunks = lb // LANES
            vn = jnp.concatenate([vn_ref[j, g], zpad], 0)
            o_new = o_new + jnp.dot(pn_ref[j, g].astype(BF16), vn.astype(BF16),
                                    preferred_element_type=F32)
            for local in range(len([p for p in SA_PASSES if p[0] == g])):
                folded = []
                for h in range(ATT_HPG):
                    acc = None
                    for c in range(chunks):
                        lanes = slice(c * LANES, (c + 1) * LANES)
                        term = caches[g][j, 0, h, :, lanes] * p_refs[g][j, local, h:h + 1, lanes]
                        acc = term if acc is None else acc + term
                    folded.append(acc)
                wf = jnp.concatenate(folded, 0)
                o = o + lax.dot_general(cs_ref[pid].astype(BF16), wf.astype(BF16), NT_DIMS,
                                        preferred_element_type=F32)
                pid += 1
        extra = [jnp.sum(o_new[t * ATT_HPG:(t + 1) * ATT_HPG] * bd, 0, keepdims=True)
                 for t in range(T_SAMPLE)]
        o_ref[j] = o[:T_SAMPLE] + jnp.concatenate(extra, 0)


SA_STEPS = N_SAMPLE // SA_SEQ_PER_STEP


def _sa_rows(a):
    a = a.transpose(1, 2, 0, 3, 4).reshape(N_SAMPLE, N_GROUPS, T_SAMPLE, ATT_W)
    return jnp.pad(a, ((0, 0), (0, 0), (0, SUBLANES - T_SAMPLE), (0, 0)))


def _sa_cache_specs(caches, kv):
    views = [c.transpose(0, 2, 3, 4, 1) for c in caches]
    specs = [pl.BlockSpec((SA_SEQ_PER_STEP, 1, ATT_HPG, ATT_DH, lb), lambda i: (i, kv, 0, 0, 0))
             for lb, _ in ATT_GROUPS]
    return views, specs


def _sa_const_specs(tables):
    return [_resident(t.shape, lambda i, nd=t.ndim: (0,) * nd) for t in tables]


def _sa_prob_layout():
    shapes, specs = [], []
    for g, (lb, _) in enumerate(ATT_GROUPS):
        n_pass = len([p for p in SA_PASSES if p[0] == g])
        shapes.append(jax.ShapeDtypeStruct((N_SAMPLE, n_pass, ATT_HPG, lb), F32))
        specs.append(pl.BlockSpec((SA_SEQ_PER_STEP, n_pass, ATT_HPG, lb), lambda i: (i, 0, 0, 0)))
    shapes.append(jax.ShapeDtypeStruct((N_SAMPLE, N_GROUPS, T_SAMPLE * ATT_HPG, LANES), F32))
    specs.append(pl.BlockSpec((SA_SEQ_PER_STEP, N_GROUPS, T_SAMPLE * ATT_HPG, LANES),
                              lambda i: (i, 0, 0, 0)))
    return shapes, specs


def _att_scores_part(q, k_new, caches):
    sels, biases, bias_new, _, bd = _att_sample_tables()
    qd = q.transpose(1, 2, 4, 3, 0).reshape(N_SAMPLE, N_GROUPS, ATT_DH, ATT_HPG * T_SAMPLE)
    views, cache_specs = _sa_cache_specs(caches, 0)
    tables = ([jnp.asarray(s, BF16) for s in sels] + [jnp.asarray(b, F32) for b in biases]
              + [jnp.asarray(bias_new, F32), jnp.asarray(bd, F32)])
    row_spec = pl.BlockSpec((SA_SEQ_PER_STEP, N_GROUPS, SUBLANES, ATT_W), lambda i: (i, 0, 0, 0))
    out_shape, out_specs = _sa_prob_layout()
    return _Part(
        _att_scores_body,
        (qd, _sa_rows(q), _sa_rows(k_new), *views, *tables),
        [pl.BlockSpec((SA_SEQ_PER_STEP, N_GROUPS, ATT_DH, ATT_HPG * T_SAMPLE),
                      lambda i: (i, 0, 0, 0)), row_spec, row_spec]
        + cache_specs + _sa_const_specs(tables),
        out_shape, out_specs,
        [pltpu.VMEM((len(SA_PASSES), ATT_HPG, SA_MAX_LB), F32),
         pltpu.VMEM((N_GROUPS * T_SAMPLE, ATT_HPG, SA_MAX_LB), F32)])


def _att_values_part(probs, v_new, caches):
    _, _, _, cs, bd = _att_sample_tables()
    views, cache_specs = _sa_cache_specs(caches, 1)
    tables = [jnp.asarray(cs, F32), jnp.asarray(bd, F32)]
    row_spec = pl.BlockSpec((SA_SEQ_PER_STEP, N_GROUPS, SUBLANES, ATT_W), lambda i: (i, 0, 0, 0))
    _, prob_specs = _sa_prob_layout()
    return _Part(
        _att_values_body,
        (*probs, _sa_rows(v_new), *views, *tables),
        prob_specs + [row_spec] + cache_specs + _sa_const_specs(tables),
        [jax.ShapeDtypeStruct((N_SAMPLE, T_SAMPLE, ATT_W), F32)],
        [pl.BlockSpec((SA_SEQ_PER_STEP, T_SAMPLE, ATT_W), lambda i: (i, 0, 0))])


def kernel(x_prompt, x_sample, c_prompt, c_sample, state_ret, cache_kv_w128, cache_kv_w512,
           cache_kv_w2048, w_ada, b_ada, ln_g, ln_b, w_ffn_in, w_ffn_out, w_ret_in, ret_gn_g,
           ret_gn_b, w_ret_out, w_att_in, w_att_out):
    d = D_MODEL
    n_p, seq, _ = x_prompt.shape
    w_ffn_in_b = w_ffn_in.astype(BF16)
    w_ffn_out_b = w_ffn_out.astype(BF16)
    w_ret_in_b = w_ret_in.astype(BF16)
    w_ret_out_b = w_ret_out.astype(BF16)
    w_att_in_b = w_att_in.astype(BF16)
    w_att_out_b = w_att_out.astype(BF16)

    c_all = jnp.concatenate(
        [c_sample, c_prompt, jnp.zeros((ADA_ROWS - N_SAMPLE - n_p, d), F32)], axis=0)
    ada = _ada_table(c_all, w_ada, b_ada)

    prompt = _Trunk(n_p * seq, 512, False, seq // 512)
    sample = _Trunk(N_SAMPLE * T_SAMPLE, N_SAMPLE, True, 1)
    xp = x_prompt.reshape(n_p * seq, d)
    xs = x_sample.transpose(1, 0, 2).reshape(T_SAMPLE * N_SAMPLE, d)

    steps = RET_S_STEPS
    tile_s = n_p * seq // steps
    prompt_s = _Trunk(n_p * seq, tile_s, False, seq // tile_s)

    def ffn_part(x, trunk, layer, which):
        sub = 0 if which == 0 else 2
        return _ffn_part(x, ada, w_ffn_in_b[layer, which], w_ffn_out_b[layer, which],
                         ln_g[layer, sub], ln_b[layer, sub], trunk, layer, sub)

    def ffn(x, trunk, layer, which):
        return _run([ffn_part(x, trunk, layer, which)], trunk.tokens // trunk.tile, "ffn")[0][0]

    xs = ffn(xs, sample, 0, 0)
    qkvg_s = _inproj(xs, ada, w_ret_in_b, sample, 0, F32)

    (xp,), (gated_s, ret_s) = _run(
        [ffn_part(xp, prompt_s, 0, 0),
         _ret_sample_part(qkvg_s.reshape(T_SAMPLE, N_SAMPLE, -1), state_ret, ret_gn_g, ret_gn_b)],
        steps, "ffn_ret_sample")

    qkvg_p = _inproj(xp, ada, w_ret_in_b, prompt, 0, BF16)
    gated_p, ret_p = _ret_prompt(qkvg_p, ret_gn_g, ret_gn_b, n_p, seq)
    xp = _outproj(gated_p, xp, ada, w_ret_out_b, ln_g[0, 1], ln_b[0, 1], prompt, 0)

    xs = _outproj(gated_s.reshape(T_SAMPLE * N_SAMPLE, -1), xs, ada, w_ret_out_b,
                  ln_g[0, 1], ln_b[0, 1], sample, 0)
    xs = ffn(xs, sample, 0, 1)
    xs = ffn(xs, sample, 1, 0)
    qkv_s = _inproj(xs, ada, w_att_in_b, sample, 1, F32)
    qkv_s = qkv_s.reshape(T_SAMPLE, N_SAMPLE, 3, N_GROUPS, ATT_HPG, ATT_DH)
    q_s, k_s, v_s = qkv_s[:, :, 0], qkv_s[:, :, 1], qkv_s[:, :, 2]
    caches = (cache_kv_w128, cache_kv_w512, cache_kv_w2048)

    probs, (xp,) = _run([_att_scores_part(q_s, k_s, caches), ffn_part(xp, prompt_s, 0, 1)],
                        steps, "att_scores_ffn")
    (xp,), (att_s,) = _run([ffn_part(xp, prompt_s, 1, 0), _att_values_part(probs, v_s, caches)],
                           steps, "ffn_att_values")

    keep_p = [min(w, seq) for w, _ in ATT_GROUPS]
    *qkv_cls, kv_tail = _att_inproj_prompt(xp, ada, w_att_in_b, prompt, 1, max(keep_p))
    band = [_band_attention(qkv_cls[g], g) for g in range(N_GROUPS)]
    xp = _att_outproj_prompt([o for o, _ in band], [l for _, l in band], xp, ada, w_att_out_b,
                             ln_g[1, 1], ln_b[1, 1], prompt, 1)
    kv_tail = kv_tail.reshape(n_p, max(keep_p), 2, N_GROUPS, ATT_HPG, ATT_DH)
    kv_p = [kv_tail[:, max(keep_p) - keep_p[g]:, :, g] for g in range(N_GROUPS)]

    att_s = att_s.transpose(1, 0, 2).reshape(T_SAMPLE * N_SAMPLE, ATT_W)
    xs = _outproj(att_s, xs, ada, w_att_out_b, ln_g[1, 1], ln_b[1, 1], sample, 1)
    kv_s = [jnp.stack([k_s[:, :, g], v_s[:, :, g]], axis=2).transpose(1, 0, 2, 3, 4)
            for g in range(N_GROUPS)]

    xp = ffn(xp, prompt, 1, 1)
    xs = ffn(xs, sample, 1, 1)

    y_prompt = xp.reshape(n_p, seq, d)
    y_sample = xs.reshape(T_SAMPLE, N_SAMPLE, d).transpose(1, 0, 2)
    return (y_prompt, y_sample, ret_p, ret_s, kv_p[0], kv_s[0], kv_p[1], kv_s[1], kv_p[2], kv_s[2])
```

```python
import functools

import numpy as np
import jax
import jax.numpy as jnp
from jax import lax
from jax.experimental import pallas as pl
from jax.experimental.pallas import tpu as pltpu

F32 = jnp.float32
BF16 = jnp.bfloat16

D_MODEL = 1024
DEPTH = 2
D_FF = 2816
RET_HEADS = 4
RET_DK = 256
RET_DV = 512
RET_CHUNK = 128
ATT_GROUPS = ((128, 1), (512, 4), (2048, 16))
N_GROUPS = 3
ATT_HPG = 8
ATT_DH = 64
ATT_W = ATT_HPG * ATT_DH
NBACK = 128
ALPHA = (2 * DEPTH) ** 0.25
NORM_EPS = 1e-5
ATT_SCALE = ATT_DH ** -0.5

V7X_VMEM_LIMIT_BYTES = 60000 * 1024
LANES = 128
SUBLANES = 8

N_SAMPLE = 128
T_SAMPLE = 4
ADA_ROWS = 136
PROMPT_ROW_BLOCK = N_SAMPLE // SUBLANES

NT_DIMS = (((1,), (1,)), ((), ()))
TN_DIMS = (((0,), (0,)), ((), ()))


def _params(*semantics):
    return pltpu.CompilerParams(dimension_semantics=semantics,
                                vmem_limit_bytes=V7X_VMEM_LIMIT_BYTES)


def _resident(block, index_map):
    return pl.BlockSpec(block, index_map, pipeline_mode=pl.Buffered(1))


def _layer_norm(z, g, b):
    mu = jnp.mean(z, -1, keepdims=True)
    zc = z - mu
    var = jnp.mean(zc * zc, -1, keepdims=True)
    return zc * lax.rsqrt(var + NORM_EPS) * g + b


def _log_gamma():
    return np.log1p(-(2.0 ** (-5.0 - np.arange(RET_HEADS, dtype=np.float64))))


def _alibi_slopes():
    h = np.arange(1, N_GROUPS * ATT_HPG + 1, dtype=np.float64)
    return (2.0 ** (-8.0 * h / (N_GROUPS * ATT_HPG))).reshape(N_GROUPS, ATT_HPG)


def _ada_body(c_ref, w_ref, b_ref, o_ref):
    s = jax.nn.silu(c_ref[...]).astype(BF16)
    w = w_ref[0].astype(BF16)
    o_ref[0, 0] = jnp.dot(s, w, preferred_element_type=F32) + b_ref[0, 0]


def _ada_table(c_all, w_ada, b_ada):
    d = D_MODEL
    return pl.pallas_call(
        _ada_body,
        grid=(DEPTH, 9),
        in_specs=[
            _resident((ADA_ROWS, d), lambda i, j: (0, 0)),
            pl.BlockSpec((1, d, d), lambda i, j: (i, 0, j)),
            pl.BlockSpec((1, 1, 1, d), lambda i, j: (i, j, 0, 0)),
        ],
        out_specs=pl.BlockSpec((1, 1, ADA_ROWS, d), lambda i, j: (i, j, 0, 0)),
        out_shape=jax.ShapeDtypeStruct((DEPTH, 9, ADA_ROWS, d), F32),
        compiler_params=_params("arbitrary", "arbitrary"),
        name="ada_table",
    )(c_all, w_ada, b_ada.reshape(DEPTH, 9, 1, d))


class _Trunk:
    def __init__(self, tokens, tile, per_row, tiles_per_seq):
        self.tokens = tokens
        self.tile = tile
        self.per_row = per_row
        self.tiles_per_seq = tiles_per_seq

    def mod_spec(self, layer, sub):
        if self.per_row:
            return pl.BlockSpec((1, 3, self.tile, D_MODEL), lambda i: (layer, sub, 0, 0))
        return pl.BlockSpec((1, 3, SUBLANES, D_MODEL),
                            lambda i: (layer, sub, PROMPT_ROW_BLOCK, 0))

    def mod_rows(self, mod_ref, k):
        if self.per_row:
            return mod_ref[0, k]
        n = pl.program_id(0) // self.tiles_per_seq
        return mod_ref[0, k, pl.ds(n, 1), :]


def _ffn_body(x_ref, mod_ref, win_ref, wout_ref, g_ref, b_ref, o_ref, *, trunk):
    x = x_ref[...]
    shift = trunk.mod_rows(mod_ref, 0)
    scale = trunk.mod_rows(mod_ref, 1)
    gate = trunk.mod_rows(mod_ref, 2)
    u = (x * (1.0 + scale) + shift).astype(BF16)
    h = jnp.dot(u, win_ref[...], preferred_element_type=F32)
    a = h[:, :D_FF]
    b = h[:, D_FF:]
    act = (jax.nn.silu(a) * b).astype(BF16)
    y = jnp.dot(act, wout_ref[...], preferred_element_type=F32)
    z = ALPHA * x + (0.5 * (1.0 + gate)) * y
    o_ref[...] = _layer_norm(z, g_ref[...], b_ref[...])


class _Part:
    def __init__(self, body, args, in_specs, out_shape, out_specs, scratch=()):
        self.body, self.args, self.in_specs = body, list(args), list(in_specs)
        self.out_shape, self.out_specs, self.scratch = list(out_shape), list(out_specs), list(scratch)


def _run(parts, steps, name):
    n_in = [len(p.args) for p in parts]
    n_out = [len(p.out_shape) for p in parts]
    n_scr = [len(p.scratch) for p in parts]

    def body(*refs):
        ins, outs, scr = refs[:sum(n_in)], refs[sum(n_in):sum(n_in) + sum(n_out)], refs[sum(n_in) + sum(n_out):]
        a = b = c = 0
        for k, p in enumerate(parts):
            p.body(*ins[a:a + n_in[k]], *outs[b:b + n_out[k]], *scr[c:c + n_scr[k]])
            a, b, c = a + n_in[k], b + n_out[k], c + n_scr[k]

    res = pl.pallas_call(
        body,
        grid=(steps,),
        in_specs=[s for p in parts for s in p.in_specs],
        out_specs=[s for p in parts for s in p.out_specs],
        out_shape=[s for p in parts for s in p.out_shape],
        scratch_shapes=[s for p in parts for s in p.scratch],
        compiler_params=_params("arbitrary"),
        name=name,
    )(*[a for p in parts for a in p.args])
    out, b = [], 0
    for k in range(len(parts)):
        out.append(res[b:b + n_out[k]])
        b += n_out[k]
    return out


def _cast_body(*refs):
    n = len(refs) // 2
    for src, dst in zip(refs[:n], refs[n:]):
        dst[...] = src[...].reshape(dst.shape).astype(dst.dtype)


def _cast_part(items, steps):
    args, in_specs, out_shape, out_specs = [], [], [], []
    for arr, lead in items:
        rows, cols = arr.shape[-2:]
        n_blocks = steps
        while rows % n_blocks or (rows // n_blocks) % (2 * SUBLANES):
            n_blocks //= 2
        blk = rows // n_blocks
        args.append(arr)
        in_specs.append(pl.BlockSpec(
            (1,) * len(lead) + (blk, cols),
            lambda i, lead=lead, last=n_blocks - 1: lead + (jnp.minimum(i, last), 0)))
        out_shape.append(jax.ShapeDtypeStruct((rows, cols), BF16))
        out_specs.append(pl.BlockSpec((blk, cols),
                                      lambda i, last=n_blocks - 1: (jnp.minimum(i, last), 0)))
    return _Part(_cast_body, args, in_specs, out_shape, out_specs)


def _ffn_part(x, ada, w_in, w_out, ln_g, ln_b, trunk, layer, sub):
    d = D_MODEL
    tm = trunk.tile
    return _Part(
        functools.partial(_ffn_body, trunk=trunk),
        (x, ada, w_in, w_out, ln_g.reshape(1, d), ln_b.reshape(1, d)),
        [
            pl.BlockSpec((tm, d), lambda i: (i, 0)),
            trunk.mod_spec(layer, sub),
            _resident((d, 2 * D_FF), lambda i: (0, 0)),
            _resident((D_FF, d), lambda i: (0, 0)),
            _resident((1, d), lambda i: (0, 0)),
            _resident((1, d), lambda i: (0, 0)),
        ],
        [jax.ShapeDtypeStruct((trunk.tokens, d), F32)],
        [pl.BlockSpec((tm, d), lambda i: (i, 0))])


def _inproj_body(x_ref, mod_ref, w_ref, o_ref, *, trunk):
    x = x_ref[...]
    u = (x * (1.0 + trunk.mod_rows(mod_ref, 1)) + trunk.mod_rows(mod_ref, 0)).astype(BF16)
    o_ref[...] = jnp.dot(u, w_ref[...], preferred_element_type=F32).astype(o_ref.dtype)


def _inproj(x, ada, w, trunk, layer, out_dtype):
    d = D_MODEL
    tm = trunk.tile
    n_out = w.shape[1]
    return pl.pallas_call(
        functools.partial(_inproj_body, trunk=trunk),
        grid=(trunk.tokens // tm,),
        in_specs=[
            pl.BlockSpec((tm, d), lambda i: (i, 0)),
            trunk.mod_spec(layer, 1),
            _resident((d, n_out), lambda i: (0, 0)),
        ],
        out_specs=pl.BlockSpec((tm, n_out), lambda i: (i, 0)),
        out_shape=jax.ShapeDtypeStruct((trunk.tokens, n_out), out_dtype),
        compiler_params=_params("arbitrary"),
        name="inproj",
    )(x, ada, w)


def _att_inproj_body(x_ref, mod_ref, w_ref, a0_ref, a1_ref, a2_ref, tail_ref, y_scr, *, trunk):
    x = x_ref[...]
    u = (x * (1.0 + trunk.mod_rows(mod_ref, 1)) + trunk.mod_rows(mod_ref, 0)).astype(BF16)
    slabs = ATT_W // LANES
    a_refs = (a0_ref, a1_ref, a2_ref)
    for part in range(3):
        for g in range(N_GROUPS):
            col0 = (part * N_GROUPS + g) * ATT_W
            for c in range(ATT_W // ATT_SLAB):
                lo = col0 + c * ATT_SLAB
                y = jnp.dot(u, w_ref[:, lo:lo + ATT_SLAB], preferred_element_type=F32)
                if part > 0:
                    tail_ref[0, :, lo - N_GROUPS * ATT_W:lo - N_GROUPS * ATT_W + ATT_SLAB] = y
                for half in range(ATT_SLAB // LANES):
                    y_scr[lo // LANES + half] = y[:, half * LANES:(half + 1) * LANES]
            dil = ATT_GROUPS[g][1]
            rows = trunk.tile // dil
            for r in range(dil):
                sel = pl.ds(r, rows, stride=dil) if dil > 1 else slice(None)
                for cb in range(slabs):
                    dst = (part * slabs + cb) * LANES
                    a_refs[g][0, r, :, dst:dst + LANES] = (
                        y_scr[col0 // LANES + cb, sel, :].astype(BF16))


def _att_inproj_prompt(x, ada, w, trunk, layer, tail_rows):
    d = D_MODEL
    tm = trunk.tile
    n_out = w.shape[1]
    tps = trunk.tiles_per_seq
    n_seq = trunk.tokens // (tps * tm)
    first = tps - tail_rows // tm
    out_shape, out_specs = [], []
    for _, dil in ATT_GROUPS:
        out_shape.append(jax.ShapeDtypeStruct((n_seq, dil, tps * tm // dil, 3 * ATT_W), BF16))
        out_specs.append(pl.BlockSpec((1, dil, tm // dil, 3 * ATT_W),
                                      lambda i: (i // tps, 0, i % tps, 0)))
    kv_w = n_out - N_GROUPS * ATT_W
    out_shape.append(jax.ShapeDtypeStruct((n_seq, tail_rows, kv_w), F32))
    out_specs.append(pl.BlockSpec((1, tm, kv_w),
                                  lambda i: (i // tps, jnp.maximum(i % tps - first, 0), 0)))
    return pl.pallas_call(
        functools.partial(_att_inproj_body, trunk=trunk),
        grid=(trunk.tokens // tm,),
        in_specs=[
            pl.BlockSpec((tm, d), lambda i: (i, 0)),
            trunk.mod_spec(layer, 1),
            _resident((d, n_out), lambda i: (0, 0)),
        ],
        out_specs=out_specs,
        out_shape=out_shape,
        scratch_shapes=[pltpu.VMEM((n_out // LANES, tm, LANES), F32)],
        compiler_params=_params("arbitrary"),
        name="att_inproj",
    )(x, ada, w)


def _outproj_tail(a, x_ref, mod_ref, w_ref, g_ref, b_ref, o_ref, trunk):
    x = x_ref[...]
    gate = trunk.mod_rows(mod_ref, 2)
    y = jnp.dot(a.astype(BF16), w_ref[...], preferred_element_type=F32)
    z = ALPHA * x + (1.0 + gate) * y
    o_ref[...] = _layer_norm(z, g_ref[...], b_ref[...])


def _outproj_body(a_ref, x_ref, mod_ref, w_ref, g_ref, b_ref, o_ref, *, trunk):
    _outproj_tail(a_ref[...], x_ref, mod_ref, w_ref, g_ref, b_ref, o_ref, trunk)


def _outproj(a, x, ada, w, ln_g, ln_b, trunk, layer):
    d = D_MODEL
    tm = trunk.tile
    k_in = w.shape[0]
    return pl.pallas_call(
        functools.partial(_outproj_body, trunk=trunk),
        grid=(trunk.tokens // tm,),
        in_specs=[
            pl.BlockSpec((tm, k_in), lambda i: (i, 0)),
            pl.BlockSpec((tm, d), lambda i: (i, 0)),
            trunk.mod_spec(layer, 1),
            _resident((k_in, d), lambda i: (0, 0)),
            _resident((1, d), lambda i: (0, 0)),
            _resident((1, d), lambda i: (0, 0)),
        ],
        out_specs=pl.BlockSpec((tm, d), lambda i: (i, 0)),
        out_shape=jax.ShapeDtypeStruct((trunk.tokens, d), F32),
        compiler_params=_params("arbitrary"),
        name="outproj",
    )(a, x, ada, w, ln_g.reshape(1, d), ln_b.reshape(1, d))


def _att_outproj_body(o0_ref, o1_ref, o2_ref, l0_ref, l1_ref, l2_ref, x_ref, mod_ref, w_ref,
                      g_ref, b_ref, out_ref, tok_scr, *, trunk):
    def token_major(ref, g, slot):
        dil = ATT_GROUPS[g][1]
        if dil == 1:
            return ref[0, 0]
        rows = trunk.tile // dil
        slabs = ATT_W // LANES
        for r in range(dil):
            for cb in range(slabs):
                tok_scr[slot, cb, pl.ds(r, rows, stride=dil), :] = (
                    ref[0, r, :, cb * LANES:(cb + 1) * LANES])
        return jnp.concatenate([tok_scr[slot, cb] for cb in range(slabs)], axis=1)

    os = [token_major(r, g, g) for g, r in enumerate((o0_ref, o1_ref, o2_ref))]
    ls = [token_major(r, g, N_GROUPS + g) for g, r in enumerate((l0_ref, l1_ref, l2_ref))]
    m = jnp.maximum(jnp.maximum(ls[0], ls[1]), ls[2])
    ws = [jnp.exp(l - m) for l in ls]
    a = (ws[0] * os[0] + ws[1] * os[1] + ws[2] * os[2]) / (ws[0] + ws[1] + ws[2])
    _outproj_tail(a, x_ref, mod_ref, w_ref, g_ref, b_ref, out_ref, trunk)


def _att_outproj_prompt(outs, lses, x, ada, w, ln_g, ln_b, trunk, layer):
    d = D_MODEL
    tm = trunk.tile
    tps = trunk.tiles_per_seq
    k_in = w.shape[0]
    cls_specs = [pl.BlockSpec((1, dil, tm // dil, ATT_W), lambda i: (i // tps, 0, i % tps, 0))
                 for _, dil in ATT_GROUPS]
    return pl.pallas_call(
        functools.partial(_att_outproj_body, trunk=trunk),
        grid=(trunk.tokens // tm,),
        in_specs=cls_specs + cls_specs + [
            pl.BlockSpec((tm, d), lambda i: (i, 0)),
            trunk.mod_spec(layer, 1),
            _resident((k_in, d), lambda i: (0, 0)),
            _resident((1, d), lambda i: (0, 0)),
            _resident((1, d), lambda i: (0, 0)),
        ],
        out_specs=pl.BlockSpec((tm, d), lambda i: (i, 0)),
        out_shape=jax.ShapeDtypeStruct((trunk.tokens, d), F32),
        scratch_shapes=[pltpu.VMEM((2 * N_GROUPS, ATT_W // LANES, tm, LANES), F32)],
        compiler_params=_params("arbitrary"),
        name="att_outproj",
    )(*outs, *lses, x, ada, w, ln_g.reshape(1, d), ln_b.reshape(1, d))


def _group_norm_gate(o, g, gn_g, gn_b):
    mu = jnp.mean(o, -1, keepdims=True)
    oc = o - mu
    var = jnp.mean(oc * oc, -1, keepdims=True)
    on = oc * lax.rsqrt(var + NORM_EPS) * gn_g + gn_b
    return jax.nn.silu(g) * on


RET_TILE = 512


def _ret_decay_tables(chunk):
    lg = _log_gamma()
    pos = np.arange(chunk, dtype=np.float64)
    diff = pos[:, None] - pos[None, :]
    inner = np.where(diff >= 0, np.exp(np.maximum(diff, 0.0)[None] * lg[:, None, None]), 0.0)
    qd = np.exp((pos[None, :] + 1.0) * lg[:, None])
    kd = np.exp((chunk - 1.0 - pos[None, :]) * lg[:, None])
    cd = np.exp(chunk * lg)
    return inner, qd, kd, cd


def _ret_prompt_body(q_ref, k_ref, v_ref, g_ref, inner_ref, qd_ref, kd_ref, gng_ref, gnb_ref,
                     o_ref, sfin_ref, s_scr, *, cdec):
    t = pl.program_id(1)

    @pl.when(t == 0)
    def _():
        s_scr[...] = jnp.zeros_like(s_scr)

    for c in range(RET_TILE // RET_CHUNK):
        rows = slice(c * RET_CHUNK, (c + 1) * RET_CHUNK)
        for h in range(RET_HEADS):
            qk = slice(h * RET_DK, (h + 1) * RET_DK)
            vv = slice(h * RET_DV, (h + 1) * RET_DV)
            q = q_ref[rows, qk]
            k = k_ref[rows, qk] * (RET_DK ** -0.5)
            v = v_ref[rows, vv]
            scores = lax.dot_general(q, k, NT_DIMS, preferred_element_type=F32) * inner_ref[h]
            s_old = s_scr[h]
            o = (jnp.dot(scores.astype(BF16), v, preferred_element_type=F32)
                 + jnp.dot(q, s_old.astype(BF16), preferred_element_type=F32) * qd_ref[h])
            kdec = (k.astype(F32) * kd_ref[h]).astype(BF16)
            upd = lax.dot_general(kdec, v, TN_DIMS, preferred_element_type=F32)
            s_scr[h] = s_old * cdec[h] + upd
            gated = _group_norm_gate(o, g_ref[rows, vv].astype(F32), gng_ref[:, vv], gnb_ref[:, vv])
            o_ref[rows, vv] = gated.astype(o_ref.dtype)

    @pl.when(t == pl.num_programs(1) - 1)
    def _():
        sfin_ref[0] = s_scr[...]


def _ret_prompt(qkvg, gn_g, gn_b, n_seq, seq):
    inner, qd, kd, cd = _ret_decay_tables(RET_CHUNK)
    inner = jnp.asarray(inner, F32)
    qd_b = jnp.asarray(np.broadcast_to(qd[:, :, None], (RET_HEADS, RET_CHUNK, RET_DV)), F32)
    kd_b = jnp.asarray(np.broadcast_to(kd[:, :, None], (RET_HEADS, RET_CHUNK, RET_DK)), F32)
    cdec = tuple(float(np.float32(c)) for c in cd)
    tiles = seq // RET_TILE
    vw = RET_HEADS * RET_DV
    qw = RET_HEADS * RET_DK
    return pl.pallas_call(
        functools.partial(_ret_prompt_body, cdec=cdec),
        grid=(n_seq, tiles),
        in_specs=[
            pl.BlockSpec((RET_TILE, qw), lambda n, t: (n * tiles + t, 0)),
            pl.BlockSpec((RET_TILE, qw), lambda n, t: (n * tiles + t, 1)),
            pl.BlockSpec((RET_TILE, vw), lambda n, t: (n * tiles + t, 1)),
            pl.BlockSpec((RET_TILE, vw), lambda n, t: (n * tiles + t, 2)),
            _resident((RET_HEADS, RET_CHUNK, RET_CHUNK), lambda n, t: (0, 0, 0)),
            _resident((RET_HEADS, RET_CHUNK, RET_DV), lambda n, t: (0, 0, 0)),
            _resident((RET_HEADS, RET_CHUNK, RET_DK), lambda n, t: (0, 0, 0)),
            _resident((1, vw), lambda n, t: (0, 0)),
            _resident((1, vw), lambda n, t: (0, 0)),
        ],
        out_specs=[
            pl.BlockSpec((RET_TILE, vw), lambda n, t: (n * tiles + t, 0)),
            pl.BlockSpec((1, RET_HEADS, RET_DK, RET_DV), lambda n, t: (n, 0, 0, 0)),
        ],
        out_shape=[
            jax.ShapeDtypeStruct((n_seq * seq, vw), BF16),
            jax.ShapeDtypeStruct((n_seq, RET_HEADS, RET_DK, RET_DV), F32),
        ],
        scratch_shapes=[pltpu.VMEM((RET_HEADS, RET_DK, RET_DV), F32)],
        compiler_params=_params("arbitrary", "arbitrary"),
        name="retention_prompt",
    )(qkvg, qkvg, qkvg, qkvg, inner, qd_b, kd_b, gn_g.reshape(1, vw), gn_b.reshape(1, vw))


RET_SEQ_GROUP = SUBLANES
RET_S_ROWS = T_SAMPLE * RET_SEQ_GROUP


def _ret_sample_body(q_ref, k_ref, v_ref, g_ref, s_ref, w_ref, qd_ref, kd_ref, cd_ref,
                     gng_ref, gnb_ref, o_ref, so_ref):
    q = q_ref[...].reshape(RET_S_ROWS, RET_DK).astype(BF16)
    k = k_ref[...].reshape(RET_S_ROWS, RET_DK) * (RET_DK ** -0.5)
    v = v_ref[...].reshape(RET_S_ROWS, RET_DV).astype(BF16)
    g = g_ref[...].reshape(RET_S_ROWS, RET_DV)
    scores = lax.dot_general(q, k.astype(BF16), NT_DIMS, preferred_element_type=F32) * w_ref[0]
    o_intra = jnp.dot(scores.astype(BF16), v, preferred_element_type=F32)
    kdec = k * kd_ref[0]
    seq_of_row = lax.broadcasted_iota(jnp.int32, (RET_S_ROWS, 1), 0) % RET_SEQ_GROUP
    o_inter = jnp.zeros((RET_S_ROWS, RET_DV), F32)
    for j in range(RET_SEQ_GROUP):
        mine = seq_of_row == j
        s_old = s_ref[j, 0]
        r = jnp.dot(q, s_old.astype(BF16), preferred_element_type=F32)
        o_inter = jnp.where(mine, r, o_inter)
        kj = jnp.where(mine, kdec, 0.0).astype(BF16)
        upd = lax.dot_general(kj, v, TN_DIMS, preferred_element_type=F32)
        so_ref[j, 0] = s_old * cd_ref[0] + upd
    o = o_intra + o_inter * qd_ref[0]
    gated = _group_norm_gate(o, g, gng_ref[...], gnb_ref[...])
    o_ref[...] = gated.reshape(T_SAMPLE, RET_SEQ_GROUP, RET_DV).astype(o_ref.dtype)


RET_S_STEPS = (N_SAMPLE // RET_SEQ_GROUP) * RET_HEADS


def _ret_sample_part(qkvg, state, gn_g, gn_b):
    inner, qd, kd, cd = _ret_decay_tables(T_SAMPLE)
    row_t = np.arange(RET_S_ROWS) // RET_SEQ_GROUP
    row_j = np.arange(RET_S_ROWS) % RET_SEQ_GROUP
    same = (row_j[:, None] == row_j[None, :])
    w = np.where(same[None], inner[:, row_t[:, None], row_t[None, :]], 0.0)
    w = jnp.asarray(w, F32)
    qd_b = jnp.asarray(np.broadcast_to(qd[:, row_t, None], (RET_HEADS, RET_S_ROWS, RET_DV)), F32)
    kd_b = jnp.asarray(np.broadcast_to(kd[:, row_t, None], (RET_HEADS, RET_S_ROWS, RET_DK)), F32)
    cd_b = jnp.asarray(np.broadcast_to(cd[:, None, None], (RET_HEADS, 1, RET_DV)), F32)
    vw = RET_HEADS * RET_DV
    nq = RET_HEADS
    tg = (T_SAMPLE, RET_SEQ_GROUP)
    nh = RET_HEADS
    return _Part(
        _ret_sample_body,
        (qkvg, qkvg, qkvg, qkvg, state, w, qd_b, kd_b, cd_b,
         gn_g.reshape(1, vw), gn_b.reshape(1, vw)),
        [
            pl.BlockSpec(tg + (RET_DK,), lambda i: (0, i // nh, i % nh)),
            pl.BlockSpec(tg + (RET_DK,), lambda i: (0, i // nh, nq + i % nh)),
            pl.BlockSpec(tg + (RET_DV,), lambda i: (0, i // nh, nq + i % nh)),
            pl.BlockSpec(tg + (RET_DV,), lambda i: (0, i // nh, 2 * nq + i % nh)),
            pl.BlockSpec((RET_SEQ_GROUP, 1, RET_DK, RET_DV), lambda i: (i // nh, i % nh, 0, 0)),
            pl.BlockSpec((1, RET_S_ROWS, RET_S_ROWS), lambda i: (i % nh, 0, 0)),
            pl.BlockSpec((1, RET_S_ROWS, RET_DV), lambda i: (i % nh, 0, 0)),
            pl.BlockSpec((1, RET_S_ROWS, RET_DK), lambda i: (i % nh, 0, 0)),
            pl.BlockSpec((1, 1, RET_DV), lambda i: (i % nh, 0, 0)),
            pl.BlockSpec((1, RET_DV), lambda i: (0, i % nh)),
            pl.BlockSpec((1, RET_DV), lambda i: (0, i % nh)),
        ],
        [
            jax.ShapeDtypeStruct((T_SAMPLE, N_SAMPLE, vw), BF16),
            jax.ShapeDtypeStruct(state.shape, F32),
        ],
        [
            pl.BlockSpec(tg + (RET_DV,), lambda i: (0, i // nh, i % nh)),
            pl.BlockSpec((RET_SEQ_GROUP, 1, RET_DK, RET_DV), lambda i: (i // nh, i % nh, 0, 0)),
        ])


ATT_BQ = 512
ATT_SLAB = 256
HEADS_PER_SLAB = ATT_SLAB // ATT_DH


def _band_window(cur_ref, prev_ref, s):
    if s == 0:
        return jnp.concatenate([prev_ref[0, 0], cur_ref[0, 0, 0:NBACK, :]], axis=0)
    return cur_ref[0, 0, (s - 1) * NBACK:(s + 1) * NBACK, :]


def _band_body(q_ref, kc_ref, kp_ref, vc_ref, vp_ref, bias_ref, o_ref, lse_ref, s_scr, p_scr):
    i = pl.program_id(2)
    ki = lax.broadcasted_iota(jnp.int32, (NBACK, 2 * NBACK), 1)
    has_prev = ki >= jnp.where(i > 0, 0, NBACK)
    lane_head = lax.broadcasted_iota(jnp.int32, (1, ATT_SLAB), 1) // ATT_DH
    qmasks = [jnp.where(lane_head == hh, ATT_SCALE, 0.0).astype(BF16)
              for hh in range(HEADS_PER_SLAB)]
    nsub = ATT_BQ // NBACK
    pairs = nsub * ATT_HPG
    windows = {}

    def window(cur_ref, prev_ref, s):
        key = (id(cur_ref), s)
        if key not in windows:
            windows[key] = _band_window(cur_ref, prev_ref, s)
        return windows[key]

    def place(idx):
        s, h = divmod(idx, ATT_HPG)
        sl, hh = divmod(h, HEADS_PER_SLAB)
        return s, h, hh, slice(sl * ATT_SLAB, (sl + 1) * ATT_SLAB)

    def scores(idx):
        s, _, hh, lanes = place(idx)
        q = q_ref[0, 0, s * NBACK:(s + 1) * NBACK, lanes]
        s_scr[idx] = lax.dot_general(q * qmasks[hh], window(kc_ref, kp_ref, s)[:, lanes], NT_DIMS,
                                     preferred_element_type=F32)

    def softmax(idx):
        s, h, _, _ = place(idx)
        sc = s_scr[idx] + bias_ref[h]
        if s == 0:
            sc = jnp.where(has_prev, sc, -jnp.inf)
        m = jnp.max(sc, -1, keepdims=True)
        e = jnp.exp(sc - m)
        den = jnp.sum(e, -1, keepdims=True)
        p_scr[idx] = (e / den).astype(BF16)
        lse_ref[0, 0, s * NBACK:(s + 1) * NBACK, h * ATT_DH:(h + 1) * ATT_DH] = jnp.broadcast_to(
            m + jnp.log(den), (NBACK, ATT_DH))

    def values(idx):
        s, h, hh, lanes = place(idx)
        oh = jnp.dot(p_scr[idx], window(vc_ref, vp_ref, s)[:, lanes], preferred_element_type=F32)
        o_ref[0, 0, s * NBACK:(s + 1) * NBACK, h * ATT_DH:(h + 1) * ATT_DH] = (
            oh[:, hh * ATT_DH:(hh + 1) * ATT_DH])

    for stage in (scores, softmax, values):
        for idx in range(pairs):
            stage(idx)


def _band_bias(group):
    _, dil = ATT_GROUPS[group]
    steps = NBACK + np.arange(NBACK)[:, None] - np.arange(2 * NBACK)[None, :]
    valid = (steps >= 0) & (steps <= NBACK)
    slopes = _alibi_slopes()[group]
    bias = -(slopes[:, None, None] * dil) * steps[None]
    return np.where(valid[None], bias, -np.inf)


def _band_attention(qkv_cls, group):
    _, dil = ATT_GROUPS[group]
    n_seq, _, length, _ = qkv_cls.shape
    nsub = ATT_BQ // NBACK
    blocks = length // ATT_BQ
    pairs = nsub * ATT_HPG
    prev = lambda i: jnp.maximum(i * nsub - 1, 0)
    cur = lambda col: pl.BlockSpec((1, 1, ATT_BQ, ATT_W), lambda n, r, i: (n, r, i, col))
    halo = lambda col: pl.BlockSpec((1, 1, NBACK, ATT_W), lambda n, r, i: (n, r, prev(i), col))
    out_sds = jax.ShapeDtypeStruct((n_seq, dil, length, ATT_W), F32)
    return pl.pallas_call(
        _band_body,
        grid=(n_seq, dil, blocks),
        in_specs=[cur(0), cur(1), halo(1), cur(2), halo(2),
                  _resident((ATT_HPG, NBACK, 2 * NBACK), lambda n, r, i: (0, 0, 0))],
        out_specs=[cur(0), cur(0)],
        out_shape=[out_sds, out_sds],
        scratch_shapes=[pltpu.VMEM((pairs, NBACK, 2 * NBACK), F32),
                        pltpu.VMEM((pairs, NBACK, 2 * NBACK), BF16)],
        compiler_params=_params("arbitrary", "arbitrary", "arbitrary"),
        name="band_attention",
    )(qkv_cls, qkv_cls, qkv_cls, qkv_cls, qkv_cls, jnp.asarray(_band_bias(group), F32))


SA_PASSES = tuple((0, (t,)) for t in range(T_SAMPLE)) + ((1, tuple(range(T_SAMPLE))),
                                                        (2, tuple(range(T_SAMPLE))))
SA_MAX_LB = max(w for w, _ in ATT_GROUPS)


def _att_sample_tables():
    slopes = _alibi_slopes()
    sels, biases = [], []
    bias_new = np.full((N_GROUPS, T_SAMPLE * ATT_HPG, LANES), -np.inf)
    cs = np.zeros((len(SA_PASSES), SUBLANES, LANES))
    lane = np.arange(LANES)
    for g, (lb, dil) in enumerate(ATT_GROUPS):
        passes = [p for p in SA_PASSES if p[0] == g]
        sel = np.zeros((ATT_HPG, T_SAMPLE, len(passes), ATT_HPG, LANES))
        for pi, (_, t_set) in enumerate(passes):
            for t in t_set:
                owns = np.ones(LANES, bool) if dil == 1 else (lane % dil == t)
                for h in range(ATT_HPG):
                    sel[h, t, pi, h, owns] = 1.0
        sels.append(sel.reshape(ATT_HPG * T_SAMPLE, len(passes) * ATT_HPG * LANES))
        pos = np.arange(lb)
        bias = np.full((T_SAMPLE, ATT_HPG, lb), -np.inf)
        for t in range(T_SAMPLE):
            if dil == 1:
                j = lb + t - pos
                valid = j <= NBACK
            else:
                j = NBACK - pos // dil
                valid = (pos % dil) == t
            bias[t] = np.where(valid[None, :], -(slopes[g][:, None] * dil) * j[None, :], -np.inf)
            for h in range(ATT_HPG):
                for t2 in range(T_SAMPLE):
                    if (dil == 1 and t2 <= t) or t2 == t:
                        bias_new[g, t * ATT_HPG + h, t2] = -slopes[g][h] * dil * (t - t2)
        biases.append(bias)
    for pi, (g, t_set) in enumerate(SA_PASSES):
        dil = ATT_GROUPS[g][1]
        for t in t_set:
            cs[pi, t] = 1.0 if dil == 1 else (lane % dil == t)
    bd = (np.arange(ATT_W)[None, :] // ATT_DH == np.arange(ATT_HPG)[:, None]).astype(np.float64)
    return sels, biases, bias_new, cs, bd


SA_SEQ_PER_STEP = 2


def _att_scores_body(qd_ref, qrow_ref, kn_ref, c0_ref, c1_ref, c2_ref,
                     sel0_ref, sel1_ref, sel2_ref, b0_ref, b1_ref, b2_ref, bn_ref, bd_ref,
                     p0_ref, p1_ref, p2_ref, pn_ref, s_scr, e_scr):
    caches = (c0_ref, c1_ref, c2_ref)
    sels = (sel0_ref, sel1_ref, sel2_ref)
    biases = (b0_ref, b1_ref, b2_ref)
    p_refs = (p0_ref, p1_ref, p2_ref)
    bd = bd_ref[...]
    zpad = jnp.zeros((LANES - SUBLANES, ATT_W), F32)
    qsels, s_news = {}, {}
    for j in range(SA_SEQ_PER_STEP):
        for g in range(N_GROUPS):
            qsels[j, g] = jnp.dot(qd_ref[j, g].astype(BF16), sels[g][...],
                                  preferred_element_type=F32)
            qrows = qrow_ref[j, g]
            qbd = jnp.concatenate(
                [jnp.broadcast_to(qrows[t:t + 1], (ATT_HPG, ATT_W)) * bd
                 for t in range(T_SAMPLE)], 0)
            kn = jnp.concatenate([kn_ref[j, g], zpad], 0)
            s_news[j, g] = lax.dot_general(qbd.astype(BF16), kn.astype(BF16), NT_DIMS,
                                           preferred_element_type=F32) * ATT_SCALE + bn_ref[g]
    for j in range(SA_SEQ_PER_STEP):
        e_new, dens, lses = {}, {}, {}
        pid = 0
        for g, (lb, dil) in enumerate(ATT_GROUPS):
            chunks = lb // LANES
            passes = [p for p in SA_PASSES if p[0] == g]
            qsel, s_new = qsels[j, g], s_news[j, g]
            pass_of = {}
            for local, (_, t_set) in enumerate(passes):
                for h in range(ATT_HPG):
                    col = (local * ATT_HPG + h) * LANES
                    qs = qsel[:, col:col + LANES]
                    for c in range(chunks):
                        lanes = slice(c * LANES, (c + 1) * LANES)
                        prod = caches[g][j, 0, h, :, lanes] * qs
                        part = prod.reshape(ATT_DH // SUBLANES, SUBLANES, LANES).sum(0)
                        s_scr[pid, h:h + 1, lanes] = part.sum(0, keepdims=True)
                for t in t_set:
                    pass_of[t] = pid
                pid += 1
            for t in range(T_SAMPLE):
                sc = s_scr[pass_of[t], :, :lb] * ATT_SCALE + biases[g][t]
                sn = s_new[t * ATT_HPG:(t + 1) * ATT_HPG]
                m = jnp.maximum(jnp.max(sc, -1, keepdims=True), jnp.max(sn, -1, keepdims=True))
                e = jnp.exp(sc - m)
                en = jnp.exp(sn - m)
                den = jnp.sum(e, -1, keepdims=True) + jnp.sum(en, -1, keepdims=True)
                e_scr[g * T_SAMPLE + t, :, :lb] = e
                e_new[g, t], dens[g, t], lses[g, t] = en, den, m + jnp.log(den)
        wts = {}
        for t in range(T_SAMPLE):
            ls = [lses[g, t] for g in range(N_GROUPS)]
            m = jnp.maximum(jnp.maximum(ls[0], ls[1]), ls[2])
            ws = [jnp.exp(l - m) for l in ls]
            tot = ws[0] + ws[1] + ws[2]
            for g in range(N_GROUPS):
                wts[g, t] = ws[g] / (tot * dens[g, t])
        for g, (lb, dil) in enumerate(ATT_GROUPS):
            pn_ref[j, g] = jnp.concatenate(
                [e_new[g, t] * wts[g, t] for t in range(T_SAMPLE)], 0)
            for local, (_, t_set) in enumerate([p for p in SA_PASSES if p[0] == g]):
                pc = None
                for t in t_set:
                    term = e_scr[g * T_SAMPLE + t, :, :lb] * wts[g, t]
                    pc = term if pc is None else pc + term
                p_refs[g][j, local] = pc


def _att_values_body(p0_ref, p1_ref, p2_ref, pn_ref, vn_ref, c0_ref, c1_ref, c2_ref,
                     cs_ref, bd_ref, o_ref):
    caches = (c0_ref, c1_ref, c2_ref)
    p_refs = (p0_ref, p1_ref, p2_ref)
    bd = bd_ref[...]
    zpad = jnp.zeros((LANES - SUBLANES, ATT_W), F32)
    wfs = {}
    for j in range(SA_SEQ_PER_STEP):
        for pid, (g, _) in enumerate(SA_PASSES):
            local = pid - [p[0] for p in SA_PASSES].index(g)
            folded = []
            for h in range(ATT_HPG):
                acc = None
                for c in range(ATT_GROUPS[g][0] // LANES):
                    lanes = slice(c * LANES, (c + 1) * LANES)
                    term = caches[g][j, 0, h, :, lanes] * p_refs[g][j, local, h:h + 1, lanes]
                    acc = term if acc is None else acc + term
                folded.append(acc)
            wfs[j, pid] = jnp.concatenate(folded, 0).astype(BF16)
    for j in range(SA_SEQ_PER_STEP):
        o = jnp.zeros((SUBLANES, ATT_W), F32)
        o_new = jnp.zeros((T_SAMPLE * ATT_HPG, ATT_W), F32)
        for g in range(N_GROUPS):
            vn = jnp.concatenate([vn_ref[j, g], zpad], 0)
            o_new = o_new + jnp.dot(pn_ref[j, g].astype(BF16), vn.astype(BF16),
                                    preferred_element_type=F32)
        for pid in range(len(SA_PASSES)):
            o = o + lax.dot_general(cs_ref[pid].astype(BF16), wfs[j, pid], NT_DIMS,
                                    preferred_element_type=F32)
        extra = [jnp.sum(o_new[t * ATT_HPG:(t + 1) * ATT_HPG] * bd, 0, keepdims=True)
                 for t in range(T_SAMPLE)]
        o_ref[j] = o[:T_SAMPLE] + jnp.concatenate(extra, 0)


SA_STEPS = N_SAMPLE // SA_SEQ_PER_STEP


def _sa_rows(a):
    a = a.transpose(1, 2, 0, 3, 4).reshape(N_SAMPLE, N_GROUPS, T_SAMPLE, ATT_W)
    return jnp.pad(a, ((0, 0), (0, 0), (0, SUBLANES - T_SAMPLE), (0, 0)))


def _sa_cache_specs(caches, kv):
    views = [c.transpose(0, 2, 3, 4, 1) for c in caches]
    specs = [pl.BlockSpec((SA_SEQ_PER_STEP, 1, ATT_HPG, ATT_DH, lb), lambda i: (i, kv, 0, 0, 0))
             for lb, _ in ATT_GROUPS]
    return views, specs


def _sa_const_specs(tables):
    return [_resident(t.shape, lambda i, nd=t.ndim: (0,) * nd) for t in tables]


def _sa_prob_layout():
    shapes, specs = [], []
    for g, (lb, _) in enumerate(ATT_GROUPS):
        n_pass = len([p for p in SA_PASSES if p[0] == g])
        shapes.append(jax.ShapeDtypeStruct((N_SAMPLE, n_pass, ATT_HPG, lb), F32))
        specs.append(pl.BlockSpec((SA_SEQ_PER_STEP, n_pass, ATT_HPG, lb), lambda i: (i, 0, 0, 0)))
    shapes.append(jax.ShapeDtypeStruct((N_SAMPLE, N_GROUPS, T_SAMPLE * ATT_HPG, LANES), F32))
    specs.append(pl.BlockSpec((SA_SEQ_PER_STEP, N_GROUPS, T_SAMPLE * ATT_HPG, LANES),
                              lambda i: (i, 0, 0, 0)))
    return shapes, specs


def _att_scores_part(q, k_new, caches):
    sels, biases, bias_new, _, bd = _att_sample_tables()
    qd = q.transpose(1, 2, 4, 3, 0).reshape(N_SAMPLE, N_GROUPS, ATT_DH, ATT_HPG * T_SAMPLE)
    views, cache_specs = _sa_cache_specs(caches, 0)
    tables = ([jnp.asarray(s, BF16) for s in sels] + [jnp.asarray(b, F32) for b in biases]
              + [jnp.asarray(bias_new, F32), jnp.asarray(bd, F32)])
    row_spec = pl.BlockSpec((SA_SEQ_PER_STEP, N_GROUPS, SUBLANES, ATT_W), lambda i: (i, 0, 0, 0))
    out_shape, out_specs = _sa_prob_layout()
    return _Part(
        _att_scores_body,
        (qd, _sa_rows(q), _sa_rows(k_new), *views, *tables),
        [pl.BlockSpec((SA_SEQ_PER_STEP, N_GROUPS, ATT_DH, ATT_HPG * T_SAMPLE),
                      lambda i: (i, 0, 0, 0)), row_spec, row_spec]
        + cache_specs + _sa_const_specs(tables),
        out_shape, out_specs,
        [pltpu.VMEM((len(SA_PASSES), ATT_HPG, SA_MAX_LB), F32),
         pltpu.VMEM((N_GROUPS * T_SAMPLE, ATT_HPG, SA_MAX_LB), F32)])


def _att_values_part(probs, v_new, caches):
    _, _, _, cs, bd = _att_sample_tables()
    views, cache_specs = _sa_cache_specs(caches, 1)
    tables = [jnp.asarray(cs, F32), jnp.asarray(bd, F32)]
    row_spec = pl.BlockSpec((SA_SEQ_PER_STEP, N_GROUPS, SUBLANES, ATT_W), lambda i: (i, 0, 0, 0))
    _, prob_specs = _sa_prob_layout()
    return _Part(
        _att_values_body,
        (*probs, _sa_rows(v_new), *views, *tables),
        prob_specs + [row_spec] + cache_specs + _sa_const_specs(tables),
        [jax.ShapeDtypeStruct((N_SAMPLE, T_SAMPLE, ATT_W), F32)],
        [pl.BlockSpec((SA_SEQ_PER_STEP, T_SAMPLE, ATT_W), lambda i: (i, 0, 0))])


def kernel(x_prompt, x_sample, c_prompt, c_sample, state_ret, cache_kv_w128, cache_kv_w512,
           cache_kv_w2048, w_ada, b_ada, ln_g, ln_b, w_ffn_in, w_ffn_out, w_ret_in, ret_gn_g,
           ret_gn_b, w_ret_out, w_att_in, w_att_out):
    d = D_MODEL
    n_p, seq, _ = x_prompt.shape
    w_ffn_in_b = {(0, 0): w_ffn_in[0, 0].astype(BF16)}
    w_ffn_out_b = {(0, 0): w_ffn_out[0, 0].astype(BF16)}
    w_ret_in_b = w_ret_in.astype(BF16)
    later_ffn = [(0, 1), (1, 0), (1, 1)]
    cast_items = ([(w_ffn_in, lw) for lw in later_ffn] + [(w_ffn_out, lw) for lw in later_ffn]
                  + [(w_ret_out, ()), (w_att_in, ()), (w_att_out, ())])

    c_all = jnp.concatenate(
        [c_sample, c_prompt, jnp.zeros((ADA_ROWS - N_SAMPLE - n_p, d), F32)], axis=0)
    ada = _ada_table(c_all, w_ada, b_ada)

    prompt = _Trunk(n_p * seq, 512, False, seq // 512)
    sample = _Trunk(N_SAMPLE * T_SAMPLE, N_SAMPLE, True, 1)
    xp = x_prompt.reshape(n_p * seq, d)
    xs = x_sample.transpose(1, 0, 2).reshape(T_SAMPLE * N_SAMPLE, d)

    steps = RET_S_STEPS
    tile_s = n_p * seq // steps
    prompt_s = _Trunk(n_p * seq, tile_s, False, seq // tile_s)

    def ffn_part(x, trunk, layer, which):
        sub = 0 if which == 0 else 2
        return _ffn_part(x, ada, w_ffn_in_b[layer, which], w_ffn_out_b[layer, which],
                         ln_g[layer, sub], ln_b[layer, sub], trunk, layer, sub)

    def ffn(x, trunk, layer, which):
        return _run([ffn_part(x, trunk, layer, which)], trunk.tokens // trunk.tile, "ffn")[0][0]

    xs = ffn(xs, sample, 0, 0)
    qkvg_s = _inproj(xs, ada, w_ret_in_b, sample, 0, F32)

    (xp,), (gated_s, ret_s), cast = _run(
        [ffn_part(xp, prompt_s, 0, 0),
         _ret_sample_part(qkvg_s.reshape(T_SAMPLE, N_SAMPLE, -1), state_ret, ret_gn_g, ret_gn_b),
         _cast_part(cast_items, steps)],
        steps, "ffn_ret_sample")
    for k, lw in enumerate(later_ffn):
        w_ffn_in_b[lw], w_ffn_out_b[lw] = cast[k], cast[len(later_ffn) + k]
    w_ret_out_b, w_att_in_b, w_att_out_b = cast[2 * len(later_ffn):]

    qkvg_p = _inproj(xp, ada, w_ret_in_b, prompt, 0, BF16)
    gated_p, ret_p = _ret_prompt(qkvg_p, ret_gn_g, ret_gn_b, n_p, seq)
    xp = _outproj(gated_p, xp, ada, w_ret_out_b, ln_g[0, 1], ln_b[0, 1], prompt, 0)

    xs = _outproj(gated_s.reshape(T_SAMPLE * N_SAMPLE, -1), xs, ada, w_ret_out_b,
                  ln_g[0, 1], ln_b[0, 1], sample, 0)
    xs = ffn(xs, sample, 0, 1)
    xs = ffn(xs, sample, 1, 0)
    qkv_s = _inproj(xs, ada, w_att_in_b, sample, 1, F32)
    qkv_s = qkv_s.reshape(T_SAMPLE, N_SAMPLE, 3, N_GROUPS, ATT_HPG, ATT_DH)
    q_s, k_s, v_s = qkv_s[:, :, 0], qkv_s[:, :, 1], qkv_s[:, :, 2]
    caches = (cache_kv_w128, cache_kv_w512, cache_kv_w2048)

    probs, (xp,) = _run([_att_scores_part(q_s, k_s, caches), ffn_part(xp, prompt_s, 0, 1)],
                        steps, "att_scores_ffn")
    (xp,), (att_s,) = _run([ffn_part(xp, prompt_s, 1, 0), _att_values_part(probs, v_s, caches)],
                           steps, "ffn_att_values")

    keep_p = [min(w, seq) for w, _ in ATT_GROUPS]
    *qkv_cls, kv_tail = _att_inproj_prompt(xp, ada, w_att_in_b, prompt, 1, max(keep_p))
    band = [_band_attention(qkv_cls[g], g) for g in range(N_GROUPS)]
    xp = _att_outproj_prompt([o for o, _ in band], [l for _, l in band], xp, ada, w_att_out_b,
                             ln_g[1, 1], ln_b[1, 1], prompt, 1)
    kv_tail = kv_tail.reshape(n_p, max(keep_p), 2, N_GROUPS, ATT_HPG, ATT_DH)
    kv_p = [kv_tail[:, max(keep_p) - keep_p[g]:, :, g] for g in range(N_GROUPS)]

    att_s = att_s.transpose(1, 0, 2).reshape(T_SAMPLE * N_SAMPLE, ATT_W)
    xs = _outproj(att_s, xs, ada, w_att_out_b, ln_g[1, 1], ln_b[1, 1], sample, 1)
    kv_s = [jnp.stack([k_s[:, :, g], v_s[:, :, g]], axis=2).transpose(1, 0, 2, 3, 4)
            for g in range(N_GROUPS)]

    xp = ffn(xp, prompt, 1, 1)
    xs = ffn(xs, sample, 1, 1)

    y_prompt = xp.reshape(n_p, seq, d)
    y_sample = xs.reshape(T_SAMPLE, N_SAMPLE, d).transpose(1, 0, 2)
    return (y_prompt, y_sample, ret_p, ret_s, kv_p[0], kv_s[0], kv_p[1], kv_s[1], kv_p[2], kv_s[2])
```

```python
import functools

import numpy as np
import jax
import jax.numpy as jnp
from jax import lax
from jax.experimental import pallas as pl
from jax.experimental.pallas import tpu as pltpu

F32 = jnp.float32
BF16 = jnp.bfloat16

D_MODEL = 1024
DEPTH = 2
D_FF = 2816
RET_HEADS = 4
RET_DK = 256
RET_DV = 512
RET_CHUNK = 128
ATT_GROUPS = ((128, 1), (512, 4), (2048, 16))
N_GROUPS = 3
ATT_HPG = 8
ATT_DH = 64
ATT_W = ATT_HPG * ATT_DH
NBACK = 128
LSE_REP = 128 // ATT_HPG
ALPHA = (2 * DEPTH) ** 0.25
NORM_EPS = 1e-5
ATT_SCALE = ATT_DH ** -0.5

V7X_VMEM_LIMIT_BYTES = 60000 * 1024
LANES = 128
SUBLANES = 8

N_SAMPLE = 128
T_SAMPLE = 4
ADA_ROWS = 136
PROMPT_ROW_BLOCK = N_SAMPLE // SUBLANES

NT_DIMS = (((1,), (1,)), ((), ()))
TN_DIMS = (((0,), (0,)), ((), ()))


def _params(*semantics):
    return pltpu.CompilerParams(dimension_semantics=semantics,
                                vmem_limit_bytes=V7X_VMEM_LIMIT_BYTES)


def _resident(block, index_map):
    return pl.BlockSpec(block, index_map, pipeline_mode=pl.Buffered(1))


def _layer_norm(z, g, b):
    mu = jnp.mean(z, -1, keepdims=True)
    zc = z - mu
    var = jnp.mean(zc * zc, -1, keepdims=True)
    return zc * lax.rsqrt(var + NORM_EPS) * g + b


def _log_gamma():
    return np.log1p(-(2.0 ** (-5.0 - np.arange(RET_HEADS, dtype=np.float64))))


def _alibi_slopes():
    h = np.arange(1, N_GROUPS * ATT_HPG + 1, dtype=np.float64)
    return (2.0 ** (-8.0 * h / (N_GROUPS * ATT_HPG))).reshape(N_GROUPS, ATT_HPG)


def _ada_body(c_ref, w_ref, b_ref, o_ref):
    s = jax.nn.silu(c_ref[...]).astype(BF16)
    w = w_ref[0].astype(BF16)
    o_ref[0, 0] = jnp.dot(s, w, preferred_element_type=F32) + b_ref[0, 0]


def _ada_table(c_all, w_ada, b_ada):
    d = D_MODEL
    return pl.pallas_call(
        _ada_body,
        grid=(DEPTH, 9),
        in_specs=[
            _resident((ADA_ROWS, d), lambda i, j: (0, 0)),
            pl.BlockSpec((1, d, d), lambda i, j: (i, 0, j)),
            pl.BlockSpec((1, 1, 1, d), lambda i, j: (i, j, 0, 0)),
        ],
        out_specs=pl.BlockSpec((1, 1, ADA_ROWS, d), lambda i, j: (i, j, 0, 0)),
        out_shape=jax.ShapeDtypeStruct((DEPTH, 9, ADA_ROWS, d), F32),
        compiler_params=_params("arbitrary", "arbitrary"),
        name="ada_table",
    )(c_all, w_ada, b_ada.reshape(DEPTH, 9, 1, d))


class _Trunk:
    def __init__(self, tokens, tile, per_row, tiles_per_seq):
        self.tokens = tokens
        self.tile = tile
        self.per_row = per_row
        self.tiles_per_seq = tiles_per_seq

    def mod_spec(self, layer, sub):
        if self.per_row:
            return pl.BlockSpec((1, 3, self.tile, D_MODEL), lambda i: (layer, sub, 0, 0))
        return pl.BlockSpec((1, 3, SUBLANES, D_MODEL),
                            lambda i: (layer, sub, PROMPT_ROW_BLOCK, 0))

    def mod_rows(self, mod_ref, k):
        if self.per_row:
            return mod_ref[0, k]
        n = pl.program_id(0) // self.tiles_per_seq
        return mod_ref[0, k, pl.ds(n, 1), :]


def _ffn_body(x_ref, mod_ref, win_ref, wout_ref, g_ref, b_ref, o_ref, *, trunk):
    x = x_ref[...]
    shift = trunk.mod_rows(mod_ref, 0)
    scale = trunk.mod_rows(mod_ref, 1)
    gate = trunk.mod_rows(mod_ref, 2)
    u = (x * (1.0 + scale) + shift).astype(BF16)
    h = jnp.dot(u, win_ref[...], preferred_element_type=F32)
    a = h[:, :D_FF]
    b = h[:, D_FF:]
    act = (jax.nn.silu(a) * b).astype(BF16)
    y = jnp.dot(act, wout_ref[...], preferred_element_type=F32)
    z = ALPHA * x + (0.5 * (1.0 + gate)) * y
    o_ref[...] = _layer_norm(z, g_ref[...], b_ref[...])


class _Part:
    def __init__(self, body, args, in_specs, out_shape, out_specs, scratch=()):
        self.body, self.args, self.in_specs = body, list(args), list(in_specs)
        self.out_shape, self.out_specs, self.scratch = list(out_shape), list(out_specs), list(scratch)


def _run(parts, steps, name):
    n_in = [len(p.args) for p in parts]
    n_out = [len(p.out_shape) for p in parts]
    n_scr = [len(p.scratch) for p in parts]

    def body(*refs):
        ins, outs, scr = refs[:sum(n_in)], refs[sum(n_in):sum(n_in) + sum(n_out)], refs[sum(n_in) + sum(n_out):]
        a = b = c = 0
        for k, p in enumerate(parts):
            p.body(*ins[a:a + n_in[k]], *outs[b:b + n_out[k]], *scr[c:c + n_scr[k]])
            a, b, c = a + n_in[k], b + n_out[k], c + n_scr[k]

    res = pl.pallas_call(
        body,
        grid=(steps,),
        in_specs=[s for p in parts for s in p.in_specs],
        out_specs=[s for p in parts for s in p.out_specs],
        out_shape=[s for p in parts for s in p.out_shape],
        scratch_shapes=[s for p in parts for s in p.scratch],
        compiler_params=_params("arbitrary"),
        name=name,
    )(*[a for p in parts for a in p.args])
    out, b = [], 0
    for k in range(len(parts)):
        out.append(res[b:b + n_out[k]])
        b += n_out[k]
    return out


def _cast_body(*refs):
    n = len(refs) // 2
    for src, dst in zip(refs[:n], refs[n:]):
        dst[...] = src[...].reshape(dst.shape).astype(dst.dtype)


def _cast_part(items, steps):
    args, in_specs, out_shape, out_specs = [], [], [], []
    for arr, lead in items:
        rows, cols = arr.shape[-2:]
        n_blocks = steps
        while rows % n_blocks or (rows // n_blocks) % (2 * SUBLANES):
            n_blocks //= 2
        blk = rows // n_blocks
        args.append(arr)
        in_specs.append(pl.BlockSpec(
            (1,) * len(lead) + (blk, cols),
            lambda i, lead=lead, last=n_blocks - 1: lead + (jnp.minimum(i, last), 0)))
        out_shape.append(jax.ShapeDtypeStruct((rows, cols), BF16))
        out_specs.append(pl.BlockSpec((blk, cols),
                                      lambda i, last=n_blocks - 1: (jnp.minimum(i, last), 0)))
    return _Part(_cast_body, args, in_specs, out_shape, out_specs)


def _ffn_part(x, ada, w_in, w_out, ln_g, ln_b, trunk, layer, sub):
    d = D_MODEL
    tm = trunk.tile
    return _Part(
        functools.partial(_ffn_body, trunk=trunk),
        (x, ada, w_in, w_out, ln_g.reshape(1, d), ln_b.reshape(1, d)),
        [
            pl.BlockSpec((tm, d), lambda i: (i, 0)),
            trunk.mod_spec(layer, sub),
            _resident((d, 2 * D_FF), lambda i: (0, 0)),
            _resident((D_FF, d), lambda i: (0, 0)),
            _resident((1, d), lambda i: (0, 0)),
            _resident((1, d), lambda i: (0, 0)),
        ],
        [jax.ShapeDtypeStruct((trunk.tokens, d), F32)],
        [pl.BlockSpec((tm, d), lambda i: (i, 0))])


FFN_STREAM_CHUNK = 256


def _ffn_stream_body(x_ref, mod_ref, wa_ref, wb_ref, wout_ref, g_ref, b_ref, o_ref, u_scr, y_scr):
    c = pl.program_id(0)
    reps = x_ref.shape[0] // mod_ref.shape[2]

    def rows(k):
        return jnp.concatenate([mod_ref[0, k]] * reps, axis=0)

    @pl.when(c == 0)
    def _():
        u_scr[...] = (x_ref[...] * (1.0 + rows(1)) + rows(0)).astype(BF16)
        y_scr[...] = jnp.zeros_like(y_scr)

    u = u_scr[...]
    a = jnp.dot(u, wa_ref[...], preferred_element_type=F32)
    b = jnp.dot(u, wb_ref[...], preferred_element_type=F32)
    act = (jax.nn.silu(a) * b).astype(BF16)
    y_scr[...] += jnp.dot(act, wout_ref[...], preferred_element_type=F32)

    @pl.when(c == pl.num_programs(0) - 1)
    def _():
        z = ALPHA * x_ref[...] + (0.5 * (1.0 + rows(2))) * y_scr[...]
        o_ref[...] = _layer_norm(z, g_ref[...], b_ref[...])


def _ffn_stream(x, ada, w_in, w_out, ln_g, ln_b, layer, sub):
    d = D_MODEL
    tokens = x.shape[0]
    ch = FFN_STREAM_CHUNK
    n_chunks = D_FF // ch
    return pl.pallas_call(
        _ffn_stream_body,
        grid=(n_chunks,),
        in_specs=[
            _resident((tokens, d), lambda c: (0, 0)),
            _resident((1, 3, N_SAMPLE, d), lambda c: (layer, sub, 0, 0)),
            pl.BlockSpec((d, ch), lambda c: (0, c)),
            pl.BlockSpec((d, ch), lambda c: (0, n_chunks + c)),
            pl.BlockSpec((ch, d), lambda c: (c, 0)),
            _resident((1, d), lambda c: (0, 0)),
            _resident((1, d), lambda c: (0, 0)),
        ],
        out_specs=pl.BlockSpec((tokens, d), lambda c: (0, 0)),
        out_shape=jax.ShapeDtypeStruct((tokens, d), F32),
        scratch_shapes=[pltpu.VMEM((tokens, d), BF16), pltpu.VMEM((tokens, d), F32)],
        compiler_params=_params("arbitrary"),
        name="ffn_stream",
    )(x, ada, w_in, w_in, w_out, ln_g.reshape(1, d), ln_b.reshape(1, d))


def _inproj_body(x_ref, mod_ref, w_ref, o_ref, *, trunk):
    x = x_ref[...]
    u = (x * (1.0 + trunk.mod_rows(mod_ref, 1)) + trunk.mod_rows(mod_ref, 0)).astype(BF16)
    o_ref[...] = jnp.dot(u, w_ref[...], preferred_element_type=F32).astype(o_ref.dtype)


def _inproj(x, ada, w, trunk, layer, out_dtype):
    d = D_MODEL
    tm = trunk.tile
    n_out = w.shape[1]
    return pl.pallas_call(
        functools.partial(_inproj_body, trunk=trunk),
        grid=(trunk.tokens // tm,),
        in_specs=[
            pl.BlockSpec((tm, d), lambda i: (i, 0)),
            trunk.mod_spec(layer, 1),
            _resident((d, n_out), lambda i: (0, 0)),
        ],
        out_specs=pl.BlockSpec((tm, n_out), lambda i: (i, 0)),
        out_shape=jax.ShapeDtypeStruct((trunk.tokens, n_out), out_dtype),
        compiler_params=_params("arbitrary"),
        name="inproj",
    )(x, ada, w)


def _att_inproj_body(x_ref, mod_ref, w_ref, a0_ref, a1_ref, a2_ref, tail_ref, y_scr, *, trunk):
    x = x_ref[...]
    u = (x * (1.0 + trunk.mod_rows(mod_ref, 1)) + trunk.mod_rows(mod_ref, 0)).astype(BF16)
    slabs = ATT_W // LANES
    a_refs = (a0_ref, a1_ref, a2_ref)
    for part in range(3):
        for g in range(N_GROUPS):
            col0 = (part * N_GROUPS + g) * ATT_W
            for c in range(ATT_W // ATT_SLAB):
                lo = col0 + c * ATT_SLAB
                y = jnp.dot(u, w_ref[:, lo:lo + ATT_SLAB], preferred_element_type=F32)
                if part > 0:
                    tail_ref[0, :, lo - N_GROUPS * ATT_W:lo - N_GROUPS * ATT_W + ATT_SLAB] = y
                for half in range(ATT_SLAB // LANES):
                    y_scr[lo // LANES + half] = y[:, half * LANES:(half + 1) * LANES]
            dil = ATT_GROUPS[g][1]
            rows = trunk.tile // dil
            for r in range(dil):
                sel = pl.ds(r, rows, stride=dil) if dil > 1 else slice(None)
                for cb in range(slabs):
                    dst = (part * slabs + cb) * LANES
                    a_refs[g][0, r, :, dst:dst + LANES] = (
                        y_scr[col0 // LANES + cb, sel, :].astype(BF16))


def _att_inproj_prompt(x, ada, w, trunk, layer, tail_rows):
    d = D_MODEL
    tm = trunk.tile
    n_out = w.shape[1]
    tps = trunk.tiles_per_seq
    n_seq = trunk.tokens // (tps * tm)
    first = tps - tail_rows // tm
    out_shape, out_specs = [], []
    for _, dil in ATT_GROUPS:
        out_shape.append(jax.ShapeDtypeStruct((n_seq, dil, tps * tm // dil, 3 * ATT_W), BF16))
        out_specs.append(pl.BlockSpec((1, dil, tm // dil, 3 * ATT_W),
                                      lambda i: (i // tps, 0, i % tps, 0)))
    kv_w = n_out - N_GROUPS * ATT_W
    out_shape.append(jax.ShapeDtypeStruct((n_seq, tail_rows, kv_w), F32))
    out_specs.append(pl.BlockSpec((1, tm, kv_w),
                                  lambda i: (i // tps, jnp.maximum(i % tps - first, 0), 0)))
    return pl.pallas_call(
        functools.partial(_att_inproj_body, trunk=trunk),
        grid=(trunk.tokens // tm,),
        in_specs=[
            pl.BlockSpec((tm, d), lambda i: (i, 0)),
            trunk.mod_spec(layer, 1),
            _resident((d, n_out), lambda i: (0, 0)),
        ],
        out_specs=out_specs,
        out_shape=out_shape,
        scratch_shapes=[pltpu.VMEM((n_out // LANES, tm, LANES), F32)],
        compiler_params=_params("arbitrary"),
        name="att_inproj",
    )(x, ada, w)


def _outproj_tail(a, x_ref, mod_ref, w_ref, g_ref, b_ref, o_ref, trunk):
    x = x_ref[...]
    gate = trunk.mod_rows(mod_ref, 2)
    y = jnp.dot(a.astype(BF16), w_ref[...], preferred_element_type=F32)
    z = ALPHA * x + (1.0 + gate) * y
    o_ref[...] = _layer_norm(z, g_ref[...], b_ref[...])


def _outproj_body(a_ref, x_ref, mod_ref, w_ref, g_ref, b_ref, o_ref, *, trunk):
    _outproj_tail(a_ref[...], x_ref, mod_ref, w_ref, g_ref, b_ref, o_ref, trunk)


def _outproj(a, x, ada, w, ln_g, ln_b, trunk, layer):
    d = D_MODEL
    tm = trunk.tile
    k_in = w.shape[0]
    return pl.pallas_call(
        functools.partial(_outproj_body, trunk=trunk),
        grid=(trunk.tokens // tm,),
        in_specs=[
            pl.BlockSpec((tm, k_in), lambda i: (i, 0)),
            pl.BlockSpec((tm, d), lambda i: (i, 0)),
            trunk.mod_spec(layer, 1),
            _resident((k_in, d), lambda i: (0, 0)),
            _resident((1, d), lambda i: (0, 0)),
            _resident((1, d), lambda i: (0, 0)),
        ],
        out_specs=pl.BlockSpec((tm, d), lambda i: (i, 0)),
        out_shape=jax.ShapeDtypeStruct((trunk.tokens, d), F32),
        compiler_params=_params("arbitrary"),
        name="outproj",
    )(a, x, ada, w, ln_g.reshape(1, d), ln_b.reshape(1, d))


def _att_outproj_body(o0_ref, o1_ref, o2_ref, l0_ref, l1_ref, l2_ref, e_ref, x_ref, mod_ref, w_ref,
                      g_ref, b_ref, out_ref, tok_scr, *, trunk):
    def token_major(ref, g, first_slab):
        dil = ATT_GROUPS[g][1]
        slabs = ref.shape[-1] // LANES
        if dil == 1:
            return ref[0, 0]
        rows = trunk.tile // dil
        for r in range(dil):
            for cb in range(slabs):
                tok_scr[first_slab + cb, pl.ds(r, rows, stride=dil), :] = (
                    ref[0, r, :, cb * LANES:(cb + 1) * LANES])
        return jnp.concatenate([tok_scr[first_slab + cb] for cb in range(slabs)], axis=1)

    o_slabs = ATT_W // LANES
    os = [token_major(r, g, g * o_slabs) for g, r in enumerate((o0_ref, o1_ref, o2_ref))]
    ls = [token_major(r, g, N_GROUPS * o_slabs + g)
          for g, r in enumerate((l0_ref, l1_ref, l2_ref))]
    m = jnp.maximum(jnp.maximum(ls[0], ls[1]), ls[2])
    ws = [jnp.exp(l - m) for l in ls]
    inv = 1.0 / (ws[0] + ws[1] + ws[2])
    a = None
    for g in range(N_GROUPS):
        alpha = ws[g] * inv
        hi = alpha.astype(BF16)
        lo = (alpha - hi.astype(F32)).astype(BF16)
        spread = (jnp.dot(hi, e_ref[...], preferred_element_type=F32)
                  + jnp.dot(lo, e_ref[...], preferred_element_type=F32))
        term = spread * os[g]
        a = term if a is None else a + term
    _outproj_tail(a, x_ref, mod_ref, w_ref, g_ref, b_ref, out_ref, trunk)


def _att_outproj_prompt(outs, lses, x, ada, w, ln_g, ln_b, trunk, layer):
    d = D_MODEL
    tm = trunk.tile
    tps = trunk.tiles_per_seq
    k_in = w.shape[0]
    cls_specs = lambda width: [
        pl.BlockSpec((1, dil, tm // dil, width), lambda i: (i // tps, 0, i % tps, 0))
        for _, dil in ATT_GROUPS]
    e = (np.arange(LANES)[:, None] // LSE_REP == np.arange(ATT_W)[None, :] // ATT_DH) / LSE_REP
    return pl.pallas_call(
        functools.partial(_att_outproj_body, trunk=trunk),
        grid=(trunk.tokens // tm,),
        in_specs=cls_specs(ATT_W) + cls_specs(LANES) + [
            _resident((LANES, ATT_W), lambda i: (0, 0)),
            pl.BlockSpec((tm, d), lambda i: (i, 0)),
            trunk.mod_spec(layer, 1),
            _resident((k_in, d), lambda i: (0, 0)),
            _resident((1, d), lambda i: (0, 0)),
            _resident((1, d), lambda i: (0, 0)),
        ],
        out_specs=pl.BlockSpec((tm, d), lambda i: (i, 0)),
        out_shape=jax.ShapeDtypeStruct((trunk.tokens, d), F32),
        scratch_shapes=[pltpu.VMEM((N_GROUPS * (ATT_W // LANES + 1), tm, LANES), F32)],
        compiler_params=_params("arbitrary"),
        name="att_outproj",
    )(*outs, *lses, jnp.asarray(e, BF16), x, ada, w, ln_g.reshape(1, d), ln_b.reshape(1, d))


def _group_norm_gate(o, g, gn_g, gn_b):
    mu = jnp.mean(o, -1, keepdims=True)
    oc = o - mu
    var = jnp.mean(oc * oc, -1, keepdims=True)
    on = oc * lax.rsqrt(var + NORM_EPS) * gn_g + gn_b
    return jax.nn.silu(g) * on


RET_TILE = 512


def _ret_decay_tables(chunk):
    lg = _log_gamma()
    pos = np.arange(chunk, dtype=np.float64)
    diff = pos[:, None] - pos[None, :]
    inner = np.where(diff >= 0, np.exp(np.maximum(diff, 0.0)[None] * lg[:, None, None]), 0.0)
    qd = np.exp((pos[None, :] + 1.0) * lg[:, None])
    kd = np.exp((chunk - 1.0 - pos[None, :]) * lg[:, None])
    cd = np.exp(chunk * lg)
    return inner, qd, kd, cd


def _ret_prompt_body(q_ref, k_ref, v_ref, g_ref, inner_ref, qd_ref, kd_ref, gng_ref, gnb_ref,
                     o_ref, sfin_ref, s_scr, *, cdec):
    t = pl.program_id(1)

    @pl.when(t == 0)
    def _():
        s_scr[...] = jnp.zeros_like(s_scr)

    for c in range(RET_TILE // RET_CHUNK):
        rows = slice(c * RET_CHUNK, (c + 1) * RET_CHUNK)
        for h in range(RET_HEADS):
            qk = slice(h * RET_DK, (h + 1) * RET_DK)
            vv = slice(h * RET_DV, (h + 1) * RET_DV)
            q = q_ref[rows, qk]
            k = k_ref[rows, qk] * (RET_DK ** -0.5)
            v = v_ref[rows, vv]
            scores = lax.dot_general(q, k, NT_DIMS, preferred_element_type=F32) * inner_ref[h]
            s_old = s_scr[h]
            o = (jnp.dot(scores.astype(BF16), v, preferred_element_type=F32)
                 + jnp.dot(q, s_old.astype(BF16), preferred_element_type=F32) * qd_ref[h])
            kdec = (k.astype(F32) * kd_ref[h]).astype(BF16)
            upd = lax.dot_general(kdec, v, TN_DIMS, preferred_element_type=F32)
            s_scr[h] = s_old * cdec[h] + upd
            gated = _group_norm_gate(o, g_ref[rows, vv].astype(F32), gng_ref[:, vv], gnb_ref[:, vv])
            o_ref[rows, vv] = gated.astype(o_ref.dtype)

    @pl.when(t == pl.num_programs(1) - 1)
    def _():
        sfin_ref[0] = s_scr[...]


def _ret_prompt(qkvg, gn_g, gn_b, n_seq, seq):
    inner, qd, kd, cd = _ret_decay_tables(RET_CHUNK)
    inner = jnp.asarray(inner, F32)
    qd_b = jnp.asarray(np.broadcast_to(qd[:, :, None], (RET_HEADS, RET_CHUNK, RET_DV)), F32)
    kd_b = jnp.asarray(np.broadcast_to(kd[:, :, None], (RET_HEADS, RET_CHUNK, RET_DK)), F32)
    cdec = tuple(float(np.float32(c)) for c in cd)
    tiles = seq // RET_TILE
    vw = RET_HEADS * RET_DV
    qw = RET_HEADS * RET_DK
    return pl.pallas_call(
        functools.partial(_ret_prompt_body, cdec=cdec),
        grid=(n_seq, tiles),
        in_specs=[
            pl.BlockSpec((RET_TILE, qw), lambda n, t: (n * tiles + t, 0)),
            pl.BlockSpec((RET_TILE, qw), lambda n, t: (n * tiles + t, 1)),
            pl.BlockSpec((RET_TILE, vw), lambda n, t: (n * tiles + t, 1)),
            pl.BlockSpec((RET_TILE, vw), lambda n, t: (n * tiles + t, 2)),
            _resident((RET_HEADS, RET_CHUNK, RET_CHUNK), lambda n, t: (0, 0, 0)),
            _resident((RET_HEADS, RET_CHUNK, RET_DV), lambda n, t: (0, 0, 0)),
            _resident((RET_HEADS, RET_CHUNK, RET_DK), lambda n, t: (0, 0, 0)),
            _resident((1, vw), lambda n, t: (0, 0)),
            _resident((1, vw), lambda n, t: (0, 0)),
        ],
        out_specs=[
            pl.BlockSpec((RET_TILE, vw), lambda n, t: (n * tiles + t, 0)),
            pl.BlockSpec((1, RET_HEADS, RET_DK, RET_DV), lambda n, t: (n, 0, 0, 0)),
        ],
        out_shape=[
            jax.ShapeDtypeStruct((n_seq * seq, vw), BF16),
            jax.ShapeDtypeStruct((n_seq, RET_HEADS, RET_DK, RET_DV), F32),
        ],
        scratch_shapes=[pltpu.VMEM((RET_HEADS, RET_DK, RET_DV), F32)],
        compiler_params=_params("arbitrary", "arbitrary"),
        name="retention_prompt",
    )(qkvg, qkvg, qkvg, qkvg, inner, qd_b, kd_b, gn_g.reshape(1, vw), gn_b.reshape(1, vw))


RET_SEQ_GROUP = SUBLANES
RET_S_ROWS = T_SAMPLE * RET_SEQ_GROUP


def _ret_sample_body(q_ref, k_ref, v_ref, g_ref, s_ref, w_ref, qd_ref, kd_ref, cd_ref,
                     gng_ref, gnb_ref, o_ref, so_ref):
    q = q_ref[...].reshape(RET_S_ROWS, RET_DK).astype(BF16)
    k = k_ref[...].reshape(RET_S_ROWS, RET_DK) * (RET_DK ** -0.5)
    v = v_ref[...].reshape(RET_S_ROWS, RET_DV).astype(BF16)
    g = g_ref[...].reshape(RET_S_ROWS, RET_DV)
    scores = lax.dot_general(q, k.astype(BF16), NT_DIMS, preferred_element_type=F32) * w_ref[0]
    o_intra = jnp.dot(scores.astype(BF16), v, preferred_element_type=F32)
    kdec = k * kd_ref[0]
    seq_of_row = lax.broadcasted_iota(jnp.int32, (RET_S_ROWS, 1), 0) % RET_SEQ_GROUP
    o_inter = jnp.zeros((RET_S_ROWS, RET_DV), F32)
    for j in range(RET_SEQ_GROUP):
        mine = seq_of_row == j
        s_old = s_ref[j, 0]
        r = jnp.dot(q, s_old.astype(BF16), preferred_element_type=F32)
        o_inter = jnp.where(mine, r, o_inter)
        kj = jnp.where(mine, kdec, 0.0).astype(BF16)
        upd = lax.dot_general(kj, v, TN_DIMS, preferred_element_type=F32)
        so_ref[j, 0] = s_old * cd_ref[0] + upd
    o = o_intra + o_inter * qd_ref[0]
    gated = _group_norm_gate(o, g, gng_ref[...], gnb_ref[...])
    o_ref[...] = gated.reshape(T_SAMPLE, RET_SEQ_GROUP, RET_DV).astype(o_ref.dtype)


RET_S_STEPS = (N_SAMPLE // RET_SEQ_GROUP) * RET_HEADS


def _ret_sample_part(qkvg, state, gn_g, gn_b):
    inner, qd, kd, cd = _ret_decay_tables(T_SAMPLE)
    row_t = np.arange(RET_S_ROWS) // RET_SEQ_GROUP
    row_j = np.arange(RET_S_ROWS) % RET_SEQ_GROUP
    same = (row_j[:, None] == row_j[None, :])
    w = np.where(same[None], inner[:, row_t[:, None], row_t[None, :]], 0.0)
    w = jnp.asarray(w, F32)
    qd_b = jnp.asarray(np.broadcast_to(qd[:, row_t, None], (RET_HEADS, RET_S_ROWS, RET_DV)), F32)
    kd_b = jnp.asarray(np.broadcast_to(kd[:, row_t, None], (RET_HEADS, RET_S_ROWS, RET_DK)), F32)
    cd_b = jnp.asarray(np.broadcast_to(cd[:, None, None], (RET_HEADS, 1, RET_DV)), F32)
    vw = RET_HEADS * RET_DV
    nq = RET_HEADS
    tg = (T_SAMPLE, RET_SEQ_GROUP)
    nh = RET_HEADS
    return _Part(
        _ret_sample_body,
        (qkvg, qkvg, qkvg, qkvg, state, w, qd_b, kd_b, cd_b,
         gn_g.reshape(1, vw), gn_b.reshape(1, vw)),
        [
            pl.BlockSpec(tg + (RET_DK,), lambda i: (0, i // nh, i % nh)),
            pl.BlockSpec(tg + (RET_DK,), lambda i: (0, i // nh, nq + i % nh)),
            pl.BlockSpec(tg + (RET_DV,), lambda i: (0, i // nh, nq + i % nh)),
            pl.BlockSpec(tg + (RET_DV,), lambda i: (0, i // nh, 2 * nq + i % nh)),
            pl.BlockSpec((RET_SEQ_GROUP, 1, RET_DK, RET_DV), lambda i: (i // nh, i % nh, 0, 0)),
            pl.BlockSpec((1, RET_S_ROWS, RET_S_ROWS), lambda i: (i % nh, 0, 0)),
            pl.BlockSpec((1, RET_S_ROWS, RET_DV), lambda i: (i % nh, 0, 0)),
            pl.BlockSpec((1, RET_S_ROWS, RET_DK), lambda i: (i % nh, 0, 0)),
            pl.BlockSpec((1, 1, RET_DV), lambda i: (i % nh, 0, 0)),
            pl.BlockSpec((1, RET_DV), lambda i: (0, i % nh)),
            pl.BlockSpec((1, RET_DV), lambda i: (0, i % nh)),
        ],
        [
            jax.ShapeDtypeStruct((T_SAMPLE, N_SAMPLE, vw), BF16),
            jax.ShapeDtypeStruct(state.shape, F32),
        ],
        [
            pl.BlockSpec(tg + (RET_DV,), lambda i: (0, i // nh, i % nh)),
            pl.BlockSpec((RET_SEQ_GROUP, 1, RET_DK, RET_DV), lambda i: (i // nh, i % nh, 0, 0)),
        ])


ATT_BQ = 512
ATT_SLAB = 256
HEADS_PER_SLAB = ATT_SLAB // ATT_DH


def _band_window(cur_ref, prev_ref, s):
    if s == 0:
        return jnp.concatenate([prev_ref[0, 0], cur_ref[0, 0, 0:NBACK, :]], axis=0)
    return cur_ref[0, 0, (s - 1) * NBACK:(s + 1) * NBACK, :]


def _band_body(q_ref, kc_ref, kp_ref, vc_ref, vp_ref, bias_ref, o_ref, lse_ref, s_scr, p_scr):
    i = pl.program_id(2)
    ki = lax.broadcasted_iota(jnp.int32, (NBACK, 2 * NBACK), 1)
    has_prev = ki >= jnp.where(i > 0, 0, NBACK)
    lane_head = lax.broadcasted_iota(jnp.int32, (1, ATT_SLAB), 1) // ATT_DH
    qmasks = [jnp.where(lane_head == hh, ATT_SCALE, 0.0).astype(BF16)
              for hh in range(HEADS_PER_SLAB)]
    nsub = ATT_BQ // NBACK
    pairs = nsub * ATT_HPG
    windows = {}

    def window(cur_ref, prev_ref, s):
        key = (id(cur_ref), s)
        if key not in windows:
            windows[key] = _band_window(cur_ref, prev_ref, s)
        return windows[key]

    def place(idx):
        s, h = divmod(idx, ATT_HPG)
        sl, hh = divmod(h, HEADS_PER_SLAB)
        return s, h, hh, slice(sl * ATT_SLAB, (sl + 1) * ATT_SLAB)

    def scores(idx):
        s, _, hh, lanes = place(idx)
        q = q_ref[0, 0, s * NBACK:(s + 1) * NBACK, lanes]
        s_scr[idx] = lax.dot_general(q * qmasks[hh], window(kc_ref, kp_ref, s)[:, lanes], NT_DIMS,
                                     preferred_element_type=F32)

    def softmax(idx):
        s, h, _, _ = place(idx)
        sc = s_scr[idx] + bias_ref[h]
        if s == 0:
            sc = jnp.where(has_prev, sc, -jnp.inf)
        m = jnp.max(sc, -1, keepdims=True)
        e = jnp.exp(sc - m)
        den = jnp.sum(e, -1, keepdims=True)
        p_scr[idx] = (e / den).astype(BF16)
        lse_ref[0, 0, s * NBACK:(s + 1) * NBACK, h * LSE_REP:(h + 1) * LSE_REP] = jnp.broadcast_to(
            m + jnp.log(den), (NBACK, LSE_REP))

    def values(idx):
        s, h, hh, lanes = place(idx)
        oh = jnp.dot(p_scr[idx], window(vc_ref, vp_ref, s)[:, lanes], preferred_element_type=F32)
        o_ref[0, 0, s * NBACK:(s + 1) * NBACK, h * ATT_DH:(h + 1) * ATT_DH] = (
            oh[:, hh * ATT_DH:(hh + 1) * ATT_DH])

    for stage in (scores, softmax, values):
        for idx in range(pairs):
            stage(idx)


def _band_bias(group):
    _, dil = ATT_GROUPS[group]
    steps = NBACK + np.arange(NBACK)[:, None] - np.arange(2 * NBACK)[None, :]
    valid = (steps >= 0) & (steps <= NBACK)
    slopes = _alibi_slopes()[group]
    bias = -(slopes[:, None, None] * dil) * steps[None]
    return np.where(valid[None], bias, -np.inf)


def _band_attention(qkv_cls, group):
    _, dil = ATT_GROUPS[group]
    n_seq, _, length, _ = qkv_cls.shape
    nsub = ATT_BQ // NBACK
    blocks = length // ATT_BQ
    pairs = nsub * ATT_HPG
    prev = lambda i: jnp.maximum(i * nsub - 1, 0)
    cur = lambda col: pl.BlockSpec((1, 1, ATT_BQ, ATT_W), lambda n, r, i: (n, r, i, col))
    halo = lambda col: pl.BlockSpec((1, 1, NBACK, ATT_W), lambda n, r, i: (n, r, prev(i), col))
    out_sds = jax.ShapeDtypeStruct((n_seq, dil, length, ATT_W), F32)
    return pl.pallas_call(
        _band_body,
        grid=(n_seq, dil, blocks),
        in_specs=[cur(0), cur(1), halo(1), cur(2), halo(2),
                  _resident((ATT_HPG, NBACK, 2 * NBACK), lambda n, r, i: (0, 0, 0))],
        out_specs=[cur(0), pl.BlockSpec((1, 1, ATT_BQ, LANES), lambda n, r, i: (n, r, i, 0))],
        out_shape=[out_sds, jax.ShapeDtypeStruct((n_seq, dil, length, LANES), F32)],
        scratch_shapes=[pltpu.VMEM((pairs, NBACK, 2 * NBACK), F32),
                        pltpu.VMEM((pairs, NBACK, 2 * NBACK), BF16)],
        compiler_params=_params("arbitrary", "arbitrary", "arbitrary"),
        name="band_attention",
    )(qkv_cls, qkv_cls, qkv_cls, qkv_cls, qkv_cls, jnp.asarray(_band_bias(group), F32))


SA_PASSES = tuple((0, (t,)) for t in range(T_SAMPLE)) + ((1, tuple(range(T_SAMPLE))),
                                                        (2, tuple(range(T_SAMPLE))))
SA_MAX_LB = max(w for w, _ in ATT_GROUPS)


def _att_sample_tables():
    slopes = _alibi_slopes()
    sels, biases = [], []
    bias_new = np.full((N_GROUPS, T_SAMPLE * ATT_HPG, LANES), -np.inf)
    cs = np.zeros((len(SA_PASSES), SUBLANES, LANES))
    lane = np.arange(LANES)
    for g, (lb, dil) in enumerate(ATT_GROUPS):
        passes = [p for p in SA_PASSES if p[0] == g]
        sel = np.zeros((ATT_HPG, T_SAMPLE, len(passes), ATT_HPG, LANES))
        for pi, (_, t_set) in enumerate(passes):
            for t in t_set:
                owns = np.ones(LANES, bool) if dil == 1 else (lane % dil == t)
                for h in range(ATT_HPG):
                    sel[h, t, pi, h, owns] = 1.0
        sels.append(sel.reshape(ATT_HPG * T_SAMPLE, len(passes) * ATT_HPG * LANES))
        pos = np.arange(lb)
        bias = np.full((T_SAMPLE, ATT_HPG, lb), -np.inf)
        for t in range(T_SAMPLE):
            if dil == 1:
                j = lb + t - pos
                valid = j <= NBACK
            else:
                j = NBACK - pos // dil
                valid = (pos % dil) == t
            bias[t] = np.where(valid[None, :], -(slopes[g][:, None] * dil) * j[None, :], -np.inf)
            for h in range(ATT_HPG):
                for t2 in range(T_SAMPLE):
                    if (dil == 1 and t2 <= t) or t2 == t:
                        bias_new[g, t * ATT_HPG + h, t2] = -slopes[g][h] * dil * (t - t2)
        biases.append(bias)
    for pi, (g, t_set) in enumerate(SA_PASSES):
        dil = ATT_GROUPS[g][1]
        for t in t_set:
            cs[pi, t] = 1.0 if dil == 1 else (lane % dil == t)
    bd = (np.arange(ATT_W)[None, :] // ATT_DH == np.arange(ATT_HPG)[:, None]).astype(np.float64)
    return sels, biases, bias_new, cs, bd


SA_SEQ_PER_STEP = 2


def _att_scores_body(qd_ref, qrow_ref, kn_ref, c0_ref, c1_ref, c2_ref,
                     sel0_ref, sel1_ref, sel2_ref, b0_ref, b1_ref, b2_ref, bn_ref, bd_ref,
                     p0_ref, p1_ref, p2_ref, pn_ref, s_scr, e_scr):
    caches = (c0_ref, c1_ref, c2_ref)
    sels = (sel0_ref, sel1_ref, sel2_ref)
    biases = (b0_ref, b1_ref, b2_ref)
    p_refs = (p0_ref, p1_ref, p2_ref)
    bd = bd_ref[...]
    zpad = jnp.zeros((LANES - SUBLANES, ATT_W), F32)
    qsels, s_news = {}, {}
    for j in range(SA_SEQ_PER_STEP):
        for g in range(N_GROUPS):
            qsels[j, g] = jnp.dot(qd_ref[j, g].astype(BF16), sels[g][...],
                                  preferred_element_type=F32)
            qrows = qrow_ref[j, g]
            qbd = jnp.concatenate(
                [jnp.broadcast_to(qrows[t:t + 1], (ATT_HPG, ATT_W)) * bd
                 for t in range(T_SAMPLE)], 0)
            kn = jnp.concatenate([kn_ref[j, g], zpad], 0)
            s_news[j, g] = lax.dot_general(qbd.astype(BF16), kn.astype(BF16), NT_DIMS,
                                           preferred_element_type=F32) * ATT_SCALE + bn_ref[g]
    for j in range(SA_SEQ_PER_STEP):
        e_new, dens, lses = {}, {}, {}
        pid = 0
        for g, (lb, dil) in enumerate(ATT_GROUPS):
            chunks = lb // LANES
            passes = [p for p in SA_PASSES if p[0] == g]
            qsel, s_new = qsels[j, g], s_news[j, g]
            pass_of = {}
            for local, (_, t_set) in enumerate(passes):
                for h in range(ATT_HPG):
                    col = (local * ATT_HPG + h) * LANES
                    qs = qsel[:, col:col + LANES]
                    for c in range(chunks):
                        lanes = slice(c * LANES, (c + 1) * LANES)
                        prod = caches[g][j, 0, h, :, lanes] * qs
                        part = prod.reshape(ATT_DH // SUBLANES, SUBLANES, LANES).sum(0)
                        s_scr[pid, h:h + 1, lanes] = part.sum(0, keepdims=True)
                for t in t_set:
                    pass_of[t] = pid
                pid += 1
            for t in range(T_SAMPLE):
                sc = s_scr[pass_of[t], :, :lb] * ATT_SCALE + biases[g][t]
                sn = s_new[t * ATT_HPG:(t + 1) * ATT_HPG]
                m = jnp.maximum(jnp.max(sc, -1, keepdims=True), jnp.max(sn, -1, keepdims=True))
                e = jnp.exp(sc - m)
                en = jnp.exp(sn - m)
                den = jnp.sum(e, -1, keepdims=True) + jnp.sum(en, -1, keepdims=True)
                e_scr[g * T_SAMPLE + t, :, :lb] = e
                e_new[g, t], dens[g, t], lses[g, t] = en, den, m + jnp.log(den)
        wts = {}
        for t in range(T_SAMPLE):
            ls = [lses[g, t] for g in range(N_GROUPS)]
            m = jnp.maximum(jnp.maximum(ls[0], ls[1]), ls[2])
            ws = [jnp.exp(l - m) for l in ls]
            tot = ws[0] + ws[1] + ws[2]
            for g in range(N_GROUPS):
                wts[g, t] = ws[g] / (tot * dens[g, t])
        for g, (lb, dil) in enumerate(ATT_GROUPS):
            pn_ref[j, g] = jnp.concatenate(
                [e_new[g, t] * wts[g, t] for t in range(T_SAMPLE)], 0)
            for local, (_, t_set) in enumerate([p for p in SA_PASSES if p[0] == g]):
                pc = None
                for t in t_set:
                    term = e_scr[g * T_SAMPLE + t, :, :lb] * wts[g, t]
                    pc = term if pc is None else pc + term
                p_refs[g][j, local] = pc


def _att_values_body(p0_ref, p1_ref, p2_ref, pn_ref, vn_ref, c0_ref, c1_ref, c2_ref,
                     cs_ref, bd_ref, o_ref):
    caches = (c0_ref, c1_ref, c2_ref)
    p_refs = (p0_ref, p1_ref, p2_ref)
    bd = bd_ref[...]
    zpad = jnp.zeros((LANES - SUBLANES, ATT_W), F32)
    wfs = {}
    for j in range(SA_SEQ_PER_STEP):
        for pid, (g, _) in enumerate(SA_PASSES):
            local = pid - [p[0] for p in SA_PASSES].index(g)
            folded = []
            for h in range(ATT_HPG):
                acc = None
                for c in range(ATT_GROUPS[g][0] // LANES):
                    lanes = slice(c * LANES, (c + 1) * LANES)
                    term = caches[g][j, 0, h, :, lanes] * p_refs[g][j, local, h:h + 1, lanes]
                    acc = term if acc is None else acc + term
                folded.append(acc)
            wfs[j, pid] = jnp.concatenate(folded, 0).astype(BF16)
    for j in range(SA_SEQ_PER_STEP):
        o = jnp.zeros((SUBLANES, ATT_W), F32)
        o_new = jnp.zeros((T_SAMPLE * ATT_HPG, ATT_W), F32)
        for g in range(N_GROUPS):
            vn = jnp.concatenate([vn_ref[j, g], zpad], 0)
            o_new = o_new + jnp.dot(pn_ref[j, g].astype(BF16), vn.astype(BF16),
                                    preferred_element_type=F32)
        for pid in range(len(SA_PASSES)):
            o = o + lax.dot_general(cs_ref[pid].astype(BF16), wfs[j, pid], NT_DIMS,
                                    preferred_element_type=F32)
        extra = [jnp.sum(o_new[t * ATT_HPG:(t + 1) * ATT_HPG] * bd, 0, keepdims=True)
                 for t in range(T_SAMPLE)]
        o_ref[j] = o[:T_SAMPLE] + jnp.concatenate(extra, 0)


SA_STEPS = N_SAMPLE // SA_SEQ_PER_STEP


def _sa_rows(a):
    a = a.transpose(1, 2, 0, 3, 4).reshape(N_SAMPLE, N_GROUPS, T_SAMPLE, ATT_W)
    return jnp.pad(a, ((0, 0), (0, 0), (0, SUBLANES - T_SAMPLE), (0, 0)))


def _sa_cache_specs(caches, kv):
    views = [c.transpose(0, 2, 3, 4, 1) for c in caches]
    specs = [pl.BlockSpec((SA_SEQ_PER_STEP, 1, ATT_HPG, ATT_DH, lb), lambda i: (i, kv, 0, 0, 0))
             for lb, _ in ATT_GROUPS]
    return views, specs


def _sa_const_specs(tables):
    return [_resident(t.shape, lambda i, nd=t.ndim: (0,) * nd) for t in tables]


def _sa_prob_layout():
    shapes, specs = [], []
    for g, (lb, _) in enumerate(ATT_GROUPS):
        n_pass = len([p for p in SA_PASSES if p[0] == g])
        shapes.append(jax.ShapeDtypeStruct((N_SAMPLE, n_pass, ATT_HPG, lb), F32))
        specs.append(pl.BlockSpec((SA_SEQ_PER_STEP, n_pass, ATT_HPG, lb), lambda i: (i, 0, 0, 0)))
    shapes.append(jax.ShapeDtypeStruct((N_SAMPLE, N_GROUPS, T_SAMPLE * ATT_HPG, LANES), F32))
    specs.append(pl.BlockSpec((SA_SEQ_PER_STEP, N_GROUPS, T_SAMPLE * ATT_HPG, LANES),
                              lambda i: (i, 0, 0, 0)))
    return shapes, specs


def _att_scores_part(q, k_new, caches):
    sels, biases, bias_new, _, bd = _att_sample_tables()
    qd = q.transpose(1, 2, 4, 3, 0).reshape(N_SAMPLE, N_GROUPS, ATT_DH, ATT_HPG * T_SAMPLE)
    views, cache_specs = _sa_cache_specs(caches, 0)
    tables = ([jnp.asarray(s, BF16) for s in sels] + [jnp.asarray(b, F32) for b in biases]
              + [jnp.asarray(bias_new, F32), jnp.asarray(bd, F32)])
    row_spec = pl.BlockSpec((SA_SEQ_PER_STEP, N_GROUPS, SUBLANES, ATT_W), lambda i: (i, 0, 0, 0))
    out_shape, out_specs = _sa_prob_layout()
    return _Part(
        _att_scores_body,
        (qd, _sa_rows(q), _sa_rows(k_new), *views, *tables),
        [pl.BlockSpec((SA_SEQ_PER_STEP, N_GROUPS, ATT_DH, ATT_HPG * T_SAMPLE),
                      lambda i: (i, 0, 0, 0)), row_spec, row_spec]
        + cache_specs + _sa_const_specs(tables),
        out_shape, out_specs,
        [pltpu.VMEM((len(SA_PASSES), ATT_HPG, SA_MAX_LB), F32),
         pltpu.VMEM((N_GROUPS * T_SAMPLE, ATT_HPG, SA_MAX_LB), F32)])


def _att_values_part(probs, v_new, caches):
    _, _, _, cs, bd = _att_sample_tables()
    views, cache_specs = _sa_cache_specs(caches, 1)
    tables = [jnp.asarray(cs, F32), jnp.asarray(bd, F32)]
    row_spec = pl.BlockSpec((SA_SEQ_PER_STEP, N_GROUPS, SUBLANES, ATT_W), lambda i: (i, 0, 0, 0))
    _, prob_specs = _sa_prob_layout()
    return _Part(
        _att_values_body,
        (*probs, _sa_rows(v_new), *views, *tables),
        prob_specs + [row_spec] + cache_specs + _sa_const_specs(tables),
        [jax.ShapeDtypeStruct((N_SAMPLE, T_SAMPLE, ATT_W), F32)],
        [pl.BlockSpec((SA_SEQ_PER_STEP, T_SAMPLE, ATT_W), lambda i: (i, 0, 0))])


def kernel(x_prompt, x_sample, c_prompt, c_sample, state_ret, cache_kv_w128, cache_kv_w512,
           cache_kv_w2048, w_ada, b_ada, ln_g, ln_b, w_ffn_in, w_ffn_out, w_ret_in, ret_gn_g,
           ret_gn_b, w_ret_out, w_att_in, w_att_out):
    d = D_MODEL
    n_p, seq, _ = x_prompt.shape
    w_ffn_in_b = {(0, 0): w_ffn_in[0, 0].astype(BF16)}
    w_ffn_out_b = {(0, 0): w_ffn_out[0, 0].astype(BF16)}
    w_ret_in_b = w_ret_in.astype(BF16)
    later_ffn = [(0, 1), (1, 0), (1, 1)]
    cast_items = ([(w_ffn_in, lw) for lw in later_ffn] + [(w_ffn_out, lw) for lw in later_ffn]
                  + [(w_ret_out, ()), (w_att_in, ()), (w_att_out, ())])

    c_all = jnp.concatenate(
        [c_sample, c_prompt, jnp.zeros((ADA_ROWS - N_SAMPLE - n_p, d), F32)], axis=0)
    ada = _ada_table(c_all, w_ada, b_ada)

    prompt = _Trunk(n_p * seq, 512, False, seq // 512)
    sample = _Trunk(N_SAMPLE * T_SAMPLE, N_SAMPLE, True, 1)
    xp = x_prompt.reshape(n_p * seq, d)
    xs = x_sample.transpose(1, 0, 2).reshape(T_SAMPLE * N_SAMPLE, d)

    steps = RET_S_STEPS
    tile_s = n_p * seq // steps
    prompt_s = _Trunk(n_p * seq, tile_s, False, seq // tile_s)

    def ffn_part(x, trunk, layer, which):
        sub = 0 if which == 0 else 2
        return _ffn_part(x, ada, w_ffn_in_b[layer, which], w_ffn_out_b[layer, which],
                         ln_g[layer, sub], ln_b[layer, sub], trunk, layer, sub)

    def ffn(x, trunk, layer, which):
        if trunk is sample:
            sub = 0 if which == 0 else 2
            return _ffn_stream(x, ada, w_ffn_in_b[layer, which], w_ffn_out_b[layer, which],
                               ln_g[layer, sub], ln_b[layer, sub], layer, sub)
        return _run([ffn_part(x, trunk, layer, which)], trunk.tokens // trunk.tile, "ffn")[0][0]

    xs = ffn(xs, sample, 0, 0)
    qkvg_s = _inproj(xs, ada, w_ret_in_b, sample, 0, F32)

    (xp,), (gated_s, ret_s), cast = _run(
        [ffn_part(xp, prompt_s, 0, 0),
         _ret_sample_part(qkvg_s.reshape(T_SAMPLE, N_SAMPLE, -1), state_ret, ret_gn_g, ret_gn_b),
         _cast_part(cast_items, steps)],
        steps, "ffn_ret_sample")
    for k, lw in enumerate(later_ffn):
        w_ffn_in_b[lw], w_ffn_out_b[lw] = cast[k], cast[len(later_ffn) + k]
    w_ret_out_b, w_att_in_b, w_att_out_b = cast[2 * len(later_ffn):]

    qkvg_p = _inproj(xp, ada, w_ret_in_b, prompt, 0, BF16)
    gated_p, ret_p = _ret_prompt(qkvg_p, ret_gn_g, ret_gn_b, n_p, seq)
    xp = _outproj(gated_p, xp, ada, w_ret_out_b, ln_g[0, 1], ln_b[0, 1], prompt, 0)

    xs = _outproj(gated_s.reshape(T_SAMPLE * N_SAMPLE, -1), xs, ada, w_ret_out_b,
                  ln_g[0, 1], ln_b[0, 1], sample, 0)
    xs = ffn(xs, sample, 0, 1)
    xs = ffn(xs, sample, 1, 0)
    qkv_s = _inproj(xs, ada, w_att_in_b, sample, 1, F32)
    qkv_s = qkv_s.reshape(T_SAMPLE, N_SAMPLE, 3, N_GROUPS, ATT_HPG, ATT_DH)
    q_s, k_s, v_s = qkv_s[:, :, 0], qkv_s[:, :, 1], qkv_s[:, :, 2]
    caches = (cache_kv_w128, cache_kv_w512, cache_kv_w2048)

    probs, (xp,) = _run([_att_scores_part(q_s, k_s, caches), ffn_part(xp, prompt_s, 0, 1)],
                        steps, "att_scores_ffn")
    (xp,), (att_s,) = _run([ffn_part(xp, prompt_s, 1, 0), _att_values_part(probs, v_s, caches)],
                           steps, "ffn_att_values")

    keep_p = [min(w, seq) for w, _ in ATT_GROUPS]
    *qkv_cls, kv_tail = _att_inproj_prompt(xp, ada, w_att_in_b, prompt, 1, max(keep_p))
    band = [_band_attention(qkv_cls[g], g) for g in range(N_GROUPS)]
    xp = _att_outproj_prompt([o for o, _ in band], [l for _, l in band], xp, ada, w_att_out_b,
                             ln_g[1, 1], ln_b[1, 1], prompt, 1)
    kv_tail = kv_tail.reshape(n_p, max(keep_p), 2, N_GROUPS, ATT_HPG, ATT_DH)
    kv_p = [kv_tail[:, max(keep_p) - keep_p[g]:, :, g] for g in range(N_GROUPS)]

    att_s = att_s.transpose(1, 0, 2).reshape(T_SAMPLE * N_SAMPLE, ATT_W)
    xs = _outproj(att_s, xs, ada, w_att_out_b, ln_g[1, 1], ln_b[1, 1], sample, 1)
    kv_s = [jnp.stack([k_s[:, :, g], v_s[:, :, g]], axis=2).transpose(1, 0, 2, 3, 4)
            for g in range(N_GROUPS)]

    xp = ffn(xp, prompt, 1, 1)
    xs = ffn(xs, sample, 1, 1)

    y_prompt = xp.reshape(n_p, seq, d)
    y_sample = xs.reshape(T_SAMPLE, N_SAMPLE, d).transpose(1, 0, 2)
    return (y_prompt, y_sample, ret_p, ret_s, kv_p[0], kv_s[0], kv_p[1], kv_s[1], kv_p[2], kv_s[2])
```

```python
import functools

import numpy as np
import jax
import jax.numpy as jnp
from jax import lax
from jax.experimental import pallas as pl
from jax.experimental.pallas import tpu as pltpu

F32 = jnp.float32
BF16 = jnp.bfloat16

D_MODEL = 1024
DEPTH = 2
D_FF = 2816
RET_HEADS = 4
RET_DK = 256
RET_DV = 512
RET_CHUNK = 128
ATT_GROUPS = ((128, 1), (512, 4), (2048, 16))
N_GROUPS = 3
ATT_HPG = 8
ATT_DH = 64
ATT_W = ATT_HPG * ATT_DH
NBACK = 128
LSE_REP = 128 // ATT_HPG
ALPHA = (2 * DEPTH) ** 0.25
NORM_EPS = 1e-5
ATT_SCALE = ATT_DH ** -0.5

V7X_VMEM_LIMIT_BYTES = 60000 * 1024
LANES = 128
SUBLANES = 8

N_SAMPLE = 128
T_SAMPLE = 4
ADA_ROWS = 136
PROMPT_ROW_BLOCK = N_SAMPLE // SUBLANES

NT_DIMS = (((1,), (1,)), ((), ()))
TN_DIMS = (((0,), (0,)), ((), ()))


def _params(*semantics):
    return pltpu.CompilerParams(dimension_semantics=semantics,
                                vmem_limit_bytes=V7X_VMEM_LIMIT_BYTES)


def _resident(block, index_map):
    return pl.BlockSpec(block, index_map, pipeline_mode=pl.Buffered(1))


def _layer_norm(z, g, b):
    mu = jnp.mean(z, -1, keepdims=True)
    zc = z - mu
    var = jnp.mean(zc * zc, -1, keepdims=True)
    return zc * lax.rsqrt(var + NORM_EPS) * g + b


def _log_gamma():
    return np.log1p(-(2.0 ** (-5.0 - np.arange(RET_HEADS, dtype=np.float64))))


def _alibi_slopes():
    h = np.arange(1, N_GROUPS * ATT_HPG + 1, dtype=np.float64)
    return (2.0 ** (-8.0 * h / (N_GROUPS * ATT_HPG))).reshape(N_GROUPS, ATT_HPG)


def _ada_body(c_ref, w_ref, b_ref, o_ref):
    s = jax.nn.silu(c_ref[...]).astype(BF16)
    w = w_ref[0].astype(BF16)
    o_ref[0, 0] = jnp.dot(s, w, preferred_element_type=F32) + b_ref[0, 0]


def _ada_table(c_all, w_ada, b_ada):
    d = D_MODEL
    return pl.pallas_call(
        _ada_body,
        grid=(DEPTH, 9),
        in_specs=[
            _resident((ADA_ROWS, d), lambda i, j: (0, 0)),
            pl.BlockSpec((1, d, d), lambda i, j: (i, 0, j)),
            pl.BlockSpec((1, 1, 1, d), lambda i, j: (i, j, 0, 0)),
        ],
        out_specs=pl.BlockSpec((1, 1, ADA_ROWS, d), lambda i, j: (i, j, 0, 0)),
        out_shape=jax.ShapeDtypeStruct((DEPTH, 9, ADA_ROWS, d), F32),
        compiler_params=_params("arbitrary", "arbitrary"),
        name="ada_table",
    )(c_all, w_ada, b_ada.reshape(DEPTH, 9, 1, d))


class _Trunk:
    def __init__(self, tokens, tile, per_row, tiles_per_seq):
        self.tokens = tokens
        self.tile = tile
        self.per_row = per_row
        self.tiles_per_seq = tiles_per_seq

    def mod_spec(self, layer, sub):
        if self.per_row:
            return pl.BlockSpec((1, 3, self.tile, D_MODEL), lambda i: (layer, sub, 0, 0))
        return pl.BlockSpec((1, 3, SUBLANES, D_MODEL),
                            lambda i: (layer, sub, PROMPT_ROW_BLOCK, 0))

    def mod_rows(self, mod_ref, k):
        if self.per_row:
            return mod_ref[0, k]
        n = pl.program_id(0) // self.tiles_per_seq
        return mod_ref[0, k, pl.ds(n, 1), :]


def _ffn_body(x_ref, mod_ref, win_ref, wout_ref, g_ref, b_ref, o_ref, *, trunk):
    x = x_ref[...]
    shift = trunk.mod_rows(mod_ref, 0)
    scale = trunk.mod_rows(mod_ref, 1)
    gate = trunk.mod_rows(mod_ref, 2)
    u = (x * (1.0 + scale) + shift).astype(BF16)
    h = jnp.dot(u, win_ref[...], preferred_element_type=F32)
    a = h[:, :D_FF]
    b = h[:, D_FF:]
    act = (jax.nn.silu(a) * b).astype(BF16)
    y = jnp.dot(act, wout_ref[...], preferred_element_type=F32)
    z = ALPHA * x + (0.5 * (1.0 + gate)) * y
    o_ref[...] = _layer_norm(z, g_ref[...], b_ref[...])


class _Part:
    def __init__(self, body, args, in_specs, out_shape, out_specs, scratch=()):
        self.body, self.args, self.in_specs = body, list(args), list(in_specs)
        self.out_shape, self.out_specs, self.scratch = list(out_shape), list(out_specs), list(scratch)


def _run(parts, steps, name):
    n_in = [len(p.args) for p in parts]
    n_out = [len(p.out_shape) for p in parts]
    n_scr = [len(p.scratch) for p in parts]

    def body(*refs):
        ins, outs, scr = refs[:sum(n_in)], refs[sum(n_in):sum(n_in) + sum(n_out)], refs[sum(n_in) + sum(n_out):]
        a = b = c = 0
        for k, p in enumerate(parts):
            p.body(*ins[a:a + n_in[k]], *outs[b:b + n_out[k]], *scr[c:c + n_scr[k]])
            a, b, c = a + n_in[k], b + n_out[k], c + n_scr[k]

    res = pl.pallas_call(
        body,
        grid=(steps,),
        in_specs=[s for p in parts for s in p.in_specs],
        out_specs=[s for p in parts for s in p.out_specs],
        out_shape=[s for p in parts for s in p.out_shape],
        scratch_shapes=[s for p in parts for s in p.scratch],
        compiler_params=_params("arbitrary"),
        name=name,
    )(*[a for p in parts for a in p.args])
    out, b = [], 0
    for k in range(len(parts)):
        out.append(res[b:b + n_out[k]])
        b += n_out[k]
    return out


def _cast_body(*refs):
    n = len(refs) // 2
    for src, dst in zip(refs[:n], refs[n:]):
        dst[...] = src[...].reshape(dst.shape).astype(dst.dtype)


def _cast_part(items, steps):
    args, in_specs, out_shape, out_specs = [], [], [], []
    for arr, lead in items:
        rows, cols = arr.shape[-2:]
        n_blocks = steps
        while rows % n_blocks or (rows // n_blocks) % (2 * SUBLANES):
            n_blocks //= 2
        blk = rows // n_blocks
        args.append(arr)
        in_specs.append(pl.BlockSpec(
            (1,) * len(lead) + (blk, cols),
            lambda i, lead=lead, last=n_blocks - 1: lead + (jnp.minimum(i, last), 0)))
        out_shape.append(jax.ShapeDtypeStruct((rows, cols), BF16))
        out_specs.append(pl.BlockSpec((blk, cols),
                                      lambda i, last=n_blocks - 1: (jnp.minimum(i, last), 0)))
    return _Part(_cast_body, args, in_specs, out_shape, out_specs)


def _ffn_part(x, ada, w_in, w_out, ln_g, ln_b, trunk, layer, sub):
    d = D_MODEL
    tm = trunk.tile
    return _Part(
        functools.partial(_ffn_body, trunk=trunk),
        (x, ada, w_in, w_out, ln_g.reshape(1, d), ln_b.reshape(1, d)),
        [
            pl.BlockSpec((tm, d), lambda i: (i, 0)),
            trunk.mod_spec(layer, sub),
            _resident((d, 2 * D_FF), lambda i: (0, 0)),
            _resident((D_FF, d), lambda i: (0, 0)),
            _resident((1, d), lambda i: (0, 0)),
            _resident((1, d), lambda i: (0, 0)),
        ],
        [jax.ShapeDtypeStruct((trunk.tokens, d), F32)],
        [pl.BlockSpec((tm, d), lambda i: (i, 0))])


def _inproj_body(x_ref, mod_ref, w_ref, o_ref, *, trunk):
    x = x_ref[...]
    u = (x * (1.0 + trunk.mod_rows(mod_ref, 1)) + trunk.mod_rows(mod_ref, 0)).astype(BF16)
    o_ref[...] = jnp.dot(u, w_ref[...], preferred_element_type=F32).astype(o_ref.dtype)


def _inproj_part(x, ada, w, trunk, layer, out_dtype):
    d = D_MODEL
    tm = trunk.tile
    n_out = w.shape[1]
    return _Part(
        functools.partial(_inproj_body, trunk=trunk),
        (x, ada, w),
        [
            pl.BlockSpec((tm, d), lambda i: (i, 0)),
            trunk.mod_spec(layer, 1),
            _resident((d, n_out), lambda i: (0, 0)),
        ],
        [jax.ShapeDtypeStruct((trunk.tokens, n_out), out_dtype)],
        [pl.BlockSpec((tm, n_out), lambda i: (i, 0))])


def _inproj(x, ada, w, trunk, layer, out_dtype):
    part = _inproj_part(x, ada, w, trunk, layer, out_dtype)
    return _run([part], trunk.tokens // trunk.tile, "inproj")[0][0]


def _att_inproj_body(x_ref, mod_ref, w_ref, a0_ref, a1_ref, a2_ref, tail_ref, y_scr, *, trunk):
    x = x_ref[...]
    u = (x * (1.0 + trunk.mod_rows(mod_ref, 1)) + trunk.mod_rows(mod_ref, 0)).astype(BF16)
    slabs = ATT_W // LANES
    a_refs = (a0_ref, a1_ref, a2_ref)
    for part in range(3):
        for g in range(N_GROUPS):
            col0 = (part * N_GROUPS + g) * ATT_W
            for c in range(ATT_W // ATT_SLAB):
                lo = col0 + c * ATT_SLAB
                y = jnp.dot(u, w_ref[:, lo:lo + ATT_SLAB], preferred_element_type=F32)
                if part > 0:
                    tail_ref[0, :, lo - N_GROUPS * ATT_W:lo - N_GROUPS * ATT_W + ATT_SLAB] = y
                for half in range(ATT_SLAB // LANES):
                    y_scr[lo // LANES + half] = y[:, half * LANES:(half + 1) * LANES]
            dil = ATT_GROUPS[g][1]
            rows = trunk.tile // dil
            for r in range(dil):
                sel = pl.ds(r, rows, stride=dil) if dil > 1 else slice(None)
                for cb in range(slabs):
                    dst = (part * slabs + cb) * LANES
                    a_refs[g][0, r, :, dst:dst + LANES] = (
                        y_scr[col0 // LANES + cb, sel, :].astype(BF16))


def _att_inproj_prompt(x, ada, w, trunk, layer, tail_rows):
    d = D_MODEL
    tm = trunk.tile
    n_out = w.shape[1]
    tps = trunk.tiles_per_seq
    n_seq = trunk.tokens // (tps * tm)
    first = tps - tail_rows // tm
    out_shape, out_specs = [], []
    for _, dil in ATT_GROUPS:
        out_shape.append(jax.ShapeDtypeStruct((n_seq, dil, tps * tm // dil, 3 * ATT_W), BF16))
        out_specs.append(pl.BlockSpec((1, dil, tm // dil, 3 * ATT_W),
                                      lambda i: (i // tps, 0, i % tps, 0)))
    kv_w = n_out - N_GROUPS * ATT_W
    out_shape.append(jax.ShapeDtypeStruct((n_seq, tail_rows, kv_w), F32))
    out_specs.append(pl.BlockSpec((1, tm, kv_w),
                                  lambda i: (i // tps, jnp.maximum(i % tps - first, 0), 0)))
    return pl.pallas_call(
        functools.partial(_att_inproj_body, trunk=trunk),
        grid=(trunk.tokens // tm,),
        in_specs=[
            pl.BlockSpec((tm, d), lambda i: (i, 0)),
            trunk.mod_spec(layer, 1),
            _resident((d, n_out), lambda i: (0, 0)),
        ],
        out_specs=out_specs,
        out_shape=out_shape,
        scratch_shapes=[pltpu.VMEM((n_out // LANES, tm, LANES), F32)],
        compiler_params=_params("arbitrary"),
        name="att_inproj",
    )(x, ada, w)


def _outproj_tail(a, x_ref, mod_ref, w_ref, g_ref, b_ref, o_ref, trunk):
    x = x_ref[...]
    gate = trunk.mod_rows(mod_ref, 2)
    y = jnp.dot(a.astype(BF16), w_ref[...], preferred_element_type=F32)
    z = ALPHA * x + (1.0 + gate) * y
    o_ref[...] = _layer_norm(z, g_ref[...], b_ref[...])


def _outproj_body(a_ref, x_ref, mod_ref, w_ref, g_ref, b_ref, o_ref, *, trunk):
    _outproj_tail(a_ref[...], x_ref, mod_ref, w_ref, g_ref, b_ref, o_ref, trunk)


def _outproj(a, x, ada, w, ln_g, ln_b, trunk, layer):
    d = D_MODEL
    tm = trunk.tile
    k_in = w.shape[0]
    return pl.pallas_call(
        functools.partial(_outproj_body, trunk=trunk),
        grid=(trunk.tokens // tm,),
        in_specs=[
            pl.BlockSpec((tm, k_in), lambda i: (i, 0)),
            pl.BlockSpec((tm, d), lambda i: (i, 0)),
            trunk.mod_spec(layer, 1),
            _resident((k_in, d), lambda i: (0, 0)),
            _resident((1, d), lambda i: (0, 0)),
            _resident((1, d), lambda i: (0, 0)),
        ],
        out_specs=pl.BlockSpec((tm, d), lambda i: (i, 0)),
        out_shape=jax.ShapeDtypeStruct((trunk.tokens, d), F32),
        compiler_params=_params("arbitrary"),
        name="outproj",
    )(a, x, ada, w, ln_g.reshape(1, d), ln_b.reshape(1, d))


def _att_outproj_body(o0_ref, o1_ref, o2_ref, l0_ref, l1_ref, l2_ref, e_ref, x_ref, mod_ref, w_ref,
                      g_ref, b_ref, out_ref, tok_scr, *, trunk):
    def token_major(ref, g, first_slab):
        dil = ATT_GROUPS[g][1]
        slabs = ref.shape[-1] // LANES
        if dil == 1:
            return ref[0, 0]
        rows = trunk.tile // dil
        for r in range(dil):
            for cb in range(slabs):
                tok_scr[first_slab + cb, pl.ds(r, rows, stride=dil), :] = (
                    ref[0, r, :, cb * LANES:(cb + 1) * LANES])
        return jnp.concatenate([tok_scr[first_slab + cb] for cb in range(slabs)], axis=1)

    o_slabs = ATT_W // LANES
    os = [token_major(r, g, g * o_slabs) for g, r in enumerate((o0_ref, o1_ref, o2_ref))]
    ls = [token_major(r, g, N_GROUPS * o_slabs + g)
          for g, r in enumerate((l0_ref, l1_ref, l2_ref))]
    m = jnp.maximum(jnp.maximum(ls[0], ls[1]), ls[2])
    ws = [jnp.exp(l - m) for l in ls]
    inv = 1.0 / (ws[0] + ws[1] + ws[2])
    a = None
    for g in range(N_GROUPS):
        alpha = ws[g] * inv
        hi = alpha.astype(BF16)
        lo = (alpha - hi.astype(F32)).astype(BF16)
        spread = (jnp.dot(hi, e_ref[...], preferred_element_type=F32)
                  + jnp.dot(lo, e_ref[...], preferred_element_type=F32))
        term = spread * os[g]
        a = term if a is None else a + term
    _outproj_tail(a, x_ref, mod_ref, w_ref, g_ref, b_ref, out_ref, trunk)


def _att_outproj_prompt(outs, lses, x, ada, w, ln_g, ln_b, trunk, layer):
    d = D_MODEL
    tm = trunk.tile
    tps = trunk.tiles_per_seq
    k_in = w.shape[0]
    cls_specs = lambda width: [
        pl.BlockSpec((1, dil, tm // dil, width), lambda i: (i // tps, 0, i % tps, 0))
        for _, dil in ATT_GROUPS]
    e = (np.arange(LANES)[:, None] // LSE_REP == np.arange(ATT_W)[None, :] // ATT_DH) / LSE_REP
    return pl.pallas_call(
        functools.partial(_att_outproj_body, trunk=trunk),
        grid=(trunk.tokens // tm,),
        in_specs=cls_specs(ATT_W) + cls_specs(LANES) + [
            _resident((LANES, ATT_W), lambda i: (0, 0)),
            pl.BlockSpec((tm, d), lambda i: (i, 0)),
            trunk.mod_spec(layer, 1),
            _resident((k_in, d), lambda i: (0, 0)),
            _resident((1, d), lambda i: (0, 0)),
            _resident((1, d), lambda i: (0, 0)),
        ],
        out_specs=pl.BlockSpec((tm, d), lambda i: (i, 0)),
        out_shape=jax.ShapeDtypeStruct((trunk.tokens, d), F32),
        scratch_shapes=[pltpu.VMEM((N_GROUPS * (ATT_W // LANES + 1), tm, LANES), F32)],
        compiler_params=_params("arbitrary"),
        name="att_outproj",
    )(*outs, *lses, jnp.asarray(e, BF16), x, ada, w, ln_g.reshape(1, d), ln_b.reshape(1, d))


def _group_norm_gate(o, g, gn_g, gn_b):
    mu = jnp.mean(o, -1, keepdims=True)
    oc = o - mu
    var = jnp.mean(oc * oc, -1, keepdims=True)
    on = oc * lax.rsqrt(var + NORM_EPS) * gn_g + gn_b
    return jax.nn.silu(g) * on


RET_TILE = 512


def _ret_decay_tables(chunk):
    lg = _log_gamma()
    pos = np.arange(chunk, dtype=np.float64)
    diff = pos[:, None] - pos[None, :]
    inner = np.where(diff >= 0, np.exp(np.maximum(diff, 0.0)[None] * lg[:, None, None]), 0.0)
    qd = np.exp((pos[None, :] + 1.0) * lg[:, None])
    kd = np.exp((chunk - 1.0 - pos[None, :]) * lg[:, None])
    cd = np.exp(chunk * lg)
    return inner, qd, kd, cd


def _ret_prompt_body(q_ref, k_ref, v_ref, g_ref, inner_ref, qd_ref, kd_ref, gng_ref, gnb_ref,
                     o_ref, sfin_ref, s_scr, *, cdec):
    t = pl.program_id(1)

    @pl.when(t == 0)
    def _():
        s_scr[...] = jnp.zeros_like(s_scr)

    for c in range(RET_TILE // RET_CHUNK):
        rows = slice(c * RET_CHUNK, (c + 1) * RET_CHUNK)
        for h in range(RET_HEADS):
            qk = slice(h * RET_DK, (h + 1) * RET_DK)
            vv = slice(h * RET_DV, (h + 1) * RET_DV)
            q = q_ref[rows, qk]
            k = k_ref[rows, qk] * (RET_DK ** -0.5)
            v = v_ref[rows, vv]
            scores = lax.dot_general(q, k, NT_DIMS, preferred_element_type=F32) * inner_ref[h]
            s_old = s_scr[h]
            o = (jnp.dot(scores.astype(BF16), v, preferred_element_type=F32)
                 + jnp.dot(q, s_old.astype(BF16), preferred_element_type=F32) * qd_ref[h])
            kdec = (k.astype(F32) * kd_ref[h]).astype(BF16)
            upd = lax.dot_general(kdec, v, TN_DIMS, preferred_element_type=F32)
            s_scr[h] = s_old * cdec[h] + upd
            gated = _group_norm_gate(o, g_ref[rows, vv].astype(F32), gng_ref[:, vv], gnb_ref[:, vv])
            o_ref[rows, vv] = gated.astype(o_ref.dtype)

    @pl.when(t == pl.num_programs(1) - 1)
    def _():
        sfin_ref[0] = s_scr[...]


def _ret_prompt(qkvg, gn_g, gn_b, n_seq, seq):
    inner, qd, kd, cd = _ret_decay_tables(RET_CHUNK)
    inner = jnp.asarray(inner, F32)
    qd_b = jnp.asarray(np.broadcast_to(qd[:, :, None], (RET_HEADS, RET_CHUNK, RET_DV)), F32)
    kd_b = jnp.asarray(np.broadcast_to(kd[:, :, None], (RET_HEADS, RET_CHUNK, RET_DK)), F32)
    cdec = tuple(float(np.float32(c)) for c in cd)
    tiles = seq // RET_TILE
    vw = RET_HEADS * RET_DV
    qw = RET_HEADS * RET_DK
    return pl.pallas_call(
        functools.partial(_ret_prompt_body, cdec=cdec),
        grid=(n_seq, tiles),
        in_specs=[
            pl.BlockSpec((RET_TILE, qw), lambda n, t: (n * tiles + t, 0)),
            pl.BlockSpec((RET_TILE, qw), lambda n, t: (n * tiles + t, 1)),
            pl.BlockSpec((RET_TILE, vw), lambda n, t: (n * tiles + t, 1)),
            pl.BlockSpec((RET_TILE, vw), lambda n, t: (n * tiles + t, 2)),
            _resident((RET_HEADS, RET_CHUNK, RET_CHUNK), lambda n, t: (0, 0, 0)),
            _resident((RET_HEADS, RET_CHUNK, RET_DV), lambda n, t: (0, 0, 0)),
            _resident((RET_HEADS, RET_CHUNK, RET_DK), lambda n, t: (0, 0, 0)),
            _resident((1, vw), lambda n, t: (0, 0)),
            _resident((1, vw), lambda n, t: (0, 0)),
        ],
        out_specs=[
            pl.BlockSpec((RET_TILE, vw), lambda n, t: (n * tiles + t, 0)),
            pl.BlockSpec((1, RET_HEADS, RET_DK, RET_DV), lambda n, t: (n, 0, 0, 0)),
        ],
        out_shape=[
            jax.ShapeDtypeStruct((n_seq * seq, vw), BF16),
            jax.ShapeDtypeStruct((n_seq, RET_HEADS, RET_DK, RET_DV), F32),
        ],
        scratch_shapes=[pltpu.VMEM((RET_HEADS, RET_DK, RET_DV), F32)],
        compiler_params=_params("arbitrary", "arbitrary"),
        name="retention_prompt",
    )(qkvg, qkvg, qkvg, qkvg, inner, qd_b, kd_b, gn_g.reshape(1, vw), gn_b.reshape(1, vw))


RET_SEQ_GROUP = SUBLANES
RET_S_ROWS = T_SAMPLE * RET_SEQ_GROUP


def _ret_sample_body(q_ref, k_ref, v_ref, g_ref, s_ref, w_ref, qd_ref, kd_ref, cd_ref,
                     gng_ref, gnb_ref, o_ref, so_ref):
    q = q_ref[...].reshape(RET_S_ROWS, RET_DK).astype(BF16)
    k = k_ref[...].reshape(RET_S_ROWS, RET_DK) * (RET_DK ** -0.5)
    v = v_ref[...].reshape(RET_S_ROWS, RET_DV).astype(BF16)
    g = g_ref[...].reshape(RET_S_ROWS, RET_DV)
    scores = lax.dot_general(q, k.astype(BF16), NT_DIMS, preferred_element_type=F32) * w_ref[0]
    o_intra = jnp.dot(scores.astype(BF16), v, preferred_element_type=F32)
    kdec = k * kd_ref[0]
    seq_of_row = lax.broadcasted_iota(jnp.int32, (RET_S_ROWS, 1), 0) % RET_SEQ_GROUP
    o_inter = jnp.zeros((RET_S_ROWS, RET_DV), F32)
    for j in range(RET_SEQ_GROUP):
        mine = seq_of_row == j
        s_old = s_ref[j, 0]
        r = jnp.dot(q, s_old.astype(BF16), preferred_element_type=F32)
        o_inter = jnp.where(mine, r, o_inter)
        kj = jnp.where(mine, kdec, 0.0).astype(BF16)
        upd = lax.dot_general(kj, v, TN_DIMS, preferred_element_type=F32)
        so_ref[j, 0] = s_old * cd_ref[0] + upd
    o = o_intra + o_inter * qd_ref[0]
    gated = _group_norm_gate(o, g, gng_ref[...], gnb_ref[...])
    o_ref[...] = gated.reshape(T_SAMPLE, RET_SEQ_GROUP, RET_DV).astype(o_ref.dtype)


RET_S_STEPS = (N_SAMPLE // RET_SEQ_GROUP) * RET_HEADS


def _ret_sample_part(qkvg, state, gn_g, gn_b):
    inner, qd, kd, cd = _ret_decay_tables(T_SAMPLE)
    row_t = np.arange(RET_S_ROWS) // RET_SEQ_GROUP
    row_j = np.arange(RET_S_ROWS) % RET_SEQ_GROUP
    same = (row_j[:, None] == row_j[None, :])
    w = np.where(same[None], inner[:, row_t[:, None], row_t[None, :]], 0.0)
    w = jnp.asarray(w, F32)
    qd_b = jnp.asarray(np.broadcast_to(qd[:, row_t, None], (RET_HEADS, RET_S_ROWS, RET_DV)), F32)
    kd_b = jnp.asarray(np.broadcast_to(kd[:, row_t, None], (RET_HEADS, RET_S_ROWS, RET_DK)), F32)
    cd_b = jnp.asarray(np.broadcast_to(cd[:, None, None], (RET_HEADS, 1, RET_DV)), F32)
    vw = RET_HEADS * RET_DV
    nq = RET_HEADS
    tg = (T_SAMPLE, RET_SEQ_GROUP)
    nh = RET_HEADS
    return _Part(
        _ret_sample_body,
        (qkvg, qkvg, qkvg, qkvg, state, w, qd_b, kd_b, cd_b,
         gn_g.reshape(1, vw), gn_b.reshape(1, vw)),
        [
            pl.BlockSpec(tg + (RET_DK,), lambda i: (0, i // nh, i % nh)),
            pl.BlockSpec(tg + (RET_DK,), lambda i: (0, i // nh, nq + i % nh)),
            pl.BlockSpec(tg + (RET_DV,), lambda i: (0, i // nh, nq + i % nh)),
            pl.BlockSpec(tg + (RET_DV,), lambda i: (0, i // nh, 2 * nq + i % nh)),
            pl.BlockSpec((RET_SEQ_GROUP, 1, RET_DK, RET_DV), lambda i: (i // nh, i % nh, 0, 0)),
            pl.BlockSpec((1, RET_S_ROWS, RET_S_ROWS), lambda i: (i % nh, 0, 0)),
            pl.BlockSpec((1, RET_S_ROWS, RET_DV), lambda i: (i % nh, 0, 0)),
            pl.BlockSpec((1, RET_S_ROWS, RET_DK), lambda i: (i % nh, 0, 0)),
            pl.BlockSpec((1, 1, RET_DV), lambda i: (i % nh, 0, 0)),
            pl.BlockSpec((1, RET_DV), lambda i: (0, i % nh)),
            pl.BlockSpec((1, RET_DV), lambda i: (0, i % nh)),
        ],
        [
            jax.ShapeDtypeStruct((T_SAMPLE, N_SAMPLE, vw), BF16),
            jax.ShapeDtypeStruct(state.shape, F32),
        ],
        [
            pl.BlockSpec(tg + (RET_DV,), lambda i: (0, i // nh, i % nh)),
            pl.BlockSpec((RET_SEQ_GROUP, 1, RET_DK, RET_DV), lambda i: (i // nh, i % nh, 0, 0)),
        ])


ATT_BQ = 512
ATT_SLAB = 256
HEADS_PER_SLAB = ATT_SLAB // ATT_DH


def _band_window(cur_ref, prev_ref, s):
    if s == 0:
        return jnp.concatenate([prev_ref[0, 0], cur_ref[0, 0, 0:NBACK, :]], axis=0)
    return cur_ref[0, 0, (s - 1) * NBACK:(s + 1) * NBACK, :]


def _band_body(q_ref, kc_ref, kp_ref, vc_ref, vp_ref, bias_ref, o_ref, lse_ref, s_scr, p_scr):
    i = pl.program_id(2)
    ki = lax.broadcasted_iota(jnp.int32, (NBACK, 2 * NBACK), 1)
    has_prev = ki >= jnp.where(i > 0, 0, NBACK)
    lane_head = lax.broadcasted_iota(jnp.int32, (1, ATT_SLAB), 1) // ATT_DH
    qmasks = [jnp.where(lane_head == hh, ATT_SCALE, 0.0).astype(BF16)
              for hh in range(HEADS_PER_SLAB)]
    nsub = ATT_BQ // NBACK
    pairs = nsub * ATT_HPG
    windows = {}

    def window(cur_ref, prev_ref, s):
        key = (id(cur_ref), s)
        if key not in windows:
            windows[key] = _band_window(cur_ref, prev_ref, s)
        return windows[key]

    def place(idx):
        s, h = divmod(idx, ATT_HPG)
        sl, hh = divmod(h, HEADS_PER_SLAB)
        return s, h, hh, slice(sl * ATT_SLAB, (sl + 1) * ATT_SLAB)

    def scores(idx):
        s, _, hh, lanes = place(idx)
        q = q_ref[0, 0, s * NBACK:(s + 1) * NBACK, lanes]
        s_scr[idx] = lax.dot_general(q * qmasks[hh], window(kc_ref, kp_ref, s)[:, lanes], NT_DIMS,
                                     preferred_element_type=F32)

    def softmax(idx):
        s, h, _, _ = place(idx)
        sc = s_scr[idx] + bias_ref[h]
        if s == 0:
            sc = jnp.where(has_prev, sc, -jnp.inf)
        m = jnp.max(sc, -1, keepdims=True)
        e = jnp.exp(sc - m)
        den = jnp.sum(e, -1, keepdims=True)
        p_scr[idx] = (e / den).astype(BF16)
        lse_ref[0, 0, s * NBACK:(s + 1) * NBACK, h * LSE_REP:(h + 1) * LSE_REP] = jnp.broadcast_to(
            m + jnp.log(den), (NBACK, LSE_REP))

    def values(idx):
        s, h, hh, lanes = place(idx)
        oh = jnp.dot(p_scr[idx], window(vc_ref, vp_ref, s)[:, lanes], preferred_element_type=F32)
        o_ref[0, 0, s * NBACK:(s + 1) * NBACK, h * ATT_DH:(h + 1) * ATT_DH] = (
            oh[:, hh * ATT_DH:(hh + 1) * ATT_DH])

    for stage in (scores, softmax, values):
        for idx in range(pairs):
            stage(idx)


def _band_bias(group):
    _, dil = ATT_GROUPS[group]
    steps = NBACK + np.arange(NBACK)[:, None] - np.arange(2 * NBACK)[None, :]
    valid = (steps >= 0) & (steps <= NBACK)
    slopes = _alibi_slopes()[group]
    bias = -(slopes[:, None, None] * dil) * steps[None]
    return np.where(valid[None], bias, -np.inf)


def _band_attention(qkv_cls, group):
    _, dil = ATT_GROUPS[group]
    n_seq, _, length, _ = qkv_cls.shape
    nsub = ATT_BQ // NBACK
    blocks = length // ATT_BQ
    pairs = nsub * ATT_HPG
    prev = lambda i: jnp.maximum(i * nsub - 1, 0)
    cur = lambda col: pl.BlockSpec((1, 1, ATT_BQ, ATT_W), lambda n, r, i: (n, r, i, col))
    halo = lambda col: pl.BlockSpec((1, 1, NBACK, ATT_W), lambda n, r, i: (n, r, prev(i), col))
    out_sds = jax.ShapeDtypeStruct((n_seq, dil, length, ATT_W), F32)
    return pl.pallas_call(
        _band_body,
        grid=(n_seq, dil, blocks),
        in_specs=[cur(0), cur(1), halo(1), cur(2), halo(2),
                  _resident((ATT_HPG, NBACK, 2 * NBACK), lambda n, r, i: (0, 0, 0))],
        out_specs=[cur(0), pl.BlockSpec((1, 1, ATT_BQ, LANES), lambda n, r, i: (n, r, i, 0))],
        out_shape=[out_sds, jax.ShapeDtypeStruct((n_seq, dil, length, LANES), F32)],
        scratch_shapes=[pltpu.VMEM((pairs, NBACK, 2 * NBACK), F32),
                        pltpu.VMEM((pairs, NBACK, 2 * NBACK), BF16)],
        compiler_params=_params("arbitrary", "arbitrary", "arbitrary"),
        name="band_attention",
    )(qkv_cls, qkv_cls, qkv_cls, qkv_cls, qkv_cls, jnp.asarray(_band_bias(group), F32))


SA_PASSES = tuple((0, (t,)) for t in range(T_SAMPLE)) + ((1, tuple(range(T_SAMPLE))),
                                                        (2, tuple(range(T_SAMPLE))))
SA_MAX_LB = max(w for w, _ in ATT_GROUPS)


def _att_sample_tables():
    slopes = _alibi_slopes()
    sels, biases = [], []
    bias_new = np.full((N_GROUPS, T_SAMPLE * ATT_HPG, LANES), -np.inf)
    cs = np.zeros((len(SA_PASSES), SUBLANES, LANES))
    lane = np.arange(LANES)
    for g, (lb, dil) in enumerate(ATT_GROUPS):
        passes = [p for p in SA_PASSES if p[0] == g]
        sel = np.zeros((ATT_HPG, T_SAMPLE, len(passes), ATT_HPG, LANES))
        for pi, (_, t_set) in enumerate(passes):
            for t in t_set:
                owns = np.ones(LANES, bool) if dil == 1 else (lane % dil == t)
                for h in range(ATT_HPG):
                    sel[h, t, pi, h, owns] = 1.0
        sels.append(sel.reshape(ATT_HPG * T_SAMPLE, len(passes) * ATT_HPG * LANES))
        pos = np.arange(lb)
        bias = np.full((T_SAMPLE, ATT_HPG, lb), -np.inf)
        for t in range(T_SAMPLE):
            if dil == 1:
                j = lb + t - pos
                valid = j <= NBACK
            else:
                j = NBACK - pos // dil
                valid = (pos % dil) == t
            bias[t] = np.where(valid[None, :], -(slopes[g][:, None] * dil) * j[None, :], -np.inf)
            for h in range(ATT_HPG):
                for t2 in range(T_SAMPLE):
                    if (dil == 1 and t2 <= t) or t2 == t:
                        bias_new[g, t * ATT_HPG + h, t2] = -slopes[g][h] * dil * (t - t2)
        biases.append(bias)
    for pi, (g, t_set) in enumerate(SA_PASSES):
        dil = ATT_GROUPS[g][1]
        for t in t_set:
            cs[pi, t] = 1.0 if dil == 1 else (lane % dil == t)
    bd = (np.arange(ATT_W)[None, :] // ATT_DH == np.arange(ATT_HPG)[:, None]).astype(np.float64)
    return sels, biases, bias_new, cs, bd


SA_SEQ_PER_STEP = 2


def _att_scores_body(qd_ref, qrow_ref, kn_ref, c0_ref, c1_ref, c2_ref,
                     sel0_ref, sel1_ref, sel2_ref, b0_ref, b1_ref, b2_ref, bn_ref, bd_ref,
                     p0_ref, p1_ref, p2_ref, pn_ref, s_scr, e_scr):
    caches = (c0_ref, c1_ref, c2_ref)
    sels = (sel0_ref, sel1_ref, sel2_ref)
    biases = (b0_ref, b1_ref, b2_ref)
    p_refs = (p0_ref, p1_ref, p2_ref)
    bd = bd_ref[...]
    zpad = jnp.zeros((LANES - SUBLANES, ATT_W), F32)
    qsels, s_news = {}, {}
    for j in range(SA_SEQ_PER_STEP):
        for g in range(N_GROUPS):
            qsels[j, g] = jnp.dot(qd_ref[j, g].astype(BF16), sels[g][...],
                                  preferred_element_type=F32)
            qrows = qrow_ref[j, g]
            qbd = jnp.concatenate(
                [jnp.broadcast_to(qrows[t:t + 1], (ATT_HPG, ATT_W)) * bd
                 for t in range(T_SAMPLE)], 0)
            kn = jnp.concatenate([kn_ref[j, g], zpad], 0)
            s_news[j, g] = lax.dot_general(qbd.astype(BF16), kn.astype(BF16), NT_DIMS,
                                           preferred_element_type=F32) * ATT_SCALE + bn_ref[g]
    for j in range(SA_SEQ_PER_STEP):
        e_new, dens, lses = {}, {}, {}
        pid = 0
        for g, (lb, dil) in enumerate(ATT_GROUPS):
            chunks = lb // LANES
            passes = [p for p in SA_PASSES if p[0] == g]
            qsel, s_new = qsels[j, g], s_news[j, g]
            pass_of = {}
            for local, (_, t_set) in enumerate(passes):
                for h in range(ATT_HPG):
                    col = (local * ATT_HPG + h) * LANES
                    qs = qsel[:, col:col + LANES]
                    for c in range(chunks):
                        lanes = slice(c * LANES, (c + 1) * LANES)
                        prod = caches[g][j, 0, h, :, lanes] * qs
                        part = prod.reshape(ATT_DH // SUBLANES, SUBLANES, LANES).sum(0)
                        s_scr[pid, h:h + 1, lanes] = part.sum(0, keepdims=True)
                for t in t_set:
                    pass_of[t] = pid
                pid += 1
            for t in range(T_SAMPLE):
                sc = s_scr[pass_of[t], :, :lb] * ATT_SCALE + biases[g][t]
                sn = s_new[t * ATT_HPG:(t + 1) * ATT_HPG]
                m = jnp.maximum(jnp.max(sc, -1, keepdims=True), jnp.max(sn, -1, keepdims=True))
                e = jnp.exp(sc - m)
                en = jnp.exp(sn - m)
                den = jnp.sum(e, -1, keepdims=True) + jnp.sum(en, -1, keepdims=True)
                e_scr[g * T_SAMPLE + t, :, :lb] = e
                e_new[g, t], dens[g, t], lses[g, t] = en, den, m + jnp.log(den)
        wts = {}
        for t in range(T_SAMPLE):
            ls = [lses[g, t] for g in range(N_GROUPS)]
            m = jnp.maximum(jnp.maximum(ls[0], ls[1]), ls[2])
            ws = [jnp.exp(l - m) for l in ls]
            tot = ws[0] + ws[1] + ws[2]
            for g in range(N_GROUPS):
                wts[g, t] = ws[g] / (tot * dens[g, t])
        for g, (lb, dil) in enumerate(ATT_GROUPS):
            pn_ref[j, g] = jnp.concatenate(
                [e_new[g, t] * wts[g, t] for t in range(T_SAMPLE)], 0)
            for local, (_, t_set) in enumerate([p for p in SA_PASSES if p[0] == g]):
                pc = None
                for t in t_set:
                    term = e_scr[g * T_SAMPLE + t, :, :lb] * wts[g, t]
                    pc = term if pc is None else pc + term
                p_refs[g][j, local] = pc


def _att_values_body(p0_ref, p1_ref, p2_ref, pn_ref, vn_ref, c0_ref, c1_ref, c2_ref,
                     cs_ref, bd_ref, o_ref):
    caches = (c0_ref, c1_ref, c2_ref)
    p_refs = (p0_ref, p1_ref, p2_ref)
    bd = bd_ref[...]
    zpad = jnp.zeros((LANES - SUBLANES, ATT_W), F32)
    wfs = {}
    for j in range(SA_SEQ_PER_STEP):
        for pid, (g, _) in enumerate(SA_PASSES):
            local = pid - [p[0] for p in SA_PASSES].index(g)
            folded = []
            for h in range(ATT_HPG):
                acc = None
                for c in range(ATT_GROUPS[g][0] // LANES):
                    lanes = slice(c * LANES, (c + 1) * LANES)
                    term = caches[g][j, 0, h, :, lanes] * p_refs[g][j, local, h:h + 1, lanes]
                    acc = term if acc is None else acc + term
                folded.append(acc)
            wfs[j, pid] = jnp.concatenate(folded, 0).astype(BF16)
    for j in range(SA_SEQ_PER_STEP):
        o = jnp.zeros((SUBLANES, ATT_W), F32)
        o_new = jnp.zeros((T_SAMPLE * ATT_HPG, ATT_W), F32)
        for g in range(N_GROUPS):
            vn = jnp.concatenate([vn_ref[j, g], zpad], 0)
            o_new = o_new + jnp.dot(pn_ref[j, g].astype(BF16), vn.astype(BF16),
                                    preferred_element_type=F32)
        for pid in range(len(SA_PASSES)):
            o = o + lax.dot_general(cs_ref[pid].astype(BF16), wfs[j, pid], NT_DIMS,
                                    preferred_element_type=F32)
        extra = [jnp.sum(o_new[t * ATT_HPG:(t + 1) * ATT_HPG] * bd, 0, keepdims=True)
                 for t in range(T_SAMPLE)]
        o_ref[j] = o[:T_SAMPLE] + jnp.concatenate(extra, 0)


SA_STEPS = N_SAMPLE // SA_SEQ_PER_STEP


def _sa_rows(a):
    a = a.transpose(1, 2, 0, 3, 4).reshape(N_SAMPLE, N_GROUPS, T_SAMPLE, ATT_W)
    return jnp.pad(a, ((0, 0), (0, 0), (0, SUBLANES - T_SAMPLE), (0, 0)))


def _sa_cache_specs(caches, kv):
    views = [c.transpose(0, 2, 3, 4, 1) for c in caches]
    specs = [pl.BlockSpec((SA_SEQ_PER_STEP, 1, ATT_HPG, ATT_DH, lb), lambda i: (i, kv, 0, 0, 0))
             for lb, _ in ATT_GROUPS]
    return views, specs


def _sa_const_specs(tables):
    return [_resident(t.shape, lambda i, nd=t.ndim: (0,) * nd) for t in tables]


def _sa_prob_layout():
    shapes, specs = [], []
    for g, (lb, _) in enumerate(ATT_GROUPS):
        n_pass = len([p for p in SA_PASSES if p[0] == g])
        shapes.append(jax.ShapeDtypeStruct((N_SAMPLE, n_pass, ATT_HPG, lb), F32))
        specs.append(pl.BlockSpec((SA_SEQ_PER_STEP, n_pass, ATT_HPG, lb), lambda i: (i, 0, 0, 0)))
    shapes.append(jax.ShapeDtypeStruct((N_SAMPLE, N_GROUPS, T_SAMPLE * ATT_HPG, LANES), F32))
    specs.append(pl.BlockSpec((SA_SEQ_PER_STEP, N_GROUPS, T_SAMPLE * ATT_HPG, LANES),
                              lambda i: (i, 0, 0, 0)))
    return shapes, specs


def _att_scores_part(q, k_new, caches):
    sels, biases, bias_new, _, bd = _att_sample_tables()
    qd = q.transpose(1, 2, 4, 3, 0).reshape(N_SAMPLE, N_GROUPS, ATT_DH, ATT_HPG * T_SAMPLE)
    views, cache_specs = _sa_cache_specs(caches, 0)
    tables = ([jnp.asarray(s, BF16) for s in sels] + [jnp.asarray(b, F32) for b in biases]
              + [jnp.asarray(bias_new, F32), jnp.asarray(bd, F32)])
    row_spec = pl.BlockSpec((SA_SEQ_PER_STEP, N_GROUPS, SUBLANES, ATT_W), lambda i: (i, 0, 0, 0))
    out_shape, out_specs = _sa_prob_layout()
    return _Part(
        _att_scores_body,
        (qd, _sa_rows(q), _sa_rows(k_new), *views, *tables),
        [pl.BlockSpec((SA_SEQ_PER_STEP, N_GROUPS, ATT_DH, ATT_HPG * T_SAMPLE),
                      lambda i: (i, 0, 0, 0)), row_spec, row_spec]
        + cache_specs + _sa_const_specs(tables),
        out_shape, out_specs,
        [pltpu.VMEM((len(SA_PASSES), ATT_HPG, SA_MAX_LB), F32),
         pltpu.VMEM((N_GROUPS * T_SAMPLE, ATT_HPG, SA_MAX_LB), F32)])


def _att_values_part(probs, v_new, caches):
    _, _, _, cs, bd = _att_sample_tables()
    views, cache_specs = _sa_cache_specs(caches, 1)
    tables = [jnp.asarray(cs, F32), jnp.asarray(bd, F32)]
    row_spec = pl.BlockSpec((SA_SEQ_PER_STEP, N_GROUPS, SUBLANES, ATT_W), lambda i: (i, 0, 0, 0))
    _, prob_specs = _sa_prob_layout()
    return _Part(
        _att_values_body,
        (*probs, _sa_rows(v_new), *views, *tables),
        prob_specs + [row_spec] + cache_specs + _sa_const_specs(tables),
        [jax.ShapeDtypeStruct((N_SAMPLE, T_SAMPLE, ATT_W), F32)],
        [pl.BlockSpec((SA_SEQ_PER_STEP, T_SAMPLE, ATT_W), lambda i: (i, 0, 0))])


def kernel(x_prompt, x_sample, c_prompt, c_sample, state_ret, cache_kv_w128, cache_kv_w512,
           cache_kv_w2048, w_ada, b_ada, ln_g, ln_b, w_ffn_in, w_ffn_out, w_ret_in, ret_gn_g,
           ret_gn_b, w_ret_out, w_att_in, w_att_out):
    d = D_MODEL
    n_p, seq, _ = x_prompt.shape
    w_ffn_in_b = {(0, 0): w_ffn_in[0, 0].astype(BF16)}
    w_ffn_out_b = {(0, 0): w_ffn_out[0, 0].astype(BF16)}
    w_ret_in_b = w_ret_in.astype(BF16)
    later_ffn = [(0, 1), (1, 0), (1, 1)]
    cast_items = ([(w_ffn_in, lw) for lw in later_ffn] + [(w_ffn_out, lw) for lw in later_ffn]
                  + [(w_ret_out, ()), (w_att_in, ()), (w_att_out, ())])

    c_all = jnp.concatenate(
        [c_sample, c_prompt, jnp.zeros((ADA_ROWS - N_SAMPLE - n_p, d), F32)], axis=0)
    ada = _ada_table(c_all, w_ada, b_ada)

    prompt = _Trunk(n_p * seq, 512, False, seq // 512)
    sample = _Trunk(N_SAMPLE * T_SAMPLE, N_SAMPLE, True, 1)
    xp = x_prompt.reshape(n_p * seq, d)
    xs = x_sample.transpose(1, 0, 2).reshape(T_SAMPLE * N_SAMPLE, d)

    steps = RET_S_STEPS
    tile_s = n_p * seq // steps
    prompt_s = _Trunk(n_p * seq, tile_s, False, seq // tile_s)

    def ffn_part(x, trunk, layer, which):
        sub = 0 if which == 0 else 2
        return _ffn_part(x, ada, w_ffn_in_b[layer, which], w_ffn_out_b[layer, which],
                         ln_g[layer, sub], ln_b[layer, sub], trunk, layer, sub)

    def ffn(x, trunk, layer, which):
        return _run([ffn_part(x, trunk, layer, which)], trunk.tokens // trunk.tile, "ffn")[0][0]

    xs = ffn(xs, sample, 0, 0)
    qkvg_s = _inproj(xs, ada, w_ret_in_b, sample, 0, F32)

    (xp,), (gated_s, ret_s), cast = _run(
        [ffn_part(xp, prompt_s, 0, 0),
         _ret_sample_part(qkvg_s.reshape(T_SAMPLE, N_SAMPLE, -1), state_ret, ret_gn_g, ret_gn_b),
         _cast_part(cast_items, steps)],
        steps, "ffn_ret_sample")
    for k, lw in enumerate(later_ffn):
        w_ffn_in_b[lw], w_ffn_out_b[lw] = cast[k], cast[len(later_ffn) + k]
    w_ret_out_b, w_att_in_b, w_att_out_b = cast[2 * len(later_ffn):]

    xs = _outproj(gated_s.reshape(T_SAMPLE * N_SAMPLE, -1), xs, ada, w_ret_out_b,
                  ln_g[0, 1], ln_b[0, 1], sample, 0)
    xs = ffn(xs, sample, 0, 1)
    xs = ffn(xs, sample, 1, 0)
    qkv_s = _inproj(xs, ada, w_att_in_b, sample, 1, F32)
    qkv_s = qkv_s.reshape(T_SAMPLE, N_SAMPLE, 3, N_GROUPS, ATT_HPG, ATT_DH)
    q_s, k_s, v_s = qkv_s[:, :, 0], qkv_s[:, :, 1], qkv_s[:, :, 2]
    caches = (cache_kv_w128, cache_kv_w512, cache_kv_w2048)

    probs, (qkvg_p,) = _run([_att_scores_part(q_s, k_s, caches),
                             _inproj_part(xp, ada, w_ret_in_b, prompt_s, 0, BF16)],
                            steps, "att_scores_inproj")
    gated_p, ret_p = _ret_prompt(qkvg_p, ret_gn_g, ret_gn_b, n_p, seq)
    xp = _outproj(gated_p, xp, ada, w_ret_out_b, ln_g[0, 1], ln_b[0, 1], prompt, 0)
    (xp,), (att_s,) = _run([ffn_part(xp, prompt_s, 0, 1), _att_values_part(probs, v_s, caches)],
                           steps, "ffn_att_values")
    xp = ffn(xp, prompt, 1, 0)

    keep_p = [min(w, seq) for w, _ in ATT_GROUPS]
    *qkv_cls, kv_tail = _att_inproj_prompt(xp, ada, w_att_in_b, prompt, 1, max(keep_p))
    band = [_band_attention(qkv_cls[g], g) for g in range(N_GROUPS)]
    xp = _att_outproj_prompt([o for o, _ in band], [l for _, l in band], xp, ada, w_att_out_b,
                             ln_g[1, 1], ln_b[1, 1], prompt, 1)
    kv_tail = kv_tail.reshape(n_p, max(keep_p), 2, N_GROUPS, ATT_HPG, ATT_DH)
    kv_p = [kv_tail[:, max(keep_p) - keep_p[g]:, :, g] for g in range(N_GROUPS)]

    att_s = att_s.transpose(1, 0, 2).reshape(T_SAMPLE * N_SAMPLE, ATT_W)
    xs = _outproj(att_s, xs, ada, w_att_out_b, ln_g[1, 1], ln_b[1, 1], sample, 1)
    kv_s = [jnp.stack([k_s[:, :, g], v_s[:, :, g]], axis=2).transpose(1, 0, 2, 3, 4)
            for g in range(N_GROUPS)]

    xp = ffn(xp, prompt, 1, 1)
    xs = ffn(xs, sample, 1, 1)

    y_prompt = xp.reshape(n_p, seq, d)
    y_sample = xs.reshape(T_SAMPLE, N_SAMPLE, d).transpose(1, 0, 2)
    return (y_prompt, y_sample, ret_p, ret_s, kv_p[0], kv_s[0], kv_p[1], kv_s[1], kv_p[2], kv_s[2])
```

```python
import functools

import numpy as np
import jax
import jax.numpy as jnp
from jax import lax
from jax.experimental import pallas as pl
from jax.experimental.pallas import tpu as pltpu

F32 = jnp.float32
BF16 = jnp.bfloat16

D_MODEL = 1024
DEPTH = 2
D_FF = 2816
RET_HEADS = 4
RET_DK = 256
RET_DV = 512
RET_CHUNK = 128
ATT_GROUPS = ((128, 1), (512, 4), (2048, 16))
N_GROUPS = 3
ATT_HPG = 8
ATT_DH = 64
ATT_W = ATT_HPG * ATT_DH
NBACK = 128
LSE_REP = 128 // ATT_HPG
ALPHA = (2 * DEPTH) ** 0.25
NORM_EPS = 1e-5
ATT_SCALE = ATT_DH ** -0.5

V7X_VMEM_LIMIT_BYTES = 60000 * 1024
LANES = 128
SUBLANES = 8

N_SAMPLE = 128
T_SAMPLE = 4
ADA_ROWS = 136
PROMPT_ROW_BLOCK = N_SAMPLE // SUBLANES

NT_DIMS = (((1,), (1,)), ((), ()))
TN_DIMS = (((0,), (0,)), ((), ()))


def _params(*semantics):
    return pltpu.CompilerParams(dimension_semantics=semantics,
                                vmem_limit_bytes=V7X_VMEM_LIMIT_BYTES)


def _resident(block, index_map):
    return pl.BlockSpec(block, index_map, pipeline_mode=pl.Buffered(1))


def _layer_norm(z, g, b):
    mu = jnp.mean(z, -1, keepdims=True)
    zc = z - mu
    var = jnp.mean(zc * zc, -1, keepdims=True)
    return zc * lax.rsqrt(var + NORM_EPS) * g + b


def _log_gamma():
    return np.log1p(-(2.0 ** (-5.0 - np.arange(RET_HEADS, dtype=np.float64))))


def _alibi_slopes():
    h = np.arange(1, N_GROUPS * ATT_HPG + 1, dtype=np.float64)
    return (2.0 ** (-8.0 * h / (N_GROUPS * ATT_HPG))).reshape(N_GROUPS, ATT_HPG)


def _ada_body(c_ref, w_ref, b_ref, o_ref):
    s = jax.nn.silu(c_ref[...]).astype(BF16)
    w = w_ref[0].astype(BF16)
    o_ref[0, 0] = jnp.dot(s, w, preferred_element_type=F32) + b_ref[0, 0]


def _ada_table(c_all, w_ada, b_ada):
    d = D_MODEL
    return pl.pallas_call(
        _ada_body,
        grid=(DEPTH, 9),
        in_specs=[
            _resident((ADA_ROWS, d), lambda i, j: (0, 0)),
            pl.BlockSpec((1, d, d), lambda i, j: (i, 0, j)),
            pl.BlockSpec((1, 1, 1, d), lambda i, j: (i, j, 0, 0)),
        ],
        out_specs=pl.BlockSpec((1, 1, ADA_ROWS, d), lambda i, j: (i, j, 0, 0)),
        out_shape=jax.ShapeDtypeStruct((DEPTH, 9, ADA_ROWS, d), F32),
        compiler_params=_params("arbitrary", "arbitrary"),
        name="ada_table",
    )(c_all, w_ada, b_ada.reshape(DEPTH, 9, 1, d))


class _Trunk:
    def __init__(self, tokens, tile, per_row, tiles_per_seq):
        self.tokens = tokens
        self.tile = tile
        self.per_row = per_row
        self.tiles_per_seq = tiles_per_seq

    def mod_spec(self, layer, sub):
        if self.per_row:
            return pl.BlockSpec((1, 3, self.tile, D_MODEL), lambda i: (layer, sub, 0, 0))
        return pl.BlockSpec((1, 3, SUBLANES, D_MODEL),
                            lambda i: (layer, sub, PROMPT_ROW_BLOCK, 0))

    def mod_rows(self, mod_ref, k):
        if self.per_row:
            return mod_ref[0, k]
        n = pl.program_id(0) // self.tiles_per_seq
        return mod_ref[0, k, pl.ds(n, 1), :]


def _ffn_body(x_ref, mod_ref, win_ref, wout_ref, g_ref, b_ref, o_ref, *, trunk):
    x = x_ref[...]
    shift = trunk.mod_rows(mod_ref, 0)
    scale = trunk.mod_rows(mod_ref, 1)
    gate = trunk.mod_rows(mod_ref, 2)
    u = (x * (1.0 + scale) + shift).astype(BF16)
    h = jnp.dot(u, win_ref[...], preferred_element_type=F32)
    a = h[:, :D_FF]
    b = h[:, D_FF:]
    act = (jax.nn.silu(a) * b).astype(BF16)
    y = jnp.dot(act, wout_ref[...], preferred_element_type=F32)
    z = ALPHA * x + (0.5 * (1.0 + gate)) * y
    o_ref[...] = _layer_norm(z, g_ref[...], b_ref[...])


class _Part:
    def __init__(self, body, args, in_specs, out_shape, out_specs, scratch=()):
        self.body, self.args, self.in_specs = body, list(args), list(in_specs)
        self.out_shape, self.out_specs, self.scratch = list(out_shape), list(out_specs), list(scratch)


def _run(parts, steps, name):
    n_in = [len(p.args) for p in parts]
    n_out = [len(p.out_shape) for p in parts]
    n_scr = [len(p.scratch) for p in parts]

    def body(*refs):
        ins, outs, scr = refs[:sum(n_in)], refs[sum(n_in):sum(n_in) + sum(n_out)], refs[sum(n_in) + sum(n_out):]
        a = b = c = 0
        for k, p in enumerate(parts):
            p.body(*ins[a:a + n_in[k]], *outs[b:b + n_out[k]], *scr[c:c + n_scr[k]])
            a, b, c = a + n_in[k], b + n_out[k], c + n_scr[k]

    res = pl.pallas_call(
        body,
        grid=(steps,),
        in_specs=[s for p in parts for s in p.in_specs],
        out_specs=[s for p in parts for s in p.out_specs],
        out_shape=[s for p in parts for s in p.out_shape],
        scratch_shapes=[s for p in parts for s in p.scratch],
        compiler_params=_params("arbitrary"),
        name=name,
    )(*[a for p in parts for a in p.args])
    out, b = [], 0
    for k in range(len(parts)):
        out.append(res[b:b + n_out[k]])
        b += n_out[k]
    return out


def _cast_body(*refs):
    n = len(refs) // 2
    for src, dst in zip(refs[:n], refs[n:]):
        dst[...] = src[...].reshape(dst.shape).astype(dst.dtype)


def _cast_part(items, steps):
    args, in_specs, out_shape, out_specs = [], [], [], []
    for arr, lead in items:
        rows, cols = arr.shape[-2:]
        n_blocks = steps
        while rows % n_blocks or (rows // n_blocks) % (2 * SUBLANES):
            n_blocks //= 2
        blk = rows // n_blocks
        args.append(arr)
        in_specs.append(pl.BlockSpec(
            (1,) * len(lead) + (blk, cols),
            lambda i, lead=lead, last=n_blocks - 1: lead + (jnp.minimum(i, last), 0)))
        out_shape.append(jax.ShapeDtypeStruct((rows, cols), BF16))
        out_specs.append(pl.BlockSpec((blk, cols),
                                      lambda i, last=n_blocks - 1: (jnp.minimum(i, last), 0)))
    return _Part(_cast_body, args, in_specs, out_shape, out_specs)


def _ffn_part(x, ada, w_in, w_out, ln_g, ln_b, trunk, layer, sub):
    d = D_MODEL
    tm = trunk.tile
    return _Part(
        functools.partial(_ffn_body, trunk=trunk),
        (x, ada, w_in, w_out, ln_g.reshape(1, d), ln_b.reshape(1, d)),
        [
            pl.BlockSpec((tm, d), lambda i: (i, 0)),
            trunk.mod_spec(layer, sub),
            _resident((d, 2 * D_FF), lambda i: (0, 0)),
            _resident((D_FF, d), lambda i: (0, 0)),
            _resident((1, d), lambda i: (0, 0)),
            _resident((1, d), lambda i: (0, 0)),
        ],
        [jax.ShapeDtypeStruct((trunk.tokens, d), F32)],
        [pl.BlockSpec((tm, d), lambda i: (i, 0))])


def _inproj_body(x_ref, mod_ref, w_ref, o_ref, *, trunk):
    x = x_ref[...]
    u = (x * (1.0 + trunk.mod_rows(mod_ref, 1)) + trunk.mod_rows(mod_ref, 0)).astype(BF16)
    o_ref[...] = jnp.dot(u, w_ref[...], preferred_element_type=F32).astype(o_ref.dtype)


def _inproj(x, ada, w, trunk, layer, out_dtype):
    d = D_MODEL
    tm = trunk.tile
    n_out = w.shape[1]
    return pl.pallas_call(
        functools.partial(_inproj_body, trunk=trunk),
        grid=(trunk.tokens // tm,),
        in_specs=[
            pl.BlockSpec((tm, d), lambda i: (i, 0)),
            trunk.mod_spec(layer, 1),
            _resident((d, n_out), lambda i: (0, 0)),
        ],
        out_specs=pl.BlockSpec((tm, n_out), lambda i: (i, 0)),
        out_shape=jax.ShapeDtypeStruct((trunk.tokens, n_out), out_dtype),
        compiler_params=_params("arbitrary"),
        name="inproj",
    )(x, ada, w)


def _att_inproj_body(x_ref, mod_ref, w_ref, a0_ref, a1_ref, a2_ref, tail_ref, y_scr, *, trunk):
    x = x_ref[...]
    u = (x * (1.0 + trunk.mod_rows(mod_ref, 1)) + trunk.mod_rows(mod_ref, 0)).astype(BF16)
    slabs = ATT_W // LANES
    a_refs = (a0_ref, a1_ref, a2_ref)
    for part in range(3):
        for g in range(N_GROUPS):
            col0 = (part * N_GROUPS + g) * ATT_W
            for c in range(ATT_W // ATT_SLAB):
                lo = col0 + c * ATT_SLAB
                y = jnp.dot(u, w_ref[:, lo:lo + ATT_SLAB], preferred_element_type=F32)
                if part > 0:
                    tail_ref[0, :, lo - N_GROUPS * ATT_W:lo - N_GROUPS * ATT_W + ATT_SLAB] = y
                for half in range(ATT_SLAB // LANES):
                    y_scr[lo // LANES + half] = y[:, half * LANES:(half + 1) * LANES]
            dil = ATT_GROUPS[g][1]
            rows = trunk.tile // dil
            for r in range(dil):
                sel = pl.ds(r, rows, stride=dil) if dil > 1 else slice(None)
                for cb in range(slabs):
                    dst = (part * slabs + cb) * LANES
                    a_refs[g][0, r, :, dst:dst + LANES] = (
                        y_scr[col0 // LANES + cb, sel, :].astype(BF16))


def _att_inproj_prompt(x, ada, w, trunk, layer, tail_rows):
    d = D_MODEL
    tm = trunk.tile
    n_out = w.shape[1]
    tps = trunk.tiles_per_seq
    n_seq = trunk.tokens // (tps * tm)
    first = tps - tail_rows // tm
    out_shape, out_specs = [], []
    for _, dil in ATT_GROUPS:
        out_shape.append(jax.ShapeDtypeStruct((n_seq, dil, tps * tm // dil, 3 * ATT_W), BF16))
        out_specs.append(pl.BlockSpec((1, dil, tm // dil, 3 * ATT_W),
                                      lambda i: (i // tps, 0, i % tps, 0)))
    kv_w = n_out - N_GROUPS * ATT_W
    out_shape.append(jax.ShapeDtypeStruct((n_seq, tail_rows, kv_w), F32))
    out_specs.append(pl.BlockSpec((1, tm, kv_w),
                                  lambda i: (i // tps, jnp.maximum(i % tps - first, 0), 0)))
    return pl.pallas_call(
        functools.partial(_att_inproj_body, trunk=trunk),
        grid=(trunk.tokens // tm,),
        in_specs=[
            pl.BlockSpec((tm, d), lambda i: (i, 0)),
            trunk.mod_spec(layer, 1),
            _resident((d, n_out), lambda i: (0, 0)),
        ],
        out_specs=out_specs,
        out_shape=out_shape,
        scratch_shapes=[pltpu.VMEM((n_out // LANES, tm, LANES), F32)],
        compiler_params=_params("arbitrary"),
        name="att_inproj",
    )(x, ada, w)


def _outproj_tail(a, x_ref, mod_ref, w_ref, g_ref, b_ref, o_ref, trunk):
    x = x_ref[...]
    gate = trunk.mod_rows(mod_ref, 2)
    y = jnp.dot(a.astype(BF16), w_ref[...], preferred_element_type=F32)
    z = ALPHA * x + (1.0 + gate) * y
    o_ref[...] = _layer_norm(z, g_ref[...], b_ref[...])


def _outproj_body(a_ref, x_ref, mod_ref, w_ref, g_ref, b_ref, o_ref, *, trunk):
    _outproj_tail(a_ref[...], x_ref, mod_ref, w_ref, g_ref, b_ref, o_ref, trunk)


def _outproj(a, x, ada, w, ln_g, ln_b, trunk, layer):
    d = D_MODEL
    tm = trunk.tile
    k_in = w.shape[0]
    return pl.pallas_call(
        functools.partial(_outproj_body, trunk=trunk),
        grid=(trunk.tokens // tm,),
        in_specs=[
            pl.BlockSpec((tm, k_in), lambda i: (i, 0)),
            pl.BlockSpec((tm, d), lambda i: (i, 0)),
            trunk.mod_spec(layer, 1),
            _resident((k_in, d), lambda i: (0, 0)),
            _resident((1, d), lambda i: (0, 0)),
            _resident((1, d), lambda i: (0, 0)),
        ],
        out_specs=pl.BlockSpec((tm, d), lambda i: (i, 0)),
        out_shape=jax.ShapeDtypeStruct((trunk.tokens, d), F32),
        compiler_params=_params("arbitrary"),
        name="outproj",
    )(a, x, ada, w, ln_g.reshape(1, d), ln_b.reshape(1, d))


def _att_outproj_body(o0_ref, o1_ref, o2_ref, l0_ref, l1_ref, l2_ref, e_ref, x_ref, mod_ref, w_ref,
                      g_ref, b_ref, out_ref, tok_scr, *, trunk):
    def token_major(ref, g, first_slab):
        dil = ATT_GROUPS[g][1]
        slabs = ref.shape[-1] // LANES
        if dil == 1:
            return ref[0, 0]
        rows = trunk.tile // dil
        for r in range(dil):
            for cb in range(slabs):
                tok_scr[first_slab + cb, pl.ds(r, rows, stride=dil), :] = (
                    ref[0, r, :, cb * LANES:(cb + 1) * LANES])
        return jnp.concatenate([tok_scr[first_slab + cb] for cb in range(slabs)], axis=1)

    o_slabs = ATT_W // LANES
    os = [token_major(r, g, g * o_slabs) for g, r in enumerate((o0_ref, o1_ref, o2_ref))]
    ls = [token_major(r, g, N_GROUPS * o_slabs + g)
          for g, r in enumerate((l0_ref, l1_ref, l2_ref))]
    m = jnp.maximum(jnp.maximum(ls[0], ls[1]), ls[2])
    ws = [jnp.exp(l - m) for l in ls]
    inv = 1.0 / (ws[0] + ws[1] + ws[2])
    a = None
    for g in range(N_GROUPS):
        alpha = ws[g] * inv
        hi = alpha.astype(BF16)
        lo = (alpha - hi.astype(F32)).astype(BF16)
        spread = (jnp.dot(hi, e_ref[...], preferred_element_type=F32)
                  + jnp.dot(lo, e_ref[...], preferred_element_type=F32))
        term = spread * os[g]
        a = term if a is None else a + term
    _outproj_tail(a, x_ref, mod_ref, w_ref, g_ref, b_ref, out_ref, trunk)


def _att_outproj_prompt(outs, lses, x, ada, w, ln_g, ln_b, trunk, layer):
    d = D_MODEL
    tm = trunk.tile
    tps = trunk.tiles_per_seq
    k_in = w.shape[0]
    cls_specs = lambda width: [
        pl.BlockSpec((1, dil, tm // dil, width), lambda i: (i // tps, 0, i % tps, 0))
        for _, dil in ATT_GROUPS]
    e = (np.arange(LANES)[:, None] // LSE_REP == np.arange(ATT_W)[None, :] // ATT_DH) / LSE_REP
    return pl.pallas_call(
        functools.partial(_att_outproj_body, trunk=trunk),
        grid=(trunk.tokens // tm,),
        in_specs=cls_specs(ATT_W) + cls_specs(LANES) + [
            _resident((LANES, ATT_W), lambda i: (0, 0)),
            pl.BlockSpec((tm, d), lambda i: (i, 0)),
            trunk.mod_spec(layer, 1),
            _resident((k_in, d), lambda i: (0, 0)),
            _resident((1, d), lambda i: (0, 0)),
            _resident((1, d), lambda i: (0, 0)),
        ],
        out_specs=pl.BlockSpec((tm, d), lambda i: (i, 0)),
        out_shape=jax.ShapeDtypeStruct((trunk.tokens, d), F32),
        scratch_shapes=[pltpu.VMEM((N_GROUPS * (ATT_W // LANES + 1), tm, LANES), F32)],
        compiler_params=_params("arbitrary"),
        name="att_outproj",
    )(*outs, *lses, jnp.asarray(e, BF16), x, ada, w, ln_g.reshape(1, d), ln_b.reshape(1, d))


def _group_norm_gate(o, g, gn_g, gn_b):
    mu = jnp.mean(o, -1, keepdims=True)
    oc = o - mu
    var = jnp.mean(oc * oc, -1, keepdims=True)
    on = oc * lax.rsqrt(var + NORM_EPS) * gn_g + gn_b
    return jax.nn.silu(g) * on


RET_TILE = 512


def _ret_decay_tables(chunk):
    lg = _log_gamma()
    pos = np.arange(chunk, dtype=np.float64)
    diff = pos[:, None] - pos[None, :]
    inner = np.where(diff >= 0, np.exp(np.maximum(diff, 0.0)[None] * lg[:, None, None]), 0.0)
    qd = np.exp((pos[None, :] + 1.0) * lg[:, None])
    kd = np.exp((chunk - 1.0 - pos[None, :]) * lg[:, None])
    cd = np.exp(chunk * lg)
    return inner, qd, kd, cd


def _ret_prompt_body(q_ref, k_ref, v_ref, g_ref, inner_ref, qd_ref, kd_ref, gng_ref, gnb_ref,
                     o_ref, sfin_ref, s_scr, *, cdec):
    t = pl.program_id(1)

    @pl.when(t == 0)
    def _():
        s_scr[...] = jnp.zeros_like(s_scr)

    for c in range(RET_TILE // RET_CHUNK):
        rows = slice(c * RET_CHUNK, (c + 1) * RET_CHUNK)
        for h in range(RET_HEADS):
            qk = slice(h * RET_DK, (h + 1) * RET_DK)
            vv = slice(h * RET_DV, (h + 1) * RET_DV)
            q = q_ref[rows, qk]
            k = k_ref[rows, qk] * (RET_DK ** -0.5)
            v = v_ref[rows, vv]
            scores = lax.dot_general(q, k, NT_DIMS, preferred_element_type=F32) * inner_ref[h]
            s_old = s_scr[h]
            o = (jnp.dot(scores.astype(BF16), v, preferred_element_type=F32)
                 + jnp.dot(q, s_old.astype(BF16), preferred_element_type=F32) * qd_ref[h])
            kdec = (k.astype(F32) * kd_ref[h]).astype(BF16)
            upd = lax.dot_general(kdec, v, TN_DIMS, preferred_element_type=F32)
            s_scr[h] = s_old * cdec[h] + upd
            gated = _group_norm_gate(o, g_ref[rows, vv].astype(F32), gng_ref[:, vv], gnb_ref[:, vv])
            o_ref[rows, vv] = gated.astype(o_ref.dtype)

    @pl.when(t == pl.num_programs(1) - 1)
    def _():
        sfin_ref[0] = s_scr[...]


def _ret_prompt(qkvg, gn_g, gn_b, n_seq, seq):
    inner, qd, kd, cd = _ret_decay_tables(RET_CHUNK)
    inner = jnp.asarray(inner, F32)
    qd_b = jnp.asarray(np.broadcast_to(qd[:, :, None], (RET_HEADS, RET_CHUNK, RET_DV)), F32)
    kd_b = jnp.asarray(np.broadcast_to(kd[:, :, None], (RET_HEADS, RET_CHUNK, RET_DK)), F32)
    cdec = tuple(float(np.float32(c)) for c in cd)
    tiles = seq // RET_TILE
    vw = RET_HEADS * RET_DV
    qw = RET_HEADS * RET_DK
    return pl.pallas_call(
        functools.partial(_ret_prompt_body, cdec=cdec),
        grid=(n_seq, tiles),
        in_specs=[
            pl.BlockSpec((RET_TILE, qw), lambda n, t: (n * tiles + t, 0)),
            pl.BlockSpec((RET_TILE, qw), lambda n, t: (n * tiles + t, 1)),
            pl.BlockSpec((RET_TILE, vw), lambda n, t: (n * tiles + t, 1)),
            pl.BlockSpec((RET_TILE, vw), lambda n, t: (n * tiles + t, 2)),
            _resident((RET_HEADS, RET_CHUNK, RET_CHUNK), lambda n, t: (0, 0, 0)),
            _resident((RET_HEADS, RET_CHUNK, RET_DV), lambda n, t: (0, 0, 0)),
            _resident((RET_HEADS, RET_CHUNK, RET_DK), lambda n, t: (0, 0, 0)),
            _resident((1, vw), lambda n, t: (0, 0)),
            _resident((1, vw), lambda n, t: (0, 0)),
        ],
        out_specs=[
            pl.BlockSpec((RET_TILE, vw), lambda n, t: (n * tiles + t, 0)),
            pl.BlockSpec((1, RET_HEADS, RET_DK, RET_DV), lambda n, t: (n, 0, 0, 0)),
        ],
        out_shape=[
            jax.ShapeDtypeStruct((n_seq * seq, vw), BF16),
            jax.ShapeDtypeStruct((n_seq, RET_HEADS, RET_DK, RET_DV), F32),
        ],
        scratch_shapes=[pltpu.VMEM((RET_HEADS, RET_DK, RET_DV), F32)],
        compiler_params=_params("arbitrary", "arbitrary"),
        name="retention_prompt",
    )(qkvg, qkvg, qkvg, qkvg, inner, qd_b, kd_b, gn_g.reshape(1, vw), gn_b.reshape(1, vw))


RET_SEQ_GROUP = SUBLANES
RET_S_ROWS = T_SAMPLE * RET_SEQ_GROUP


def _ret_sample_body(q_ref, k_ref, v_ref, g_ref, s_ref, w_ref, qd_ref, kd_ref, cd_ref,
                     gng_ref, gnb_ref, o_ref, so_ref):
    q = q_ref[...].reshape(RET_S_ROWS, RET_DK).astype(BF16)
    k = k_ref[...].reshape(RET_S_ROWS, RET_DK) * (RET_DK ** -0.5)
    v = v_ref[...].reshape(RET_S_ROWS, RET_DV).astype(BF16)
    g = g_ref[...].reshape(RET_S_ROWS, RET_DV)
    scores = lax.dot_general(q, k.astype(BF16), NT_DIMS, preferred_element_type=F32) * w_ref[0]
    o_intra = jnp.dot(scores.astype(BF16), v, preferred_element_type=F32)
    kdec = k * kd_ref[0]
    seq_of_row = lax.broadcasted_iota(jnp.int32, (RET_S_ROWS, 1), 0) % RET_SEQ_GROUP
    o_inter = jnp.zeros((RET_S_ROWS, RET_DV), F32)
    for j in range(RET_SEQ_GROUP):
        mine = seq_of_row == j
        s_old = s_ref[j, 0]
        r = jnp.dot(q, s_old.astype(BF16), preferred_element_type=F32)
        o_inter = jnp.where(mine, r, o_inter)
        kj = jnp.where(mine, kdec, 0.0).astype(BF16)
        upd = lax.dot_general(kj, v, TN_DIMS, preferred_element_type=F32)
        so_ref[j, 0] = s_old * cd_ref[0] + upd
    o = o_intra + o_inter * qd_ref[0]
    gated = _group_norm_gate(o, g, gng_ref[...], gnb_ref[...])
    o_ref[...] = gated.reshape(T_SAMPLE, RET_SEQ_GROUP, RET_DV).astype(o_ref.dtype)


RET_S_STEPS = (N_SAMPLE // RET_SEQ_GROUP) * RET_HEADS


def _ret_sample_part(qkvg, state, gn_g, gn_b):
    inner, qd, kd, cd = _ret_decay_tables(T_SAMPLE)
    row_t = np.arange(RET_S_ROWS) // RET_SEQ_GROUP
    row_j = np.arange(RET_S_ROWS) % RET_SEQ_GROUP
    same = (row_j[:, None] == row_j[None, :])
    w = np.where(same[None], inner[:, row_t[:, None], row_t[None, :]], 0.0)
    w = jnp.asarray(w, F32)
    qd_b = jnp.asarray(np.broadcast_to(qd[:, row_t, None], (RET_HEADS, RET_S_ROWS, RET_DV)), F32)
    kd_b = jnp.asarray(np.broadcast_to(kd[:, row_t, None], (RET_HEADS, RET_S_ROWS, RET_DK)), F32)
    cd_b = jnp.asarray(np.broadcast_to(cd[:, None, None], (RET_HEADS, 1, RET_DV)), F32)
    vw = RET_HEADS * RET_DV
    nq = RET_HEADS
    tg = (T_SAMPLE, RET_SEQ_GROUP)
    nh = RET_HEADS
    return _Part(
        _ret_sample_body,
        (qkvg, qkvg, qkvg, qkvg, state, w, qd_b, kd_b, cd_b,
         gn_g.reshape(1, vw), gn_b.reshape(1, vw)),
        [
            pl.BlockSpec(tg + (RET_DK,), lambda i: (0, i // nh, i % nh)),
            pl.BlockSpec(tg + (RET_DK,), lambda i: (0, i // nh, nq + i % nh)),
            pl.BlockSpec(tg + (RET_DV,), lambda i: (0, i // nh, nq + i % nh)),
            pl.BlockSpec(tg + (RET_DV,), lambda i: (0, i // nh, 2 * nq + i % nh)),
            pl.BlockSpec((RET_SEQ_GROUP, 1, RET_DK, RET_DV), lambda i: (i // nh, i % nh, 0, 0)),
            pl.BlockSpec((1, RET_S_ROWS, RET_S_ROWS), lambda i: (i % nh, 0, 0)),
            pl.BlockSpec((1, RET_S_ROWS, RET_DV), lambda i: (i % nh, 0, 0)),
            pl.BlockSpec((1, RET_S_ROWS, RET_DK), lambda i: (i % nh, 0, 0)),
            pl.BlockSpec((1, 1, RET_DV), lambda i: (i % nh, 0, 0)),
            pl.BlockSpec((1, RET_DV), lambda i: (0, i % nh)),
            pl.BlockSpec((1, RET_DV), lambda i: (0, i % nh)),
        ],
        [
            jax.ShapeDtypeStruct((T_SAMPLE, N_SAMPLE, vw), BF16),
            jax.ShapeDtypeStruct(state.shape, F32),
        ],
        [
            pl.BlockSpec(tg + (RET_DV,), lambda i: (0, i // nh, i % nh)),
            pl.BlockSpec((RET_SEQ_GROUP, 1, RET_DK, RET_DV), lambda i: (i // nh, i % nh, 0, 0)),
        ])


ATT_BQ = 512
ATT_SLAB = 256
HEADS_PER_SLAB = ATT_SLAB // ATT_DH


def _band_window(cur_ref, prev_ref, s):
    if s == 0:
        return jnp.concatenate([prev_ref[0, 0], cur_ref[0, 0, 0:NBACK, :]], axis=0)
    return cur_ref[0, 0, (s - 1) * NBACK:(s + 1) * NBACK, :]


def _band_body(q_ref, kc_ref, kp_ref, vc_ref, vp_ref, bias_ref, o_ref, lse_ref, s_scr, p_scr):
    i = pl.program_id(2)
    ki = lax.broadcasted_iota(jnp.int32, (NBACK, 2 * NBACK), 1)
    has_prev = ki >= jnp.where(i > 0, 0, NBACK)
    lane_head = lax.broadcasted_iota(jnp.int32, (1, ATT_SLAB), 1) // ATT_DH
    qmasks = [jnp.where(lane_head == hh, ATT_SCALE, 0.0).astype(BF16)
              for hh in range(HEADS_PER_SLAB)]
    nsub = ATT_BQ // NBACK
    pairs = nsub * ATT_HPG
    windows = {}

    def window(cur_ref, prev_ref, s):
        key = (id(cur_ref), s)
        if key not in windows:
            windows[key] = _band_window(cur_ref, prev_ref, s)
        return windows[key]

    def place(idx):
        s, h = divmod(idx, ATT_HPG)
        sl, hh = divmod(h, HEADS_PER_SLAB)
        return s, h, hh, slice(sl * ATT_SLAB, (sl + 1) * ATT_SLAB)

    def scores(idx):
        s, _, hh, lanes = place(idx)
        q = q_ref[0, 0, s * NBACK:(s + 1) * NBACK, lanes]
        s_scr[idx] = lax.dot_general(q * qmasks[hh], window(kc_ref, kp_ref, s)[:, lanes], NT_DIMS,
                                     preferred_element_type=F32)

    def softmax(idx):
        s, h, _, _ = place(idx)
        sc = s_scr[idx] + bias_ref[h]
        if s == 0:
            sc = jnp.where(has_prev, sc, -jnp.inf)
        m = jnp.max(sc, -1, keepdims=True)
        e = jnp.exp(sc - m)
        den = jnp.sum(e, -1, keepdims=True)
        p_scr[idx] = (e / den).astype(BF16)
        lse_ref[0, 0, s * NBACK:(s + 1) * NBACK, h * LSE_REP:(h + 1) * LSE_REP] = jnp.broadcast_to(
            m + jnp.log(den), (NBACK, LSE_REP))

    def values(idx):
        s, h, hh, lanes = place(idx)
        oh = jnp.dot(p_scr[idx], window(vc_ref, vp_ref, s)[:, lanes], preferred_element_type=F32)
        o_ref[0, 0, s * NBACK:(s + 1) * NBACK, h * ATT_DH:(h + 1) * ATT_DH] = (
            oh[:, hh * ATT_DH:(hh + 1) * ATT_DH])

    for stage in (scores, softmax, values):
        for idx in range(pairs):
            stage(idx)


def _band_bias(group):
    _, dil = ATT_GROUPS[group]
    steps = NBACK + np.arange(NBACK)[:, None] - np.arange(2 * NBACK)[None, :]
    valid = (steps >= 0) & (steps <= NBACK)
    slopes = _alibi_slopes()[group]
    bias = -(slopes[:, None, None] * dil) * steps[None]
    return np.where(valid[None], bias, -np.inf)


def _band_attention(qkv_cls, group):
    _, dil = ATT_GROUPS[group]
    n_seq, _, length, _ = qkv_cls.shape
    nsub = ATT_BQ // NBACK
    blocks = length // ATT_BQ
    pairs = nsub * ATT_HPG
    prev = lambda i: jnp.maximum(i * nsub - 1, 0)
    cur = lambda col: pl.BlockSpec((1, 1, ATT_BQ, ATT_W), lambda n, r, i: (n, r, i, col))
    halo = lambda col: pl.BlockSpec((1, 1, NBACK, ATT_W), lambda n, r, i: (n, r, prev(i), col))
    out_sds = jax.ShapeDtypeStruct((n_seq, dil, length, ATT_W), F32)
    return pl.pallas_call(
        _band_body,
        grid=(n_seq, dil, blocks),
        in_specs=[cur(0), cur(1), halo(1), cur(2), halo(2),
                  _resident((ATT_HPG, NBACK, 2 * NBACK), lambda n, r, i: (0, 0, 0))],
        out_specs=[cur(0), pl.BlockSpec((1, 1, ATT_BQ, LANES), lambda n, r, i: (n, r, i, 0))],
        out_shape=[out_sds, jax.ShapeDtypeStruct((n_seq, dil, length, LANES), F32)],
        scratch_shapes=[pltpu.VMEM((pairs, NBACK, 2 * NBACK), F32),
                        pltpu.VMEM((pairs, NBACK, 2 * NBACK), BF16)],
        compiler_params=_params("arbitrary", "arbitrary", "arbitrary"),
        name="band_attention",
    )(qkv_cls, qkv_cls, qkv_cls, qkv_cls, qkv_cls, jnp.asarray(_band_bias(group), F32))


SA_PASSES = tuple((0, (t,)) for t in range(T_SAMPLE)) + ((1, tuple(range(T_SAMPLE))),
                                                        (2, tuple(range(T_SAMPLE))))
SA_MAX_LB = max(w for w, _ in ATT_GROUPS)


def _att_sample_tables():
    slopes = _alibi_slopes()
    sels, biases = [], []
    bias_new = np.full((N_GROUPS, T_SAMPLE * ATT_HPG, LANES), -np.inf)
    cs = np.zeros((len(SA_PASSES), SUBLANES, LANES))
    lane = np.arange(LANES)
    for g, (lb, dil) in enumerate(ATT_GROUPS):
        passes = [p for p in SA_PASSES if p[0] == g]
        sel = np.zeros((ATT_HPG, T_SAMPLE, len(passes), ATT_HPG, LANES))
        for pi, (_, t_set) in enumerate(passes):
            for t in t_set:
                owns = np.ones(LANES, bool) if dil == 1 else (lane % dil == t)
                for h in range(ATT_HPG):
                    sel[h, t, pi, h, owns] = 1.0
        sels.append(sel.reshape(ATT_HPG * T_SAMPLE, len(passes) * ATT_HPG * LANES))
        pos = np.arange(lb)
        bias = np.full((T_SAMPLE, ATT_HPG, lb), -np.inf)
        for t in range(T_SAMPLE):
            if dil == 1:
                j = lb + t - pos
                valid = j <= NBACK
            else:
                j = NBACK - pos // dil
                valid = (pos % dil) == t
            bias[t] = np.where(valid[None, :], -(slopes[g][:, None] * dil) * j[None, :], -np.inf)
            for h in range(ATT_HPG):
                for t2 in range(T_SAMPLE):
                    if (dil == 1 and t2 <= t) or t2 == t:
                        bias_new[g, t * ATT_HPG + h, t2] = -slopes[g][h] * dil * (t - t2)
        biases.append(bias)
    for pi, (g, t_set) in enumerate(SA_PASSES):
        dil = ATT_GROUPS[g][1]
        for t in t_set:
            cs[pi, t] = 1.0 if dil == 1 else (lane % dil == t)
    bd = (np.arange(ATT_W)[None, :] // ATT_DH == np.arange(ATT_HPG)[:, None]).astype(np.float64)
    return sels, biases, bias_new, cs, bd


SA_SEQ_PER_STEP = 2


def _att_scores_body(qd_ref, qrow_ref, kn_ref, c0_ref, c1_ref, c2_ref,
                     sel0_ref, sel1_ref, sel2_ref, b0_ref, b1_ref, b2_ref, bn_ref, bd_ref,
                     p0_ref, p1_ref, p2_ref, pn_ref, s_scr, e_scr):
    caches = (c0_ref, c1_ref, c2_ref)
    sels = (sel0_ref, sel1_ref, sel2_ref)
    biases = (b0_ref, b1_ref, b2_ref)
    p_refs = (p0_ref, p1_ref, p2_ref)
    bd = bd_ref[...]
    zpad = jnp.zeros((LANES - SUBLANES, ATT_W), F32)
    qsels, s_news = {}, {}
    for j in range(SA_SEQ_PER_STEP):
        for g in range(N_GROUPS):
            qsels[j, g] = jnp.dot(qd_ref[j, g].astype(BF16), sels[g][...],
                                  preferred_element_type=F32)
            qrows = qrow_ref[j, g]
            qbd = jnp.concatenate(
                [jnp.broadcast_to(qrows[t:t + 1], (ATT_HPG, ATT_W)) * bd
                 for t in range(T_SAMPLE)], 0)
            kn = jnp.concatenate([kn_ref[j, g], zpad], 0)
            s_news[j, g] = lax.dot_general(qbd.astype(BF16), kn.astype(BF16), NT_DIMS,
                                           preferred_element_type=F32) * ATT_SCALE + bn_ref[g]
    for j in range(SA_SEQ_PER_STEP):
        e_new, dens, lses = {}, {}, {}
        pid = 0
        for g, (lb, dil) in enumerate(ATT_GROUPS):
            chunks = lb // LANES
            passes = [p for p in SA_PASSES if p[0] == g]
            qsel, s_new = qsels[j, g], s_news[j, g]
            pass_of = {}
            for local, (_, t_set) in enumerate(passes):
                for h in range(ATT_HPG):
                    col = (local * ATT_HPG + h) * LANES
                    qs = qsel[:, col:col + LANES]
                    for c in range(chunks):
                        lanes = slice(c * LANES, (c + 1) * LANES)
                        prod = caches[g][j, 0, h, :, lanes] * qs
                        part = prod.reshape(ATT_DH // SUBLANES, SUBLANES, LANES).sum(0)
                        s_scr[pid, h:h + 1, lanes] = part.sum(0, keepdims=True)
                for t in t_set:
                    pass_of[t] = pid
                pid += 1
            for t in range(T_SAMPLE):
                sc = s_scr[pass_of[t], :, :lb] * ATT_SCALE + biases[g][t]
                sn = s_new[t * ATT_HPG:(t + 1) * ATT_HPG]
                m = jnp.maximum(jnp.max(sc, -1, keepdims=True), jnp.max(sn, -1, keepdims=True))
                e = jnp.exp(sc - m)
                en = jnp.exp(sn - m)
                den = jnp.sum(e, -1, keepdims=True) + jnp.sum(en, -1, keepdims=True)
                e_scr[g * T_SAMPLE + t, :, :lb] = e
                e_new[g, t], dens[g, t], lses[g, t] = en, den, m + jnp.log(den)
        wts = {}
        for t in range(T_SAMPLE):
            ls = [lses[g, t] for g in range(N_GROUPS)]
            m = jnp.maximum(jnp.maximum(ls[0], ls[1]), ls[2])
            ws = [jnp.exp(l - m) for l in ls]
            tot = ws[0] + ws[1] + ws[2]
            for g in range(N_GROUPS):
                wts[g, t] = ws[g] / (tot * dens[g, t])
        for g, (lb, dil) in enumerate(ATT_GROUPS):
            pn_ref[j, g] = jnp.concatenate(
                [e_new[g, t] * wts[g, t] for t in range(T_SAMPLE)], 0)
            for local, (_, t_set) in enumerate([p for p in SA_PASSES if p[0] == g]):
                pc = None
                for t in t_set:
                    term = e_scr[g * T_SAMPLE + t, :, :lb] * wts[g, t]
                    pc = term if pc is None else pc + term
                p_refs[g][j, local] = pc


def _att_values_body(p0_ref, p1_ref, p2_ref, pn_ref, vn_ref, c0_ref, c1_ref, c2_ref,
                     cs_ref, bd_ref, o_ref):
    caches = (c0_ref, c1_ref, c2_ref)
    p_refs = (p0_ref, p1_ref, p2_ref)
    bd = bd_ref[...]
    zpad = jnp.zeros((LANES - SUBLANES, ATT_W), F32)
    wfs = {}
    for j in range(SA_SEQ_PER_STEP):
        for pid, (g, _) in enumerate(SA_PASSES):
            local = pid - [p[0] for p in SA_PASSES].index(g)
            folded = []
            for h in range(ATT_HPG):
                acc = None
                for c in range(ATT_GROUPS[g][0] // LANES):
                    lanes = slice(c * LANES, (c + 1) * LANES)
                    term = caches[g][j, 0, h, :, lanes] * p_refs[g][j, local, h:h + 1, lanes]
                    acc = term if acc is None else acc + term
                folded.append(acc)
            wfs[j, pid] = jnp.concatenate(folded, 0).astype(BF16)
    for j in range(SA_SEQ_PER_STEP):
        o = jnp.zeros((SUBLANES, ATT_W), F32)
        o_new = jnp.zeros((T_SAMPLE * ATT_HPG, ATT_W), F32)
        for g in range(N_GROUPS):
            vn = jnp.concatenate([vn_ref[j, g], zpad], 0)
            o_new = o_new + jnp.dot(pn_ref[j, g].astype(BF16), vn.astype(BF16),
                                    preferred_element_type=F32)
        for pid in range(len(SA_PASSES)):
            o = o + lax.dot_general(cs_ref[pid].astype(BF16), wfs[j, pid], NT_DIMS,
                                    preferred_element_type=F32)
        extra = [jnp.sum(o_new[t * ATT_HPG:(t + 1) * ATT_HPG] * bd, 0, keepdims=True)
                 for t in range(T_SAMPLE)]
        o_ref[j] = o[:T_SAMPLE] + jnp.concatenate(extra, 0)


SA_STEPS = N_SAMPLE // SA_SEQ_PER_STEP


def _sa_rows(a):
    a = a.transpose(1, 2, 0, 3, 4).reshape(N_SAMPLE, N_GROUPS, T_SAMPLE, ATT_W)
    return jnp.pad(a, ((0, 0), (0, 0), (0, SUBLANES - T_SAMPLE), (0, 0)))


def _sa_cache_specs(caches, kv):
    views = [c.transpose(0, 2, 3, 4, 1) for c in caches]
    specs = [pl.BlockSpec((SA_SEQ_PER_STEP, 1, ATT_HPG, ATT_DH, lb), lambda i: (i, kv, 0, 0, 0))
             for lb, _ in ATT_GROUPS]
    return views, specs


def _sa_const_specs(tables):
    return [_resident(t.shape, lambda i, nd=t.ndim: (0,) * nd) for t in tables]


def _sa_prob_layout():
    shapes, specs = [], []
    for g, (lb, _) in enumerate(ATT_GROUPS):
        n_pass = len([p for p in SA_PASSES if p[0] == g])
        shapes.append(jax.ShapeDtypeStruct((N_SAMPLE, n_pass, ATT_HPG, lb), F32))
        specs.append(pl.BlockSpec((SA_SEQ_PER_STEP, n_pass, ATT_HPG, lb), lambda i: (i, 0, 0, 0)))
    shapes.append(jax.ShapeDtypeStruct((N_SAMPLE, N_GROUPS, T_SAMPLE * ATT_HPG, LANES), F32))
    specs.append(pl.BlockSpec((SA_SEQ_PER_STEP, N_GROUPS, T_SAMPLE * ATT_HPG, LANES),
                              lambda i: (i, 0, 0, 0)))
    return shapes, specs


def _att_scores_part(q, k_new, caches):
    sels, biases, bias_new, _, bd = _att_sample_tables()
    qd = q.transpose(1, 2, 4, 3, 0).reshape(N_SAMPLE, N_GROUPS, ATT_DH, ATT_HPG * T_SAMPLE)
    views, cache_specs = _sa_cache_specs(caches, 0)
    tables = ([jnp.asarray(s, BF16) for s in sels] + [jnp.asarray(b, F32) for b in biases]
              + [jnp.asarray(bias_new, F32), jnp.asarray(bd, F32)])
    row_spec = pl.BlockSpec((SA_SEQ_PER_STEP, N_GROUPS, SUBLANES, ATT_W), lambda i: (i, 0, 0, 0))
    out_shape, out_specs = _sa_prob_layout()
    return _Part(
        _att_scores_body,
        (qd, _sa_rows(q), _sa_rows(k_new), *views, *tables),
        [pl.BlockSpec((SA_SEQ_PER_STEP, N_GROUPS, ATT_DH, ATT_HPG * T_SAMPLE),
                      lambda i: (i, 0, 0, 0)), row_spec, row_spec]
        + cache_specs + _sa_const_specs(tables),
        out_shape, out_specs,
        [pltpu.VMEM((len(SA_PASSES), ATT_HPG, SA_MAX_LB), F32),
         pltpu.VMEM((N_GROUPS * T_SAMPLE, ATT_HPG, SA_MAX_LB), F32)])


def _att_values_part(probs, v_new, caches):
    _, _, _, cs, bd = _att_sample_tables()
    views, cache_specs = _sa_cache_specs(caches, 1)
    tables = [jnp.asarray(cs, F32), jnp.asarray(bd, F32)]
    row_spec = pl.BlockSpec((SA_SEQ_PER_STEP, N_GROUPS, SUBLANES, ATT_W), lambda i: (i, 0, 0, 0))
    _, prob_specs = _sa_prob_layout()
    return _Part(
        _att_values_body,
        (*probs, _sa_rows(v_new), *views, *tables),
        prob_specs + [row_spec] + cache_specs + _sa_const_specs(tables),
        [jax.ShapeDtypeStruct((N_SAMPLE, T_SAMPLE, ATT_W), F32)],
        [pl.BlockSpec((SA_SEQ_PER_STEP, T_SAMPLE, ATT_W), lambda i: (i, 0, 0))])


def kernel(x_prompt, x_sample, c_prompt, c_sample, state_ret, cache_kv_w128, cache_kv_w512,
           cache_kv_w2048, w_ada, b_ada, ln_g, ln_b, w_ffn_in, w_ffn_out, w_ret_in, ret_gn_g,
           ret_gn_b, w_ret_out, w_att_in, w_att_out):
    d = D_MODEL
    n_p, seq, _ = x_prompt.shape
    w_ffn_in_b = {(0, 0): w_ffn_in[0, 0].astype(BF16)}
    w_ffn_out_b = {(0, 0): w_ffn_out[0, 0].astype(BF16)}
    w_ret_in_b = w_ret_in.astype(BF16)
    later_ffn = [(0, 1), (1, 0), (1, 1)]
    cast_items = ([(w_ffn_in, lw) for lw in later_ffn] + [(w_ffn_out, lw) for lw in later_ffn]
                  + [(w_ret_out, ()), (w_att_in, ()), (w_att_out, ())])

    c_all = jnp.concatenate(
        [c_sample, c_prompt, jnp.zeros((ADA_ROWS - N_SAMPLE - n_p, d), F32)], axis=0)
    ada = _ada_table(c_all, w_ada, b_ada)

    prompt = _Trunk(n_p * seq, 512, False, seq // 512)
    prompt_wide = _Trunk(n_p * seq, 1024, False, seq // 1024)
    sample = _Trunk(N_SAMPLE * T_SAMPLE, N_SAMPLE, True, 1)
    xp = x_prompt.reshape(n_p * seq, d)
    xs = x_sample.transpose(1, 0, 2).reshape(T_SAMPLE * N_SAMPLE, d)

    steps = RET_S_STEPS
    tile_s = n_p * seq // steps
    prompt_s = _Trunk(n_p * seq, tile_s, False, seq // tile_s)

    def ffn_part(x, trunk, layer, which):
        sub = 0 if which == 0 else 2
        return _ffn_part(x, ada, w_ffn_in_b[layer, which], w_ffn_out_b[layer, which],
                         ln_g[layer, sub], ln_b[layer, sub], trunk, layer, sub)

    def ffn(x, trunk, layer, which):
        return _run([ffn_part(x, trunk, layer, which)], trunk.tokens // trunk.tile, "ffn")[0][0]

    xs = ffn(xs, sample, 0, 0)
    qkvg_s = _inproj(xs, ada, w_ret_in_b, sample, 0, F32)

    (xp,), (gated_s, ret_s), cast = _run(
        [ffn_part(xp, prompt_s, 0, 0),
         _ret_sample_part(qkvg_s.reshape(T_SAMPLE, N_SAMPLE, -1), state_ret, ret_gn_g, ret_gn_b),
         _cast_part(cast_items, steps)],
        steps, "ffn_ret_sample")
    for k, lw in enumerate(later_ffn):
        w_ffn_in_b[lw], w_ffn_out_b[lw] = cast[k], cast[len(later_ffn) + k]
    w_ret_out_b, w_att_in_b, w_att_out_b = cast[2 * len(later_ffn):]

    xs = _outproj(gated_s.reshape(T_SAMPLE * N_SAMPLE, -1), xs, ada, w_ret_out_b,
                  ln_g[0, 1], ln_b[0, 1], sample, 0)
    xs = ffn(xs, sample, 0, 1)
    xs = ffn(xs, sample, 1, 0)
    qkv_s = _inproj(xs, ada, w_att_in_b, sample, 1, F32)
    qkv_s = qkv_s.reshape(T_SAMPLE, N_SAMPLE, 3, N_GROUPS, ATT_HPG, ATT_DH)
    q_s, k_s, v_s = qkv_s[:, :, 0], qkv_s[:, :, 1], qkv_s[:, :, 2]
    caches = (cache_kv_w128, cache_kv_w512, cache_kv_w2048)

    qkvg_p = _inproj(xp, ada, w_ret_in_b, prompt, 0, BF16)
    gated_p, ret_p = _ret_prompt(qkvg_p, ret_gn_g, ret_gn_b, n_p, seq)
    xp = _outproj(gated_p, xp, ada, w_ret_out_b, ln_g[0, 1], ln_b[0, 1], prompt_wide, 0)

    probs, (xp,) = _run([_att_scores_part(q_s, k_s, caches), ffn_part(xp, prompt_s, 0, 1)],
                        steps, "att_scores_ffn")
    (xp,), (att_s,) = _run([ffn_part(xp, prompt_s, 1, 0), _att_values_part(probs, v_s, caches)],
                           steps, "ffn_att_values")

    keep_p = [min(w, seq) for w, _ in ATT_GROUPS]
    *qkv_cls, kv_tail = _att_inproj_prompt(xp, ada, w_att_in_b, prompt, 1, max(keep_p))
    band = [_band_attention(qkv_cls[g], g) for g in range(N_GROUPS)]
    xp = _att_outproj_prompt([o for o, _ in band], [l for _, l in band], xp, ada, w_att_out_b,
                             ln_g[1, 1], ln_b[1, 1], prompt_wide, 1)
    kv_tail = kv_tail.reshape(n_p, max(keep_p), 2, N_GROUPS, ATT_HPG, ATT_DH)
    kv_p = [kv_tail[:, max(keep_p) - keep_p[g]:, :, g] for g in range(N_GROUPS)]

    att_s = att_s.transpose(1, 0, 2).reshape(T_SAMPLE * N_SAMPLE, ATT_W)
    xs = _outproj(att_s, xs, ada, w_att_out_b, ln_g[1, 1], ln_b[1, 1], sample, 1)
    kv_s = [jnp.stack([k_s[:, :, g], v_s[:, :, g]], axis=2).transpose(1, 0, 2, 3, 4)
            for g in range(N_GROUPS)]

    xp = ffn(xp, prompt, 1, 1)
    xs = ffn(xs, sample, 1, 1)

    y_prompt = xp.reshape(n_p, seq, d)
    y_sample = xs.reshape(T_SAMPLE, N_SAMPLE, d).transpose(1, 0, 2)
    return (y_prompt, y_sample, ret_p, ret_s, kv_p[0], kv_s[0], kv_p[1], kv_s[1], kv_p[2], kv_s[2])
```

```python
import functools

import numpy as np
import jax
import jax.numpy as jnp
from jax import lax
from jax.experimental import pallas as pl
from jax.experimental.pallas import tpu as pltpu

F32 = jnp.float32
BF16 = jnp.bfloat16

D_MODEL = 1024
DEPTH = 2
D_FF = 2816
RET_HEADS = 4
RET_DK = 256
RET_DV = 512
RET_CHUNK = 128
ATT_GROUPS = ((128, 1), (512, 4), (2048, 16))
N_GROUPS = 3
ATT_HPG = 8
ATT_DH = 64
ATT_W = ATT_HPG * ATT_DH
NBACK = 128
LSE_REP = 128 // ATT_HPG
ALPHA = (2 * DEPTH) ** 0.25
NORM_EPS = 1e-5
ATT_SCALE = ATT_DH ** -0.5

V7X_VMEM_LIMIT_BYTES = 60000 * 1024
LANES = 128
SUBLANES = 8

N_SAMPLE = 128
T_SAMPLE = 4
ADA_ROWS = 136
PROMPT_ROW_BLOCK = N_SAMPLE // SUBLANES

NT_DIMS = (((1,), (1,)), ((), ()))
TN_DIMS = (((0,), (0,)), ((), ()))


def _params(*semantics):
    return pltpu.CompilerParams(dimension_semantics=semantics,
                                vmem_limit_bytes=V7X_VMEM_LIMIT_BYTES)


def _resident(block, index_map):
    return pl.BlockSpec(block, index_map, pipeline_mode=pl.Buffered(1))


def _layer_norm(z, g, b):
    mu = jnp.mean(z, -1, keepdims=True)
    zc = z - mu
    var = jnp.mean(zc * zc, -1, keepdims=True)
    return zc * lax.rsqrt(var + NORM_EPS) * g + b


def _log_gamma():
    return np.log1p(-(2.0 ** (-5.0 - np.arange(RET_HEADS, dtype=np.float64))))


def _alibi_slopes():
    h = np.arange(1, N_GROUPS * ATT_HPG + 1, dtype=np.float64)
    return (2.0 ** (-8.0 * h / (N_GROUPS * ATT_HPG))).reshape(N_GROUPS, ATT_HPG)


def _ada_body(c_ref, w_ref, b_ref, o_ref):
    s = jax.nn.silu(c_ref[...]).astype(BF16)
    w = w_ref[0].astype(BF16)
    o_ref[0, 0] = jnp.dot(s, w, preferred_element_type=F32) + b_ref[0, 0]


def _ada_table(c_all, w_ada, b_ada):
    d = D_MODEL
    return pl.pallas_call(
        _ada_body,
        grid=(DEPTH, 9),
        in_specs=[
            _resident((ADA_ROWS, d), lambda i, j: (0, 0)),
            pl.BlockSpec((1, d, d), lambda i, j: (i, 0, j)),
            pl.BlockSpec((1, 1, 1, d), lambda i, j: (i, j, 0, 0)),
        ],
        out_specs=pl.BlockSpec((1, 1, ADA_ROWS, d), lambda i, j: (i, j, 0, 0)),
        out_shape=jax.ShapeDtypeStruct((DEPTH, 9, ADA_ROWS, d), F32),
        compiler_params=_params("arbitrary", "arbitrary"),
        name="ada_table",
    )(c_all, w_ada, b_ada.reshape(DEPTH, 9, 1, d))


class _Trunk:
    def __init__(self, tokens, tile, per_row, tiles_per_seq):
        self.tokens = tokens
        self.tile = tile
        self.per_row = per_row
        self.tiles_per_seq = tiles_per_seq

    def mod_spec(self, layer, sub):
        if self.per_row:
            return pl.BlockSpec((1, 3, self.tile, D_MODEL), lambda i: (layer, sub, 0, 0))
        return pl.BlockSpec((1, 3, SUBLANES, D_MODEL),
                            lambda i: (layer, sub, PROMPT_ROW_BLOCK, 0))

    def mod_rows(self, mod_ref, k):
        if self.per_row:
            return mod_ref[0, k]
        n = pl.program_id(0) // self.tiles_per_seq
        return mod_ref[0, k, pl.ds(n, 1), :]


def _ffn_body(x_ref, mod_ref, win_ref, wout_ref, g_ref, b_ref, o_ref, *, trunk):
    x = x_ref[...]
    shift = trunk.mod_rows(mod_ref, 0)
    scale = trunk.mod_rows(mod_ref, 1)
    gate = trunk.mod_rows(mod_ref, 2)
    u = (x * (1.0 + scale) + shift).astype(BF16)
    h = jnp.dot(u, win_ref[...], preferred_element_type=F32)
    a = h[:, :D_FF]
    b = h[:, D_FF:]
    act = (jax.nn.silu(a) * b).astype(BF16)
    y = jnp.dot(act, wout_ref[...], preferred_element_type=F32)
    z = ALPHA * x + (0.5 * (1.0 + gate)) * y
    o_ref[...] = _layer_norm(z, g_ref[...], b_ref[...])


class _Part:
    def __init__(self, body, args, in_specs, out_shape, out_specs, scratch=()):
        self.body, self.args, self.in_specs = body, list(args), list(in_specs)
        self.out_shape, self.out_specs, self.scratch = list(out_shape), list(out_specs), list(scratch)


def _run(parts, steps, name):
    n_in = [len(p.args) for p in parts]
    n_out = [len(p.out_shape) for p in parts]
    n_scr = [len(p.scratch) for p in parts]

    def body(*refs):
        ins, outs, scr = refs[:sum(n_in)], refs[sum(n_in):sum(n_in) + sum(n_out)], refs[sum(n_in) + sum(n_out):]
        a = b = c = 0
        for k, p in enumerate(parts):
            p.body(*ins[a:a + n_in[k]], *outs[b:b + n_out[k]], *scr[c:c + n_scr[k]])
            a, b, c = a + n_in[k], b + n_out[k], c + n_scr[k]

    res = pl.pallas_call(
        body,
        grid=(steps,),
        in_specs=[s for p in parts for s in p.in_specs],
        out_specs=[s for p in parts for s in p.out_specs],
        out_shape=[s for p in parts for s in p.out_shape],
        scratch_shapes=[s for p in parts for s in p.scratch],
        compiler_params=_params("arbitrary"),
        name=name,
    )(*[a for p in parts for a in p.args])
    out, b = [], 0
    for k in range(len(parts)):
        out.append(res[b:b + n_out[k]])
        b += n_out[k]
    return out


def _cast_body(*refs):
    n = len(refs) // 2
    for src, dst in zip(refs[:n], refs[n:]):
        dst[...] = src[...].reshape(dst.shape).astype(dst.dtype)


def _cast_part(items, steps):
    args, in_specs, out_shape, out_specs = [], [], [], []
    for arr, lead in items:
        rows, cols = arr.shape[-2:]
        n_blocks = steps
        while rows % n_blocks or (rows // n_blocks) % (2 * SUBLANES):
            n_blocks //= 2
        blk = rows // n_blocks
        args.append(arr)
        in_specs.append(pl.BlockSpec(
            (1,) * len(lead) + (blk, cols),
            lambda i, lead=lead, last=n_blocks - 1: lead + (jnp.minimum(i, last), 0)))
        out_shape.append(jax.ShapeDtypeStruct((rows, cols), BF16))
        out_specs.append(pl.BlockSpec((blk, cols),
                                      lambda i, last=n_blocks - 1: (jnp.minimum(i, last), 0)))
    return _Part(_cast_body, args, in_specs, out_shape, out_specs)


def _ffn_part(x, ada, w_in, w_out, ln_g, ln_b, trunk, layer, sub):
    d = D_MODEL
    tm = trunk.tile
    return _Part(
        functools.partial(_ffn_body, trunk=trunk),
        (x, ada, w_in, w_out, ln_g.reshape(1, d), ln_b.reshape(1, d)),
        [
            pl.BlockSpec((tm, d), lambda i: (i, 0)),
            trunk.mod_spec(layer, sub),
            _resident((d, 2 * D_FF), lambda i: (0, 0)),
            _resident((D_FF, d), lambda i: (0, 0)),
            _resident((1, d), lambda i: (0, 0)),
            _resident((1, d), lambda i: (0, 0)),
        ],
        [jax.ShapeDtypeStruct((trunk.tokens, d), F32)],
        [pl.BlockSpec((tm, d), lambda i: (i, 0))])


def _inproj_body(x_ref, mod_ref, w_ref, o_ref, *, trunk):
    x = x_ref[...]
    u = (x * (1.0 + trunk.mod_rows(mod_ref, 1)) + trunk.mod_rows(mod_ref, 0)).astype(BF16)
    o_ref[...] = jnp.dot(u, w_ref[...], preferred_element_type=F32).astype(o_ref.dtype)


def _inproj(x, ada, w, trunk, layer, out_dtype, tiles_to_columns=False):
    d = D_MODEL
    tm = trunk.tile
    n_out = w.shape[1]
    tiles = trunk.tokens // tm
    if tiles_to_columns:
        out_spec = pl.BlockSpec((tm, n_out), lambda i: (0, i))
        out_shape = jax.ShapeDtypeStruct((tm, tiles * n_out), out_dtype)
    else:
        out_spec = pl.BlockSpec((tm, n_out), lambda i: (i, 0))
        out_shape = jax.ShapeDtypeStruct((trunk.tokens, n_out), out_dtype)
    return pl.pallas_call(
        functools.partial(_inproj_body, trunk=trunk),
        grid=(tiles,),
        in_specs=[
            pl.BlockSpec((tm, d), lambda i: (i, 0)),
            trunk.mod_spec(layer, 1),
            _resident((d, n_out), lambda i: (0, 0)),
        ],
        out_specs=out_spec,
        out_shape=out_shape,
        compiler_params=_params("arbitrary"),
        name="inproj",
    )(x, ada, w)


def _att_inproj_body(x_ref, mod_ref, w_ref, a0_ref, a1_ref, a2_ref, tail_ref, y_scr, *, trunk):
    x = x_ref[...]
    u = (x * (1.0 + trunk.mod_rows(mod_ref, 1)) + trunk.mod_rows(mod_ref, 0)).astype(BF16)
    slabs = ATT_W // LANES
    a_refs = (a0_ref, a1_ref, a2_ref)
    for part in range(3):
        for g in range(N_GROUPS):
            col0 = (part * N_GROUPS + g) * ATT_W
            for c in range(ATT_W // ATT_SLAB):
                lo = col0 + c * ATT_SLAB
                y = jnp.dot(u, w_ref[:, lo:lo + ATT_SLAB], preferred_element_type=F32)
                if part > 0:
                    tail_ref[0, :, lo - N_GROUPS * ATT_W:lo - N_GROUPS * ATT_W + ATT_SLAB] = y
                for half in range(ATT_SLAB // LANES):
                    y_scr[lo // LANES + half] = y[:, half * LANES:(half + 1) * LANES]
            dil = ATT_GROUPS[g][1]
            rows = trunk.tile // dil
            for r in range(dil):
                sel = pl.ds(r, rows, stride=dil) if dil > 1 else slice(None)
                for cb in range(slabs):
                    dst = (part * slabs + cb) * LANES
                    a_refs[g][0, r, :, dst:dst + LANES] = (
                        y_scr[col0 // LANES + cb, sel, :].astype(BF16))


def _att_inproj_prompt(x, ada, w, trunk, layer, tail_rows):
    d = D_MODEL
    tm = trunk.tile
    n_out = w.shape[1]
    tps = trunk.tiles_per_seq
    n_seq = trunk.tokens // (tps * tm)
    first = tps - tail_rows // tm
    out_shape, out_specs = [], []
    for _, dil in ATT_GROUPS:
        out_shape.append(jax.ShapeDtypeStruct((n_seq, dil, tps * tm // dil, 3 * ATT_W), BF16))
        out_specs.append(pl.BlockSpec((1, dil, tm // dil, 3 * ATT_W),
                                      lambda i: (i // tps, 0, i % tps, 0)))
    kv_w = n_out - N_GROUPS * ATT_W
    out_shape.append(jax.ShapeDtypeStruct((n_seq, tail_rows, kv_w), F32))
    out_specs.append(pl.BlockSpec((1, tm, kv_w),
                                  lambda i: (i // tps, jnp.maximum(i % tps - first, 0), 0)))
    return pl.pallas_call(
        functools.partial(_att_inproj_body, trunk=trunk),
        grid=(trunk.tokens // tm,),
        in_specs=[
            pl.BlockSpec((tm, d), lambda i: (i, 0)),
            trunk.mod_spec(layer, 1),
            _resident((d, n_out), lambda i: (0, 0)),
        ],
        out_specs=out_specs,
        out_shape=out_shape,
        scratch_shapes=[pltpu.VMEM((n_out // LANES, tm, LANES), F32)],
        compiler_params=_params("arbitrary"),
        name="att_inproj",
    )(x, ada, w)


def _outproj_tail(a, x_ref, mod_ref, w_ref, g_ref, b_ref, o_ref, trunk):
    x = x_ref[...]
    gate = trunk.mod_rows(mod_ref, 2)
    y = jnp.dot(a.astype(BF16), w_ref[...], preferred_element_type=F32)
    z = ALPHA * x + (1.0 + gate) * y
    o_ref[...] = _layer_norm(z, g_ref[...], b_ref[...])


def _outproj_body(a_ref, x_ref, mod_ref, w_ref, g_ref, b_ref, o_ref, *, trunk):
    _outproj_tail(a_ref[...], x_ref, mod_ref, w_ref, g_ref, b_ref, o_ref, trunk)


def _outproj(a, x, ada, w, ln_g, ln_b, trunk, layer):
    d = D_MODEL
    tm = trunk.tile
    k_in = w.shape[0]
    return pl.pallas_call(
        functools.partial(_outproj_body, trunk=trunk),
        grid=(trunk.tokens // tm,),
        in_specs=[
            pl.BlockSpec((tm, k_in), lambda i: (i, 0)),
            pl.BlockSpec((tm, d), lambda i: (i, 0)),
            trunk.mod_spec(layer, 1),
            _resident((k_in, d), lambda i: (0, 0)),
            _resident((1, d), lambda i: (0, 0)),
            _resident((1, d), lambda i: (0, 0)),
        ],
        out_specs=pl.BlockSpec((tm, d), lambda i: (i, 0)),
        out_shape=jax.ShapeDtypeStruct((trunk.tokens, d), F32),
        compiler_params=_params("arbitrary"),
        name="outproj",
    )(a, x, ada, w, ln_g.reshape(1, d), ln_b.reshape(1, d))


def _att_outproj_body(o0_ref, o1_ref, o2_ref, l0_ref, l1_ref, l2_ref, e_ref, x_ref, mod_ref, w_ref,
                      g_ref, b_ref, out_ref, tok_scr, *, trunk):
    def token_major(ref, g, first_slab):
        dil = ATT_GROUPS[g][1]
        slabs = ref.shape[-1] // LANES
        if dil == 1:
            return ref[0, 0]
        rows = trunk.tile // dil
        for r in range(dil):
            for cb in range(slabs):
                tok_scr[first_slab + cb, pl.ds(r, rows, stride=dil), :] = (
                    ref[0, r, :, cb * LANES:(cb + 1) * LANES])
        return jnp.concatenate([tok_scr[first_slab + cb] for cb in range(slabs)], axis=1)

    o_slabs = ATT_W // LANES
    os = [token_major(r, g, g * o_slabs) for g, r in enumerate((o0_ref, o1_ref, o2_ref))]
    ls = [token_major(r, g, N_GROUPS * o_slabs + g)
          for g, r in enumerate((l0_ref, l1_ref, l2_ref))]
    m = jnp.maximum(jnp.maximum(ls[0], ls[1]), ls[2])
    ws = [jnp.exp(l - m) for l in ls]
    inv = 1.0 / (ws[0] + ws[1] + ws[2])
    a = None
    for g in range(N_GROUPS):
        alpha = ws[g] * inv
        hi = alpha.astype(BF16)
        lo = (alpha - hi.astype(F32)).astype(BF16)
        spread = (jnp.dot(hi, e_ref[...], preferred_element_type=F32)
                  + jnp.dot(lo, e_ref[...], preferred_element_type=F32))
        term = spread * os[g]
        a = term if a is None else a + term
    _outproj_tail(a, x_ref, mod_ref, w_ref, g_ref, b_ref, out_ref, trunk)


def _att_outproj_prompt(outs, lses, x, ada, w, ln_g, ln_b, trunk, layer):
    d = D_MODEL
    tm = trunk.tile
    tps = trunk.tiles_per_seq
    k_in = w.shape[0]
    cls_specs = lambda width: [
        pl.BlockSpec((1, dil, tm // dil, width), lambda i: (i // tps, 0, i % tps, 0))
        for _, dil in ATT_GROUPS]
    e = (np.arange(LANES)[:, None] // LSE_REP == np.arange(ATT_W)[None, :] // ATT_DH) / LSE_REP
    return pl.pallas_call(
        functools.partial(_att_outproj_body, trunk=trunk),
        grid=(trunk.tokens // tm,),
        in_specs=cls_specs(ATT_W) + cls_specs(LANES) + [
            _resident((LANES, ATT_W), lambda i: (0, 0)),
            pl.BlockSpec((tm, d), lambda i: (i, 0)),
            trunk.mod_spec(layer, 1),
            _resident((k_in, d), lambda i: (0, 0)),
            _resident((1, d), lambda i: (0, 0)),
            _resident((1, d), lambda i: (0, 0)),
        ],
        out_specs=pl.BlockSpec((tm, d), lambda i: (i, 0)),
        out_shape=jax.ShapeDtypeStruct((trunk.tokens, d), F32),
        scratch_shapes=[pltpu.VMEM((N_GROUPS * (ATT_W // LANES + 1), tm, LANES), F32)],
        compiler_params=_params("arbitrary"),
        name="att_outproj",
    )(*outs, *lses, jnp.asarray(e, BF16), x, ada, w, ln_g.reshape(1, d), ln_b.reshape(1, d))


def _group_norm_gate(o, g, gn_g, gn_b):
    mu = jnp.mean(o, -1, keepdims=True)
    oc = o - mu
    var = jnp.mean(oc * oc, -1, keepdims=True)
    on = oc * lax.rsqrt(var + NORM_EPS) * gn_g + gn_b
    return jax.nn.silu(g) * on


RET_TILE = 1024


def _ret_decay_tables(chunk):
    lg = _log_gamma()
    pos = np.arange(chunk, dtype=np.float64)
    diff = pos[:, None] - pos[None, :]
    inner = np.where(diff >= 0, np.exp(np.maximum(diff, 0.0)[None] * lg[:, None, None]), 0.0)
    qd = np.exp((pos[None, :] + 1.0) * lg[:, None])
    kd = np.exp((chunk - 1.0 - pos[None, :]) * lg[:, None])
    cd = np.exp(chunk * lg)
    return inner, qd, kd, cd


def _ret_prompt_body(q_ref, k_ref, v_ref, g_ref, inner_ref, qd_ref, kd_ref, gng_ref, gnb_ref,
                     o_ref, sfin_ref, s_scr, *, cdec):
    t = pl.program_id(1)

    @pl.when(t == 0)
    def _():
        s_scr[...] = jnp.zeros_like(s_scr)

    for c in range(RET_TILE // RET_CHUNK):
        rows = slice(c * RET_CHUNK, (c + 1) * RET_CHUNK)
        for h in range(RET_HEADS):
            qk = slice(h * RET_DK, (h + 1) * RET_DK)
            vv = slice(h * RET_DV, (h + 1) * RET_DV)
            q = q_ref[rows, qk]
            k = k_ref[rows, qk] * (RET_DK ** -0.5)
            v = v_ref[rows, vv]
            scores = lax.dot_general(q, k, NT_DIMS, preferred_element_type=F32) * inner_ref[h]
            s_old = s_scr[h]
            o = (jnp.dot(scores.astype(BF16), v, preferred_element_type=F32)
                 + jnp.dot(q, s_old.astype(BF16), preferred_element_type=F32) * qd_ref[h])
            kdec = (k.astype(F32) * kd_ref[h]).astype(BF16)
            upd = lax.dot_general(kdec, v, TN_DIMS, preferred_element_type=F32)
            s_scr[h] = s_old * cdec[h] + upd
            gated = _group_norm_gate(o, g_ref[rows, vv].astype(F32), gng_ref[:, vv], gnb_ref[:, vv])
            o_ref[rows, vv] = gated.astype(o_ref.dtype)

    @pl.when(t == pl.num_programs(1) - 1)
    def _():
        sfin_ref[0] = s_scr[...]


def _ret_prompt(qkvg, gn_g, gn_b, n_seq, seq):
    inner, qd, kd, cd = _ret_decay_tables(RET_CHUNK)
    inner = jnp.asarray(inner, F32)
    qd_b = jnp.asarray(np.broadcast_to(qd[:, :, None], (RET_HEADS, RET_CHUNK, RET_DV)), F32)
    kd_b = jnp.asarray(np.broadcast_to(kd[:, :, None], (RET_HEADS, RET_CHUNK, RET_DK)), F32)
    cdec = tuple(float(np.float32(c)) for c in cd)
    tiles = seq // RET_TILE
    vw = RET_HEADS * RET_DV
    qw = RET_HEADS * RET_DK
    return pl.pallas_call(
        functools.partial(_ret_prompt_body, cdec=cdec),
        grid=(n_seq, tiles),
        in_specs=[
            pl.BlockSpec((RET_TILE, qw), lambda n, t: (n * tiles + t, 0)),
            pl.BlockSpec((RET_TILE, qw), lambda n, t: (n * tiles + t, 1)),
            pl.BlockSpec((RET_TILE, vw), lambda n, t: (n * tiles + t, 1)),
            pl.BlockSpec((RET_TILE, vw), lambda n, t: (n * tiles + t, 2)),
            _resident((RET_HEADS, RET_CHUNK, RET_CHUNK), lambda n, t: (0, 0, 0)),
            _resident((RET_HEADS, RET_CHUNK, RET_DV), lambda n, t: (0, 0, 0)),
            _resident((RET_HEADS, RET_CHUNK, RET_DK), lambda n, t: (0, 0, 0)),
            _resident((1, vw), lambda n, t: (0, 0)),
            _resident((1, vw), lambda n, t: (0, 0)),
        ],
        out_specs=[
            pl.BlockSpec((RET_TILE, vw), lambda n, t: (n * tiles + t, 0)),
            pl.BlockSpec((1, RET_HEADS, RET_DK, RET_DV), lambda n, t: (n, 0, 0, 0)),
        ],
        out_shape=[
            jax.ShapeDtypeStruct((n_seq * seq, vw), BF16),
            jax.ShapeDtypeStruct((n_seq, RET_HEADS, RET_DK, RET_DV), F32),
        ],
        scratch_shapes=[pltpu.VMEM((RET_HEADS, RET_DK, RET_DV), F32)],
        compiler_params=_params("arbitrary", "arbitrary"),
        name="retention_prompt",
    )(qkvg, qkvg, qkvg, qkvg, inner, qd_b, kd_b, gn_g.reshape(1, vw), gn_b.reshape(1, vw))


RET_SEQ_GROUP = SUBLANES
RET_S_ROWS = T_SAMPLE * RET_SEQ_GROUP


def _ret_sample_body(q_ref, k_ref, v_ref, g_ref, s_ref, w_ref, qd_ref, kd_ref, cd_ref,
                     gng_ref, gnb_ref, o_ref, so_ref):
    q = q_ref[...].reshape(RET_S_ROWS, RET_DK).astype(BF16)
    k = k_ref[...].reshape(RET_S_ROWS, RET_DK) * (RET_DK ** -0.5)
    v = v_ref[...].reshape(RET_S_ROWS, RET_DV).astype(BF16)
    g = g_ref[...].reshape(RET_S_ROWS, RET_DV)
    scores = lax.dot_general(q, k.astype(BF16), NT_DIMS, preferred_element_type=F32) * w_ref[0]
    o_intra = jnp.dot(scores.astype(BF16), v, preferred_element_type=F32)
    kdec = k * kd_ref[0]
    seq_of_row = lax.broadcasted_iota(jnp.int32, (RET_S_ROWS, 1), 0) % RET_SEQ_GROUP
    o_inter = jnp.zeros((RET_S_ROWS, RET_DV), F32)
    for j in range(RET_SEQ_GROUP):
        mine = seq_of_row == j
        s_old = s_ref[j, 0]
        r = jnp.dot(q, s_old.astype(BF16), preferred_element_type=F32)
        o_inter = jnp.where(mine, r, o_inter)
        kj = jnp.where(mine, kdec, 0.0).astype(BF16)
        upd = lax.dot_general(kj, v, TN_DIMS, preferred_element_type=F32)
        so_ref[j, 0] = s_old * cd_ref[0] + upd
    o = o_intra + o_inter * qd_ref[0]
    gated = _group_norm_gate(o, g, gng_ref[...], gnb_ref[...])
    o_ref[...] = gated.reshape(T_SAMPLE, RET_SEQ_GROUP, RET_DV).astype(o_ref.dtype)


RET_S_STEPS = (N_SAMPLE // RET_SEQ_GROUP) * RET_HEADS


def _ret_sample_part(qkvg, state, gn_g, gn_b):
    inner, qd, kd, cd = _ret_decay_tables(T_SAMPLE)
    row_t = np.arange(RET_S_ROWS) // RET_SEQ_GROUP
    row_j = np.arange(RET_S_ROWS) % RET_SEQ_GROUP
    same = (row_j[:, None] == row_j[None, :])
    w = np.where(same[None], inner[:, row_t[:, None], row_t[None, :]], 0.0)
    w = jnp.asarray(w, F32)
    qd_b = jnp.asarray(np.broadcast_to(qd[:, row_t, None], (RET_HEADS, RET_S_ROWS, RET_DV)), F32)
    kd_b = jnp.asarray(np.broadcast_to(kd[:, row_t, None], (RET_HEADS, RET_S_ROWS, RET_DK)), F32)
    cd_b = jnp.asarray(np.broadcast_to(cd[:, None, None], (RET_HEADS, 1, RET_DV)), F32)
    vw = RET_HEADS * RET_DV
    nq = RET_HEADS
    tg = (T_SAMPLE, RET_SEQ_GROUP)
    nh = RET_HEADS
    return _Part(
        _ret_sample_body,
        (qkvg, qkvg, qkvg, qkvg, state, w, qd_b, kd_b, cd_b,
         gn_g.reshape(1, vw), gn_b.reshape(1, vw)),
        [
            pl.BlockSpec(tg + (RET_DK,), lambda i: (0, i // nh, i % nh)),
            pl.BlockSpec(tg + (RET_DK,), lambda i: (0, i // nh, nq + i % nh)),
            pl.BlockSpec(tg + (RET_DV,), lambda i: (0, i // nh, nq + i % nh)),
            pl.BlockSpec(tg + (RET_DV,), lambda i: (0, i // nh, 2 * nq + i % nh)),
            pl.BlockSpec((RET_SEQ_GROUP, 1, RET_DK, RET_DV), lambda i: (i // nh, i % nh, 0, 0)),
            pl.BlockSpec((1, RET_S_ROWS, RET_S_ROWS), lambda i: (i % nh, 0, 0)),
            pl.BlockSpec((1, RET_S_ROWS, RET_DV), lambda i: (i % nh, 0, 0)),
            pl.BlockSpec((1, RET_S_ROWS, RET_DK), lambda i: (i % nh, 0, 0)),
            pl.BlockSpec((1, 1, RET_DV), lambda i: (i % nh, 0, 0)),
            pl.BlockSpec((1, RET_DV), lambda i: (0, i % nh)),
            pl.BlockSpec((1, RET_DV), lambda i: (0, i % nh)),
        ],
        [
            jax.ShapeDtypeStruct((T_SAMPLE, N_SAMPLE, vw), BF16),
            jax.ShapeDtypeStruct(state.shape, F32),
        ],
        [
            pl.BlockSpec(tg + (RET_DV,), lambda i: (0, i // nh, i % nh)),
            pl.BlockSpec((RET_SEQ_GROUP, 1, RET_DK, RET_DV), lambda i: (i // nh, i % nh, 0, 0)),
        ])


ATT_BQ = 512
ATT_SLAB = 256
HEADS_PER_SLAB = ATT_SLAB // ATT_DH


def _band_window(cur_ref, prev_ref, s):
    if s == 0:
        return jnp.concatenate([prev_ref[0, 0], cur_ref[0, 0, 0:NBACK, :]], axis=0)
    return cur_ref[0, 0, (s - 1) * NBACK:(s + 1) * NBACK, :]


def _band_body(q_ref, kc_ref, kp_ref, vc_ref, vp_ref, bias_ref, o_ref, lse_ref, s_scr, p_scr):
    i = pl.program_id(2)
    ki = lax.broadcasted_iota(jnp.int32, (NBACK, 2 * NBACK), 1)
    has_prev = ki >= jnp.where(i > 0, 0, NBACK)
    lane_head = lax.broadcasted_iota(jnp.int32, (1, ATT_SLAB), 1) // ATT_DH
    qmasks = [jnp.where(lane_head == hh, ATT_SCALE, 0.0).astype(BF16)
              for hh in range(HEADS_PER_SLAB)]
    nsub = ATT_BQ // NBACK
    pairs = nsub * ATT_HPG
    windows = {}

    def window(cur_ref, prev_ref, s):
        key = (id(cur_ref), s)
        if key not in windows:
            windows[key] = _band_window(cur_ref, prev_ref, s)
        return windows[key]

    def place(idx):
        s, h = divmod(idx, ATT_HPG)
        sl, hh = divmod(h, HEADS_PER_SLAB)
        return s, h, hh, slice(sl * ATT_SLAB, (sl + 1) * ATT_SLAB)

    def scores(idx):
        s, _, hh, lanes = place(idx)
        q = q_ref[0, 0, s * NBACK:(s + 1) * NBACK, lanes]
        s_scr[idx] = lax.dot_general(q * qmasks[hh], window(kc_ref, kp_ref, s)[:, lanes], NT_DIMS,
                                     preferred_element_type=F32)

    def softmax(idx):
        s, h, _, _ = place(idx)
        sc = s_scr[idx] + bias_ref[h]
        if s == 0:
            sc = jnp.where(has_prev, sc, -jnp.inf)
        m = jnp.max(sc, -1, keepdims=True)
        e = jnp.exp(sc - m)
        den = jnp.sum(e, -1, keepdims=True)
        p_scr[idx] = (e / den).astype(BF16)
        lse_ref[0, 0, s * NBACK:(s + 1) * NBACK, h * LSE_REP:(h + 1) * LSE_REP] = jnp.broadcast_to(
            m + jnp.log(den), (NBACK, LSE_REP))

    def values(idx):
        s, h, hh, lanes = place(idx)
        oh = jnp.dot(p_scr[idx], window(vc_ref, vp_ref, s)[:, lanes], preferred_element_type=F32)
        o_ref[0, 0, s * NBACK:(s + 1) * NBACK, h * ATT_DH:(h + 1) * ATT_DH] = (
            oh[:, hh * ATT_DH:(hh + 1) * ATT_DH])

    for stage in (scores, softmax, values):
        for idx in range(pairs):
            stage(idx)


def _band_bias(group):
    _, dil = ATT_GROUPS[group]
    steps = NBACK + np.arange(NBACK)[:, None] - np.arange(2 * NBACK)[None, :]
    valid = (steps >= 0) & (steps <= NBACK)
    slopes = _alibi_slopes()[group]
    bias = -(slopes[:, None, None] * dil) * steps[None]
    return np.where(valid[None], bias, -np.inf)


def _band_attention(qkv_cls, group):
    _, dil = ATT_GROUPS[group]
    n_seq, _, length, _ = qkv_cls.shape
    nsub = ATT_BQ // NBACK
    blocks = length // ATT_BQ
    pairs = nsub * ATT_HPG
    prev = lambda i: jnp.maximum(i * nsub - 1, 0)
    cur = lambda col: pl.BlockSpec((1, 1, ATT_BQ, ATT_W), lambda n, r, i: (n, r, i, col))
    halo = lambda col: pl.BlockSpec((1, 1, NBACK, ATT_W), lambda n, r, i: (n, r, prev(i), col))
    out_sds = jax.ShapeDtypeStruct((n_seq, dil, length, ATT_W), F32)
    return pl.pallas_call(
        _band_body,
        grid=(n_seq, dil, blocks),
        in_specs=[cur(0), cur(1), halo(1), cur(2), halo(2),
                  _resident((ATT_HPG, NBACK, 2 * NBACK), lambda n, r, i: (0, 0, 0))],
        out_specs=[cur(0), pl.BlockSpec((1, 1, ATT_BQ, LANES), lambda n, r, i: (n, r, i, 0))],
        out_shape=[out_sds, jax.ShapeDtypeStruct((n_seq, dil, length, LANES), F32)],
        scratch_shapes=[pltpu.VMEM((pairs, NBACK, 2 * NBACK), F32),
                        pltpu.VMEM((pairs, NBACK, 2 * NBACK), BF16)],
        compiler_params=_params("arbitrary", "arbitrary", "arbitrary"),
        name="band_attention",
    )(qkv_cls, qkv_cls, qkv_cls, qkv_cls, qkv_cls, jnp.asarray(_band_bias(group), F32))


SA_PASSES = tuple((0, (t,)) for t in range(T_SAMPLE)) + ((1, tuple(range(T_SAMPLE))),
                                                        (2, tuple(range(T_SAMPLE))))
SA_MAX_LB = max(w for w, _ in ATT_GROUPS)


def _att_sample_tables():
    slopes = _alibi_slopes()
    sels, biases = [], []
    bias_new = np.full((N_GROUPS, T_SAMPLE * ATT_HPG, LANES), -np.inf)
    cs = np.zeros((len(SA_PASSES), SUBLANES, LANES))
    lane = np.arange(LANES)
    for g, (lb, dil) in enumerate(ATT_GROUPS):
        passes = [p for p in SA_PASSES if p[0] == g]
        sel = np.zeros((ATT_HPG, T_SAMPLE, len(passes), ATT_HPG, LANES))
        for pi, (_, t_set) in enumerate(passes):
            for t in t_set:
                owns = np.ones(LANES, bool) if dil == 1 else (lane % dil == t)
                for h in range(ATT_HPG):
                    sel[h, t, pi, h, owns] = 1.0
        sels.append(sel.reshape(ATT_HPG * T_SAMPLE, len(passes) * ATT_HPG * LANES))
        pos = np.arange(lb)
        bias = np.full((T_SAMPLE, ATT_HPG, lb), -np.inf)
        for t in range(T_SAMPLE):
            if dil == 1:
                j = lb + t - pos
                valid = j <= NBACK
            else:
                j = NBACK - pos // dil
                valid = (pos % dil) == t
            bias[t] = np.where(valid[None, :], -(slopes[g][:, None] * dil) * j[None, :], -np.inf)
            for h in range(ATT_HPG):
                for t2 in range(T_SAMPLE):
                    if (dil == 1 and t2 <= t) or t2 == t:
                        bias_new[g, t * ATT_HPG + h, t2] = -slopes[g][h] * dil * (t - t2)
        biases.append(bias)
    for pi, (g, t_set) in enumerate(SA_PASSES):
        dil = ATT_GROUPS[g][1]
        for t in t_set:
            cs[pi, t] = 1.0 if dil == 1 else (lane % dil == t)
    bd = (np.arange(ATT_W)[None, :] // ATT_DH == np.arange(ATT_HPG)[:, None]).astype(np.float64)
    return sels, biases, bias_new, cs, bd


SA_SEQ_PER_STEP = 2


def _sa_new_row(y_ref, j, t, col):
    row = (pl.program_id(0) % (SUBLANES // SA_SEQ_PER_STEP)) * SA_SEQ_PER_STEP + j
    lane0 = (t * 3 * N_GROUPS + col) * ATT_W
    return y_ref[pl.ds(row, 1), lane0:lane0 + ATT_W]


def _sa_new_rows(y_ref, j, col, n_rows):
    rows = [_sa_new_row(y_ref, j, t, col) for t in range(T_SAMPLE)]
    return jnp.concatenate(rows + [jnp.zeros((n_rows - T_SAMPLE, ATT_W), F32)], 0)


def _att_scores_body(qd_ref, y_ref, c0_ref, c1_ref, c2_ref,
                     sel0_ref, sel1_ref, sel2_ref, b0_ref, b1_ref, b2_ref, bn_ref, bd_ref,
                     p0_ref, p1_ref, p2_ref, pn_ref, s_scr, e_scr):
    caches = (c0_ref, c1_ref, c2_ref)
    sels = (sel0_ref, sel1_ref, sel2_ref)
    biases = (b0_ref, b1_ref, b2_ref)
    p_refs = (p0_ref, p1_ref, p2_ref)
    bd = bd_ref[...]
    qsels, s_news = {}, {}
    for j in range(SA_SEQ_PER_STEP):
        for g in range(N_GROUPS):
            qsels[j, g] = jnp.dot(qd_ref[j, g].astype(BF16), sels[g][...],
                                  preferred_element_type=F32)
            qbd = jnp.concatenate(
                [jnp.broadcast_to(_sa_new_row(y_ref, j, t, g), (ATT_HPG, ATT_W)) * bd
                 for t in range(T_SAMPLE)], 0)
            kn = _sa_new_rows(y_ref, j, N_GROUPS + g, LANES)
            s_news[j, g] = lax.dot_general(qbd.astype(BF16), kn.astype(BF16), NT_DIMS,
                                           preferred_element_type=F32) * ATT_SCALE + bn_ref[g]
    for j in range(SA_SEQ_PER_STEP):
        e_new, dens, lses = {}, {}, {}
        pid = 0
        for g, (lb, dil) in enumerate(ATT_GROUPS):
            chunks = lb // LANES
            passes = [p for p in SA_PASSES if p[0] == g]
            qsel, s_new = qsels[j, g], s_news[j, g]
            pass_of = {}
            for local, (_, t_set) in enumerate(passes):
                for h in range(ATT_HPG):
                    col = (local * ATT_HPG + h) * LANES
                    qs = qsel[:, col:col + LANES]
                    for c in range(chunks):
                        lanes = slice(c * LANES, (c + 1) * LANES)
                        prod = caches[g][j, 0, h, :, lanes] * qs
                        part = prod.reshape(ATT_DH // SUBLANES, SUBLANES, LANES).sum(0)
                        s_scr[pid, h:h + 1, lanes] = part.sum(0, keepdims=True)
                for t in t_set:
                    pass_of[t] = pid
                pid += 1
            for t in range(T_SAMPLE):
                sc = s_scr[pass_of[t], :, :lb] * ATT_SCALE + biases[g][t]
                sn = s_new[t * ATT_HPG:(t + 1) * ATT_HPG]
                m = jnp.maximum(jnp.max(sc, -1, keepdims=True), jnp.max(sn, -1, keepdims=True))
                e = jnp.exp(sc - m)
                en = jnp.exp(sn - m)
                den = jnp.sum(e, -1, keepdims=True) + jnp.sum(en, -1, keepdims=True)
                e_scr[g * T_SAMPLE + t, :, :lb] = e
                e_new[g, t], dens[g, t], lses[g, t] = en, den, m + jnp.log(den)
        wts = {}
        for t in range(T_SAMPLE):
            ls = [lses[g, t] for g in range(N_GROUPS)]
            m = jnp.maximum(jnp.maximum(ls[0], ls[1]), ls[2])
            ws = [jnp.exp(l - m) for l in ls]
            tot = ws[0] + ws[1] + ws[2]
            for g in range(N_GROUPS):
                wts[g, t] = ws[g] / (tot * dens[g, t])
        for g, (lb, dil) in enumerate(ATT_GROUPS):
            pn_ref[j, g] = jnp.concatenate(
                [e_new[g, t] * wts[g, t] for t in range(T_SAMPLE)], 0)
            for local, (_, t_set) in enumerate([p for p in SA_PASSES if p[0] == g]):
                pc = None
                for t in t_set:
                    term = e_scr[g * T_SAMPLE + t, :, :lb] * wts[g, t]
                    pc = term if pc is None else pc + term
                p_refs[g][j, local] = pc


def _att_values_body(p0_ref, p1_ref, p2_ref, pn_ref, y_ref, c0_ref, c1_ref, c2_ref,
                     cs_ref, bd_ref, o_ref):
    caches = (c0_ref, c1_ref, c2_ref)
    p_refs = (p0_ref, p1_ref, p2_ref)
    bd = bd_ref[...]
    wfs = {}
    for j in range(SA_SEQ_PER_STEP):
        for pid, (g, _) in enumerate(SA_PASSES):
            local = pid - [p[0] for p in SA_PASSES].index(g)
            folded = []
            for h in range(ATT_HPG):
                acc = None
                for c in range(ATT_GROUPS[g][0] // LANES):
                    lanes = slice(c * LANES, (c + 1) * LANES)
                    term = caches[g][j, 0, h, :, lanes] * p_refs[g][j, local, h:h + 1, lanes]
                    acc = term if acc is None else acc + term
                folded.append(acc)
            wfs[j, pid] = jnp.concatenate(folded, 0).astype(BF16)
    for j in range(SA_SEQ_PER_STEP):
        o = jnp.zeros((SUBLANES, ATT_W), F32)
        o_new = jnp.zeros((T_SAMPLE * ATT_HPG, ATT_W), F32)
        for g in range(N_GROUPS):
            vn = _sa_new_rows(y_ref, j, 2 * N_GROUPS + g, LANES)
            o_new = o_new + jnp.dot(pn_ref[j, g].astype(BF16), vn.astype(BF16),
                                    preferred_element_type=F32)
        for pid in range(len(SA_PASSES)):
            o = o + lax.dot_general(cs_ref[pid].astype(BF16), wfs[j, pid], NT_DIMS,
                                    preferred_element_type=F32)
        extra = [jnp.sum(o_new[t * ATT_HPG:(t + 1) * ATT_HPG] * bd, 0, keepdims=True)
                 for t in range(T_SAMPLE)]
        o_ref[j] = o[:T_SAMPLE] + jnp.concatenate(extra, 0)


SA_STEPS = N_SAMPLE // SA_SEQ_PER_STEP


def _sa_new_spec():
    per_block = SUBLANES // SA_SEQ_PER_STEP
    return pl.BlockSpec((SUBLANES, T_SAMPLE * 3 * N_GROUPS * ATT_W), lambda i: (i // per_block, 0))


def _sa_cache_specs(caches, kv):
    views = [c.transpose(0, 2, 3, 4, 1) for c in caches]
    specs = [pl.BlockSpec((SA_SEQ_PER_STEP, 1, ATT_HPG, ATT_DH, lb), lambda i: (i, kv, 0, 0, 0))
             for lb, _ in ATT_GROUPS]
    return views, specs


def _sa_const_specs(tables):
    return [_resident(t.shape, lambda i, nd=t.ndim: (0,) * nd) for t in tables]


def _sa_prob_layout():
    shapes, specs = [], []
    for g, (lb, _) in enumerate(ATT_GROUPS):
        n_pass = len([p for p in SA_PASSES if p[0] == g])
        shapes.append(jax.ShapeDtypeStruct((N_SAMPLE, n_pass, ATT_HPG, lb), F32))
        specs.append(pl.BlockSpec((SA_SEQ_PER_STEP, n_pass, ATT_HPG, lb), lambda i: (i, 0, 0, 0)))
    shapes.append(jax.ShapeDtypeStruct((N_SAMPLE, N_GROUPS, T_SAMPLE * ATT_HPG, LANES), F32))
    specs.append(pl.BlockSpec((SA_SEQ_PER_STEP, N_GROUPS, T_SAMPLE * ATT_HPG, LANES),
                              lambda i: (i, 0, 0, 0)))
    return shapes, specs


def _att_scores_part(y, caches):
    sels, biases, bias_new, _, bd = _att_sample_tables()
    q = y.reshape(N_SAMPLE, T_SAMPLE, 3 * N_GROUPS, ATT_HPG, ATT_DH)[:, :, :N_GROUPS]
    qd = q.transpose(0, 2, 4, 3, 1).reshape(N_SAMPLE, N_GROUPS, ATT_DH, ATT_HPG * T_SAMPLE)
    views, cache_specs = _sa_cache_specs(caches, 0)
    tables = ([jnp.asarray(s, BF16) for s in sels] + [jnp.asarray(b, F32) for b in biases]
              + [jnp.asarray(bias_new, F32), jnp.asarray(bd, F32)])
    out_shape, out_specs = _sa_prob_layout()
    return _Part(
        _att_scores_body,
        (qd, y, *views, *tables),
        [pl.BlockSpec((SA_SEQ_PER_STEP, N_GROUPS, ATT_DH, ATT_HPG * T_SAMPLE),
                      lambda i: (i, 0, 0, 0)), _sa_new_spec()]
        + cache_specs + _sa_const_specs(tables),
        out_shape, out_specs,
        [pltpu.VMEM((len(SA_PASSES), ATT_HPG, SA_MAX_LB), F32),
         pltpu.VMEM((N_GROUPS * T_SAMPLE, ATT_HPG, SA_MAX_LB), F32)])


def _att_values_part(probs, y, caches):
    _, _, _, cs, bd = _att_sample_tables()
    views, cache_specs = _sa_cache_specs(caches, 1)
    tables = [jnp.asarray(cs, F32), jnp.asarray(bd, F32)]
    _, prob_specs = _sa_prob_layout()
    return _Part(
        _att_values_body,
        (*probs, y, *views, *tables),
        prob_specs + [_sa_new_spec()] + cache_specs + _sa_const_specs(tables),
        [jax.ShapeDtypeStruct((N_SAMPLE, T_SAMPLE, ATT_W), F32)],
        [pl.BlockSpec((SA_SEQ_PER_STEP, T_SAMPLE, ATT_W), lambda i: (i, 0, 0))])


def kernel(x_prompt, x_sample, c_prompt, c_sample, state_ret, cache_kv_w128, cache_kv_w512,
           cache_kv_w2048, w_ada, b_ada, ln_g, ln_b, w_ffn_in, w_ffn_out, w_ret_in, ret_gn_g,
           ret_gn_b, w_ret_out, w_att_in, w_att_out):
    d = D_MODEL
    n_p, seq, _ = x_prompt.shape
    w_ffn_in_b = {(0, 0): w_ffn_in[0, 0].astype(BF16)}
    w_ffn_out_b = {(0, 0): w_ffn_out[0, 0].astype(BF16)}
    w_ret_in_b = w_ret_in.astype(BF16)
    later_ffn = [(0, 1), (1, 0), (1, 1)]
    cast_items = ([(w_ffn_in, lw) for lw in later_ffn] + [(w_ffn_out, lw) for lw in later_ffn]
                  + [(w_ret_out, ()), (w_att_in, ()), (w_att_out, ())])

    c_all = jnp.concatenate(
        [c_sample, c_prompt, jnp.zeros((ADA_ROWS - N_SAMPLE - n_p, d), F32)], axis=0)
    ada = _ada_table(c_all, w_ada, b_ada)

    prompt = _Trunk(n_p * seq, 512, False, seq // 512)
    prompt_wide = _Trunk(n_p * seq, 1024, False, seq // 1024)
    sample = _Trunk(N_SAMPLE * T_SAMPLE, N_SAMPLE, True, 1)
    xp = x_prompt.reshape(n_p * seq, d)
    xs = x_sample.transpose(1, 0, 2).reshape(T_SAMPLE * N_SAMPLE, d)

    steps = RET_S_STEPS
    tile_s = n_p * seq // steps
    prompt_s = _Trunk(n_p * seq, tile_s, False, seq // tile_s)

    def ffn_part(x, trunk, layer, which):
        sub = 0 if which == 0 else 2
        return _ffn_part(x, ada, w_ffn_in_b[layer, which], w_ffn_out_b[layer, which],
                         ln_g[layer, sub], ln_b[layer, sub], trunk, layer, sub)

    def ffn(x, trunk, layer, which):
        return _run([ffn_part(x, trunk, layer, which)], trunk.tokens // trunk.tile, "ffn")[0][0]

    xs = ffn(xs, sample, 0, 0)
    qkvg_s = _inproj(xs, ada, w_ret_in_b, sample, 0, F32)

    (xp,), (gated_s, ret_s), cast = _run(
        [ffn_part(xp, prompt_s, 0, 0),
         _ret_sample_part(qkvg_s.reshape(T_SAMPLE, N_SAMPLE, -1), state_ret, ret_gn_g, ret_gn_b),
         _cast_part(cast_items, steps)],
        steps, "ffn_ret_sample")
    for k, lw in enumerate(later_ffn):
        w_ffn_in_b[lw], w_ffn_out_b[lw] = cast[k], cast[len(later_ffn) + k]
    w_ret_out_b, w_att_in_b, w_att_out_b = cast[2 * len(later_ffn):]

    xs = _outproj(gated_s.reshape(T_SAMPLE * N_SAMPLE, -1), xs, ada, w_ret_out_b,
                  ln_g[0, 1], ln_b[0, 1], sample, 0)
    xs = ffn(xs, sample, 0, 1)
    xs = ffn(xs, sample, 1, 0)
    qkv_s = _inproj(xs, ada, w_att_in_b, sample, 1, F32, tiles_to_columns=True)
    caches = (cache_kv_w128, cache_kv_w512, cache_kv_w2048)

    qkvg_p = _inproj(xp, ada, w_ret_in_b, prompt, 0, BF16)
    gated_p, ret_p = _ret_prompt(qkvg_p, ret_gn_g, ret_gn_b, n_p, seq)
    xp = _outproj(gated_p, xp, ada, w_ret_out_b, ln_g[0, 1], ln_b[0, 1], prompt_wide, 0)

    probs, (xp,) = _run([_att_scores_part(qkv_s, caches), ffn_part(xp, prompt_s, 0, 1)],
                        steps, "att_scores_ffn")
    (xp,), (att_s,) = _run([ffn_part(xp, prompt_s, 1, 0), _att_values_part(probs, qkv_s, caches)],
                           steps, "ffn_att_values")

    keep_p = [min(w, seq) for w, _ in ATT_GROUPS]
    *qkv_cls, kv_tail = _att_inproj_prompt(xp, ada, w_att_in_b, prompt, 1, max(keep_p))
    band = [_band_attention(qkv_cls[g], g) for g in range(N_GROUPS)]
    xp = _att_outproj_prompt([o for o, _ in band], [l for _, l in band], xp, ada, w_att_out_b,
                             ln_g[1, 1], ln_b[1, 1], prompt_wide, 1)
    kv_tail = kv_tail.reshape(n_p, max(keep_p), 2, N_GROUPS, ATT_HPG, ATT_DH)
    kv_p = [kv_tail[:, max(keep_p) - keep_p[g]:, :, g] for g in range(N_GROUPS)]

    att_s = att_s.transpose(1, 0, 2).reshape(T_SAMPLE * N_SAMPLE, ATT_W)
    xs = _outproj(att_s, xs, ada, w_att_out_b, ln_g[1, 1], ln_b[1, 1], sample, 1)
    new_s = qkv_s.reshape(N_SAMPLE, T_SAMPLE, 3, N_GROUPS, ATT_HPG, ATT_DH)
    kv_s = [new_s[:, :, 1:, g] for g in range(N_GROUPS)]

    xp = ffn(xp, prompt, 1, 1)
    xs = ffn(xs, sample, 1, 1)

    y_prompt = xp.reshape(n_p, seq, d)
    y_sample = xs.reshape(T_SAMPLE, N_SAMPLE, d).transpose(1, 0, 2)
    return (y_prompt, y_sample, ret_p, ret_s, kv_p[0], kv_s[0], kv_p[1], kv_s[1], kv_p[2], kv_s[2])
```

```python
import functools

import numpy as np
import jax
import jax.numpy as jnp
from jax import lax
from jax.experimental import pallas as pl
from jax.experimental.pallas import tpu as pltpu

F32 = jnp.float32
BF16 = jnp.bfloat16

D_MODEL = 1024
DEPTH = 2
D_FF = 2816
RET_HEADS = 4
RET_DK = 256
RET_DV = 512
RET_CHUNK = 128
ATT_GROUPS = ((128, 1), (512, 4), (2048, 16))
N_GROUPS = 3
ATT_HPG = 8
ATT_DH = 64
ATT_W = ATT_HPG * ATT_DH
NBACK = 128
LSE_REP = 128 // ATT_HPG
ALPHA = (2 * DEPTH) ** 0.25
NORM_EPS = 1e-5
ATT_SCALE = ATT_DH ** -0.5

V7X_VMEM_LIMIT_BYTES = 60000 * 1024
LANES = 128
SUBLANES = 8

N_SAMPLE = 128
T_SAMPLE = 4
ADA_ROWS = 136
PROMPT_ROW_BLOCK = N_SAMPLE // SUBLANES

NT_DIMS = (((1,), (1,)), ((), ()))
TN_DIMS = (((0,), (0,)), ((), ()))


def _params(*semantics):
    return pltpu.CompilerParams(dimension_semantics=semantics,
                                vmem_limit_bytes=V7X_VMEM_LIMIT_BYTES)


def _resident(block, index_map):
    return pl.BlockSpec(block, index_map, pipeline_mode=pl.Buffered(1))


def _layer_norm(z, g, b):
    mu = jnp.mean(z, -1, keepdims=True)
    zc = z - mu
    var = jnp.mean(zc * zc, -1, keepdims=True)
    return zc * lax.rsqrt(var + NORM_EPS) * g + b


def _log_gamma():
    return np.log1p(-(2.0 ** (-5.0 - np.arange(RET_HEADS, dtype=np.float64))))


def _alibi_slopes():
    h = np.arange(1, N_GROUPS * ATT_HPG + 1, dtype=np.float64)
    return (2.0 ** (-8.0 * h / (N_GROUPS * ATT_HPG))).reshape(N_GROUPS, ATT_HPG)


ADA_SLABS = 3


def _ada_body(c_ref, w_ref, b_ref, o_ref):
    s = jax.nn.silu(c_ref[...]).astype(BF16)
    for k in range(ADA_SLABS):
        w = w_ref[0, :, k * D_MODEL:(k + 1) * D_MODEL].astype(BF16)
        o_ref[0, k] = jnp.dot(s, w, preferred_element_type=F32) + b_ref[0, k]


def _ada_table(c_all, w_ada, b_ada):
    d = D_MODEL
    return pl.pallas_call(
        _ada_body,
        grid=(DEPTH, 9 // ADA_SLABS),
        in_specs=[
            _resident((ADA_ROWS, d), lambda i, j: (0, 0)),
            pl.BlockSpec((1, d, ADA_SLABS * d), lambda i, j: (i, 0, j)),
            pl.BlockSpec((1, ADA_SLABS, 1, d), lambda i, j: (i, j, 0, 0)),
        ],
        out_specs=pl.BlockSpec((1, ADA_SLABS, ADA_ROWS, d), lambda i, j: (i, j, 0, 0)),
        out_shape=jax.ShapeDtypeStruct((DEPTH, 9, ADA_ROWS, d), F32),
        compiler_params=_params("arbitrary", "arbitrary"),
        name="ada_table",
    )(c_all, w_ada, b_ada.reshape(DEPTH, 9, 1, d))


class _Trunk:
    def __init__(self, tokens, tile, per_row, tiles_per_seq):
        self.tokens = tokens
        self.tile = tile
        self.per_row = per_row
        self.tiles_per_seq = tiles_per_seq

    def mod_spec(self, layer, sub):
        if self.per_row:
            return pl.BlockSpec((1, 3, self.tile, D_MODEL), lambda i: (layer, sub, 0, 0))
        return pl.BlockSpec((1, 3, SUBLANES, D_MODEL),
                            lambda i: (layer, sub, PROMPT_ROW_BLOCK, 0))

    def mod_rows(self, mod_ref, k):
        if self.per_row:
            return mod_ref[0, k]
        n = pl.program_id(0) // self.tiles_per_seq
        return mod_ref[0, k, pl.ds(n, 1), :]


def _ffn_body(x_ref, mod_ref, win_ref, wout_ref, g_ref, b_ref, o_ref, *, trunk):
    x = x_ref[...]
    shift = trunk.mod_rows(mod_ref, 0)
    scale = trunk.mod_rows(mod_ref, 1)
    gate = trunk.mod_rows(mod_ref, 2)
    u = (x * (1.0 + scale) + shift).astype(BF16)
    h = jnp.dot(u, win_ref[...], preferred_element_type=F32)
    a = h[:, :D_FF]
    b = h[:, D_FF:]
    act = (jax.nn.silu(a) * b).astype(BF16)
    y = jnp.dot(act, wout_ref[...], preferred_element_type=F32)
    z = ALPHA * x + (0.5 * (1.0 + gate)) * y
    o_ref[...] = _layer_norm(z, g_ref[...], b_ref[...])


class _Part:
    def __init__(self, body, args, in_specs, out_shape, out_specs, scratch=()):
        self.body, self.args, self.in_specs = body, list(args), list(in_specs)
        self.out_shape, self.out_specs, self.scratch = list(out_shape), list(out_specs), list(scratch)


def _run(parts, steps, name):
    n_in = [len(p.args) for p in parts]
    n_out = [len(p.out_shape) for p in parts]
    n_scr = [len(p.scratch) for p in parts]

    def body(*refs):
        ins, outs, scr = refs[:sum(n_in)], refs[sum(n_in):sum(n_in) + sum(n_out)], refs[sum(n_in) + sum(n_out):]
        a = b = c = 0
        for k, p in enumerate(parts):
            p.body(*ins[a:a + n_in[k]], *outs[b:b + n_out[k]], *scr[c:c + n_scr[k]])
            a, b, c = a + n_in[k], b + n_out[k], c + n_scr[k]

    res = pl.pallas_call(
        body,
        grid=(steps,),
        in_specs=[s for p in parts for s in p.in_specs],
        out_specs=[s for p in parts for s in p.out_specs],
        out_shape=[s for p in parts for s in p.out_shape],
        scratch_shapes=[s for p in parts for s in p.scratch],
        compiler_params=_params("arbitrary"),
        name=name,
    )(*[a for p in parts for a in p.args])
    out, b = [], 0
    for k in range(len(parts)):
        out.append(res[b:b + n_out[k]])
        b += n_out[k]
    return out


def _cast_body(*refs):
    n = len(refs) // 2
    for src, dst in zip(refs[:n], refs[n:]):
        dst[...] = src[...].reshape(dst.shape).astype(dst.dtype)


def _cast_part(items, steps):
    args, in_specs, out_shape, out_specs = [], [], [], []
    for arr, lead in items:
        rows, cols = arr.shape[-2:]
        n_blocks = steps
        while rows % n_blocks or (rows // n_blocks) % (2 * SUBLANES):
            n_blocks //= 2
        blk = rows // n_blocks
        args.append(arr)
        in_specs.append(pl.BlockSpec(
            (1,) * len(lead) + (blk, cols),
            lambda i, lead=lead, last=n_blocks - 1: lead + (jnp.minimum(i, last), 0)))
        out_shape.append(jax.ShapeDtypeStruct((rows, cols), BF16))
        out_specs.append(pl.BlockSpec((blk, cols),
                                      lambda i, last=n_blocks - 1: (jnp.minimum(i, last), 0)))
    return _Part(_cast_body, args, in_specs, out_shape, out_specs)


def _ffn_part(x, ada, w_in, w_out, ln_g, ln_b, trunk, layer, sub):
    d = D_MODEL
    tm = trunk.tile
    return _Part(
        functools.partial(_ffn_body, trunk=trunk),
        (x, ada, w_in, w_out, ln_g.reshape(1, d), ln_b.reshape(1, d)),
        [
            pl.BlockSpec((tm, d), lambda i: (i, 0)),
            trunk.mod_spec(layer, sub),
            _resident((d, 2 * D_FF), lambda i: (0, 0)),
            _resident((D_FF, d), lambda i: (0, 0)),
            _resident((1, d), lambda i: (0, 0)),
            _resident((1, d), lambda i: (0, 0)),
        ],
        [jax.ShapeDtypeStruct((trunk.tokens, d), F32)],
        [pl.BlockSpec((tm, d), lambda i: (i, 0))])


def _inproj_body(x_ref, mod_ref, w_ref, o_ref, *, trunk):
    x = x_ref[...]
    u = (x * (1.0 + trunk.mod_rows(mod_ref, 1)) + trunk.mod_rows(mod_ref, 0)).astype(BF16)
    o_ref[...] = jnp.dot(u, w_ref[...], preferred_element_type=F32).astype(o_ref.dtype)


def _inproj(x, ada, w, trunk, layer, out_dtype, tiles_to_columns=False):
    d = D_MODEL
    tm = trunk.tile
    n_out = w.shape[1]
    tiles = trunk.tokens // tm
    if tiles_to_columns:
        out_spec = pl.BlockSpec((tm, n_out), lambda i: (0, i))
        out_shape = jax.ShapeDtypeStruct((tm, tiles * n_out), out_dtype)
    else:
        out_spec = pl.BlockSpec((tm, n_out), lambda i: (i, 0))
        out_shape = jax.ShapeDtypeStruct((trunk.tokens, n_out), out_dtype)
    return pl.pallas_call(
        functools.partial(_inproj_body, trunk=trunk),
        grid=(tiles,),
        in_specs=[
            pl.BlockSpec((tm, d), lambda i: (i, 0)),
            trunk.mod_spec(layer, 1),
            _resident((d, n_out), lambda i: (0, 0)),
        ],
        out_specs=out_spec,
        out_shape=out_shape,
        compiler_params=_params("arbitrary"),
        name="inproj",
    )(x, ada, w)


def _att_inproj_body(x_ref, mod_ref, w_ref, a0_ref, a1_ref, a2_ref, tail_ref, y_scr, *, trunk):
    x = x_ref[...]
    u = (x * (1.0 + trunk.mod_rows(mod_ref, 1)) + trunk.mod_rows(mod_ref, 0)).astype(BF16)
    slabs = ATT_W // LANES
    a_refs = (a0_ref, a1_ref, a2_ref)
    for part in range(3):
        for g in range(N_GROUPS):
            col0 = (part * N_GROUPS + g) * ATT_W
            for c in range(ATT_W // ATT_SLAB):
                lo = col0 + c * ATT_SLAB
                y = jnp.dot(u, w_ref[:, lo:lo + ATT_SLAB], preferred_element_type=F32)
                if part > 0:
                    tail_ref[0, :, lo - N_GROUPS * ATT_W:lo - N_GROUPS * ATT_W + ATT_SLAB] = y
                for half in range(ATT_SLAB // LANES):
                    y_scr[lo // LANES + half] = y[:, half * LANES:(half + 1) * LANES]
            dil = ATT_GROUPS[g][1]
            rows = trunk.tile // dil
            for r in range(dil):
                sel = pl.ds(r, rows, stride=dil) if dil > 1 else slice(None)
                for cb in range(slabs):
                    dst = (part * slabs + cb) * LANES
                    a_refs[g][0, r, :, dst:dst + LANES] = (
                        y_scr[col0 // LANES + cb, sel, :].astype(BF16))


def _att_inproj_prompt(x, ada, w, trunk, layer, tail_rows):
    d = D_MODEL
    tm = trunk.tile
    n_out = w.shape[1]
    tps = trunk.tiles_per_seq
    n_seq = trunk.tokens // (tps * tm)
    first = tps - tail_rows // tm
    out_shape, out_specs = [], []
    for _, dil in ATT_GROUPS:
        out_shape.append(jax.ShapeDtypeStruct((n_seq, dil, tps * tm // dil, 3 * ATT_W), BF16))
        out_specs.append(pl.BlockSpec((1, dil, tm // dil, 3 * ATT_W),
                                      lambda i: (i // tps, 0, i % tps, 0)))
    kv_w = n_out - N_GROUPS * ATT_W
    out_shape.append(jax.ShapeDtypeStruct((n_seq, tail_rows, kv_w), F32))
    out_specs.append(pl.BlockSpec((1, tm, kv_w),
                                  lambda i: (i // tps, jnp.maximum(i % tps - first, 0), 0)))
    return pl.pallas_call(
        functools.partial(_att_inproj_body, trunk=trunk),
        grid=(trunk.tokens // tm,),
        in_specs=[
            pl.BlockSpec((tm, d), lambda i: (i, 0)),
            trunk.mod_spec(layer, 1),
            _resident((d, n_out), lambda i: (0, 0)),
        ],
        out_specs=out_specs,
        out_shape=out_shape,
        scratch_shapes=[pltpu.VMEM((n_out // LANES, tm, LANES), F32)],
        compiler_params=_params("arbitrary"),
        name="att_inproj",
    )(x, ada, w)


def _outproj_tail(a, x_ref, mod_ref, w_ref, g_ref, b_ref, o_ref, trunk):
    x = x_ref[...]
    gate = trunk.mod_rows(mod_ref, 2)
    y = jnp.dot(a.astype(BF16), w_ref[...], preferred_element_type=F32)
    z = ALPHA * x + (1.0 + gate) * y
    o_ref[...] = _layer_norm(z, g_ref[...], b_ref[...])


def _outproj_body(a_ref, x_ref, mod_ref, w_ref, g_ref, b_ref, o_ref, *, trunk):
    _outproj_tail(a_ref[...], x_ref, mod_ref, w_ref, g_ref, b_ref, o_ref, trunk)


def _outproj(a, x, ada, w, ln_g, ln_b, trunk, layer):
    d = D_MODEL
    tm = trunk.tile
    k_in = w.shape[0]
    return pl.pallas_call(
        functools.partial(_outproj_body, trunk=trunk),
        grid=(trunk.tokens // tm,),
        in_specs=[
            pl.BlockSpec((tm, k_in), lambda i: (i, 0)),
            pl.BlockSpec((tm, d), lambda i: (i, 0)),
            trunk.mod_spec(layer, 1),
            _resident((k_in, d), lambda i: (0, 0)),
            _resident((1, d), lambda i: (0, 0)),
            _resident((1, d), lambda i: (0, 0)),
        ],
        out_specs=pl.BlockSpec((tm, d), lambda i: (i, 0)),
        out_shape=jax.ShapeDtypeStruct((trunk.tokens, d), F32),
        compiler_params=_params("arbitrary"),
        name="outproj",
    )(a, x, ada, w, ln_g.reshape(1, d), ln_b.reshape(1, d))


def _att_outproj_body(o0_ref, o1_ref, o2_ref, l0_ref, l1_ref, l2_ref, e_ref, x_ref, mod_ref, w_ref,
                      g_ref, b_ref, out_ref, tok_scr, *, trunk):
    def token_major(ref, g, first_slab):
        dil = ATT_GROUPS[g][1]
        slabs = ref.shape[-1] // LANES
        if dil == 1:
            return ref[0, 0].astype(F32)
        rows = trunk.tile // dil
        for r in range(dil):
            for cb in range(slabs):
                tok_scr[first_slab + cb, pl.ds(r, rows, stride=dil), :] = (
                    ref[0, r, :, cb * LANES:(cb + 1) * LANES].astype(F32))
        return jnp.concatenate([tok_scr[first_slab + cb] for cb in range(slabs)], axis=1)

    o_slabs = ATT_W // LANES
    os = [token_major(r, g, g * o_slabs) for g, r in enumerate((o0_ref, o1_ref, o2_ref))]
    ls = [token_major(r, g, N_GROUPS * o_slabs + g)
          for g, r in enumerate((l0_ref, l1_ref, l2_ref))]
    m = jnp.maximum(jnp.maximum(ls[0], ls[1]), ls[2])
    ws = [jnp.exp(l - m) for l in ls]
    inv = 1.0 / (ws[0] + ws[1] + ws[2])
    a = None
    for g in range(N_GROUPS):
        alpha = ws[g] * inv
        hi = alpha.astype(BF16)
        lo = (alpha - hi.astype(F32)).astype(BF16)
        spread = (jnp.dot(hi, e_ref[...], preferred_element_type=F32)
                  + jnp.dot(lo, e_ref[...], preferred_element_type=F32))
        term = spread * os[g]
        a = term if a is None else a + term
    _outproj_tail(a, x_ref, mod_ref, w_ref, g_ref, b_ref, out_ref, trunk)


def _att_outproj_prompt(outs, lses, x, ada, w, ln_g, ln_b, trunk, layer):
    d = D_MODEL
    tm = trunk.tile
    tps = trunk.tiles_per_seq
    k_in = w.shape[0]
    cls_specs = lambda width: [
        pl.BlockSpec((1, dil, tm // dil, width), lambda i: (i // tps, 0, i % tps, 0))
        for _, dil in ATT_GROUPS]
    e = (np.arange(LANES)[:, None] // LSE_REP == np.arange(ATT_W)[None, :] // ATT_DH) / LSE_REP
    return pl.pallas_call(
        functools.partial(_att_outproj_body, trunk=trunk),
        grid=(trunk.tokens // tm,),
        in_specs=cls_specs(ATT_W) + cls_specs(LANES) + [
            _resident((LANES, ATT_W), lambda i: (0, 0)),
            pl.BlockSpec((tm, d), lambda i: (i, 0)),
            trunk.mod_spec(layer, 1),
            _resident((k_in, d), lambda i: (0, 0)),
            _resident((1, d), lambda i: (0, 0)),
            _resident((1, d), lambda i: (0, 0)),
        ],
        out_specs=pl.BlockSpec((tm, d), lambda i: (i, 0)),
        out_shape=jax.ShapeDtypeStruct((trunk.tokens, d), F32),
        scratch_shapes=[pltpu.VMEM((N_GROUPS * (ATT_W // LANES + 1), tm, LANES), F32)],
        compiler_params=_params("arbitrary"),
        name="att_outproj",
    )(*outs, *lses, jnp.asarray(e, BF16), x, ada, w, ln_g.reshape(1, d), ln_b.reshape(1, d))


def _group_norm_gate(o, g, gn_g, gn_b):
    mu = jnp.mean(o, -1, keepdims=True)
    oc = o - mu
    var = jnp.mean(oc * oc, -1, keepdims=True)
    on = oc * lax.rsqrt(var + NORM_EPS) * gn_g + gn_b
    return jax.nn.silu(g) * on


RET_TILE = 1024


def _ret_decay_tables(chunk):
    lg = _log_gamma()
    pos = np.arange(chunk, dtype=np.float64)
    diff = pos[:, None] - pos[None, :]
    inner = np.where(diff >= 0, np.exp(np.maximum(diff, 0.0)[None] * lg[:, None, None]), 0.0)
    qd = np.exp((pos[None, :] + 1.0) * lg[:, None])
    kd = np.exp((chunk - 1.0 - pos[None, :]) * lg[:, None])
    cd = np.exp(chunk * lg)
    return inner, qd, kd, cd


def _ret_prompt_body(q_ref, k_ref, v_ref, g_ref, inner_ref, qd_ref, kd_ref, gng_ref, gnb_ref,
                     o_ref, sfin_ref, s_scr, *, cdec):
    t = pl.program_id(1)

    @pl.when(t == 0)
    def _():
        s_scr[...] = jnp.zeros_like(s_scr)

    for c in range(RET_TILE // RET_CHUNK):
        rows = slice(c * RET_CHUNK, (c + 1) * RET_CHUNK)
        for h in range(RET_HEADS):
            qk = slice(h * RET_DK, (h + 1) * RET_DK)
            vv = slice(h * RET_DV, (h + 1) * RET_DV)
            q = q_ref[rows, qk]
            k = k_ref[rows, qk] * (RET_DK ** -0.5)
            v = v_ref[rows, vv]
            scores = lax.dot_general(q, k, NT_DIMS, preferred_element_type=F32) * inner_ref[h]
            s_old = s_scr[h]
            o = (jnp.dot(scores.astype(BF16), v, preferred_element_type=F32)
                 + jnp.dot(q, s_old.astype(BF16), preferred_element_type=F32) * qd_ref[h])
            kdec = (k.astype(F32) * kd_ref[h]).astype(BF16)
            upd = lax.dot_general(kdec, v, TN_DIMS, preferred_element_type=F32)
            s_scr[h] = s_old * cdec[h] + upd
            gated = _group_norm_gate(o, g_ref[rows, vv].astype(F32), gng_ref[:, vv], gnb_ref[:, vv])
            o_ref[rows, vv] = gated.astype(o_ref.dtype)

    @pl.when(t == pl.num_programs(1) - 1)
    def _():
        sfin_ref[0] = s_scr[...]


def _ret_prompt(qkvg, gn_g, gn_b, n_seq, seq):
    inner, qd, kd, cd = _ret_decay_tables(RET_CHUNK)
    inner = jnp.asarray(inner, F32)
    qd_b = jnp.asarray(np.broadcast_to(qd[:, :, None], (RET_HEADS, RET_CHUNK, RET_DV)), F32)
    kd_b = jnp.asarray(np.broadcast_to(kd[:, :, None], (RET_HEADS, RET_CHUNK, RET_DK)), F32)
    cdec = tuple(float(np.float32(c)) for c in cd)
    tiles = seq // RET_TILE
    vw = RET_HEADS * RET_DV
    qw = RET_HEADS * RET_DK
    return pl.pallas_call(
        functools.partial(_ret_prompt_body, cdec=cdec),
        grid=(n_seq, tiles),
        in_specs=[
            pl.BlockSpec((RET_TILE, qw), lambda n, t: (n * tiles + t, 0)),
            pl.BlockSpec((RET_TILE, qw), lambda n, t: (n * tiles + t, 1)),
            pl.BlockSpec((RET_TILE, vw), lambda n, t: (n * tiles + t, 1)),
            pl.BlockSpec((RET_TILE, vw), lambda n, t: (n * tiles + t, 2)),
            _resident((RET_HEADS, RET_CHUNK, RET_CHUNK), lambda n, t: (0, 0, 0)),
            _resident((RET_HEADS, RET_CHUNK, RET_DV), lambda n, t: (0, 0, 0)),
            _resident((RET_HEADS, RET_CHUNK, RET_DK), lambda n, t: (0, 0, 0)),
            _resident((1, vw), lambda n, t: (0, 0)),
            _resident((1, vw), lambda n, t: (0, 0)),
        ],
        out_specs=[
            pl.BlockSpec((RET_TILE, vw), lambda n, t: (n * tiles + t, 0)),
            pl.BlockSpec((1, RET_HEADS, RET_DK, RET_DV), lambda n, t: (n, 0, 0, 0)),
        ],
        out_shape=[
            jax.ShapeDtypeStruct((n_seq * seq, vw), BF16),
            jax.ShapeDtypeStruct((n_seq, RET_HEADS, RET_DK, RET_DV), F32),
        ],
        scratch_shapes=[pltpu.VMEM((RET_HEADS, RET_DK, RET_DV), F32)],
        compiler_params=_params("arbitrary", "arbitrary"),
        name="retention_prompt",
    )(qkvg, qkvg, qkvg, qkvg, inner, qd_b, kd_b, gn_g.reshape(1, vw), gn_b.reshape(1, vw))


RET_SEQ_GROUP = SUBLANES
RET_S_ROWS = T_SAMPLE * RET_SEQ_GROUP


def _ret_sample_body(q_ref, k_ref, v_ref, g_ref, s_ref, w_ref, qd_ref, kd_ref, cd_ref,
                     gng_ref, gnb_ref, o_ref, so_ref):
    q = q_ref[...].reshape(RET_S_ROWS, RET_DK).astype(BF16)
    k = k_ref[...].reshape(RET_S_ROWS, RET_DK) * (RET_DK ** -0.5)
    v = v_ref[...].reshape(RET_S_ROWS, RET_DV).astype(BF16)
    g = g_ref[...].reshape(RET_S_ROWS, RET_DV)
    scores = lax.dot_general(q, k.astype(BF16), NT_DIMS, preferred_element_type=F32) * w_ref[0]
    o_intra = jnp.dot(scores.astype(BF16), v, preferred_element_type=F32)
    kdec = k * kd_ref[0]
    seq_of_row = lax.broadcasted_iota(jnp.int32, (RET_S_ROWS, 1), 0) % RET_SEQ_GROUP
    o_inter = jnp.zeros((RET_S_ROWS, RET_DV), F32)
    for j in range(RET_SEQ_GROUP):
        mine = seq_of_row == j
        s_old = s_ref[j, 0]
        r = jnp.dot(q, s_old.astype(BF16), preferred_element_type=F32)
        o_inter = jnp.where(mine, r, o_inter)
        kj = jnp.where(mine, kdec, 0.0).astype(BF16)
        upd = lax.dot_general(kj, v, TN_DIMS, preferred_element_type=F32)
        so_ref[j, 0] = s_old * cd_ref[0] + upd
    o = o_intra + o_inter * qd_ref[0]
    gated = _group_norm_gate(o, g, gng_ref[...], gnb_ref[...])
    o_ref[...] = gated.reshape(T_SAMPLE, RET_SEQ_GROUP, RET_DV).astype(o_ref.dtype)


RET_S_STEPS = (N_SAMPLE // RET_SEQ_GROUP) * RET_HEADS


def _ret_sample_part(qkvg, state, gn_g, gn_b):
    inner, qd, kd, cd = _ret_decay_tables(T_SAMPLE)
    row_t = np.arange(RET_S_ROWS) // RET_SEQ_GROUP
    row_j = np.arange(RET_S_ROWS) % RET_SEQ_GROUP
    same = (row_j[:, None] == row_j[None, :])
    w = np.where(same[None], inner[:, row_t[:, None], row_t[None, :]], 0.0)
    w = jnp.asarray(w, F32)
    qd_b = jnp.asarray(np.broadcast_to(qd[:, row_t, None], (RET_HEADS, RET_S_ROWS, RET_DV)), F32)
    kd_b = jnp.asarray(np.broadcast_to(kd[:, row_t, None], (RET_HEADS, RET_S_ROWS, RET_DK)), F32)
    cd_b = jnp.asarray(np.broadcast_to(cd[:, None, None], (RET_HEADS, 1, RET_DV)), F32)
    vw = RET_HEADS * RET_DV
    nq = RET_HEADS
    tg = (T_SAMPLE, RET_SEQ_GROUP)
    nh = RET_HEADS
    return _Part(
        _ret_sample_body,
        (qkvg, qkvg, qkvg, qkvg, state, w, qd_b, kd_b, cd_b,
         gn_g.reshape(1, vw), gn_b.reshape(1, vw)),
        [
            pl.BlockSpec(tg + (RET_DK,), lambda i: (0, i // nh, i % nh)),
            pl.BlockSpec(tg + (RET_DK,), lambda i: (0, i // nh, nq + i % nh)),
            pl.BlockSpec(tg + (RET_DV,), lambda i: (0, i // nh, nq + i % nh)),
            pl.BlockSpec(tg + (RET_DV,), lambda i: (0, i // nh, 2 * nq + i % nh)),
            pl.BlockSpec((RET_SEQ_GROUP, 1, RET_DK, RET_DV), lambda i: (i // nh, i % nh, 0, 0)),
            pl.BlockSpec((1, RET_S_ROWS, RET_S_ROWS), lambda i: (i % nh, 0, 0)),
            pl.BlockSpec((1, RET_S_ROWS, RET_DV), lambda i: (i % nh, 0, 0)),
            pl.BlockSpec((1, RET_S_ROWS, RET_DK), lambda i: (i % nh, 0, 0)),
            pl.BlockSpec((1, 1, RET_DV), lambda i: (i % nh, 0, 0)),
            pl.BlockSpec((1, RET_DV), lambda i: (0, i % nh)),
            pl.BlockSpec((1, RET_DV), lambda i: (0, i % nh)),
        ],
        [
            jax.ShapeDtypeStruct((T_SAMPLE, N_SAMPLE, vw), BF16),
            jax.ShapeDtypeStruct(state.shape, F32),
        ],
        [
            pl.BlockSpec(tg + (RET_DV,), lambda i: (0, i // nh, i % nh)),
            pl.BlockSpec((RET_SEQ_GROUP, 1, RET_DK, RET_DV), lambda i: (i // nh, i % nh, 0, 0)),
        ])


ATT_BQ = 512
ATT_SLAB = 256
HEADS_PER_SLAB = ATT_SLAB // ATT_DH


def _band_window(cur_ref, prev_ref, s):
    if s == 0:
        return jnp.concatenate([prev_ref[0, 0], cur_ref[0, 0, 0:NBACK, :]], axis=0)
    return cur_ref[0, 0, (s - 1) * NBACK:(s + 1) * NBACK, :]


def _band_body(q_ref, kc_ref, kp_ref, vc_ref, vp_ref, bias_ref, o_ref, lse_ref, s_scr, p_scr):
    i = pl.program_id(2)
    ki = lax.broadcasted_iota(jnp.int32, (NBACK, 2 * NBACK), 1)
    has_prev = ki >= jnp.where(i > 0, 0, NBACK)
    lane_head = lax.broadcasted_iota(jnp.int32, (1, ATT_SLAB), 1) // ATT_DH
    qmasks = [jnp.where(lane_head == hh, ATT_SCALE, 0.0).astype(BF16)
              for hh in range(HEADS_PER_SLAB)]
    nsub = ATT_BQ // NBACK
    pairs = nsub * ATT_HPG
    windows = {}

    def window(cur_ref, prev_ref, s):
        key = (id(cur_ref), s)
        if key not in windows:
            windows[key] = _band_window(cur_ref, prev_ref, s)
        return windows[key]

    def place(idx):
        s, h = divmod(idx, ATT_HPG)
        sl, hh = divmod(h, HEADS_PER_SLAB)
        return s, h, hh, slice(sl * ATT_SLAB, (sl + 1) * ATT_SLAB)

    def scores(idx):
        s, _, hh, lanes = place(idx)
        q = q_ref[0, 0, s * NBACK:(s + 1) * NBACK, lanes]
        s_scr[idx] = lax.dot_general(q * qmasks[hh], window(kc_ref, kp_ref, s)[:, lanes], NT_DIMS,
                                     preferred_element_type=F32)

    def softmax(idx):
        s, h, _, _ = place(idx)
        sc = s_scr[idx] + bias_ref[h]
        if s == 0:
            sc = jnp.where(has_prev, sc, -jnp.inf)
        m = jnp.max(sc, -1, keepdims=True)
        e = jnp.exp(sc - m)
        den = jnp.sum(e, -1, keepdims=True)
        p_scr[idx] = (e / den).astype(BF16)
        lse_ref[0, 0, s * NBACK:(s + 1) * NBACK, h * LSE_REP:(h + 1) * LSE_REP] = jnp.broadcast_to(
            m + jnp.log(den), (NBACK, LSE_REP))

    def values(idx):
        s, h, hh, lanes = place(idx)
        oh = jnp.dot(p_scr[idx], window(vc_ref, vp_ref, s)[:, lanes], preferred_element_type=F32)
        o_ref[0, 0, s * NBACK:(s + 1) * NBACK, h * ATT_DH:(h + 1) * ATT_DH] = (
            oh[:, hh * ATT_DH:(hh + 1) * ATT_DH].astype(o_ref.dtype))

    for stage in (scores, softmax, values):
        for idx in range(pairs):
            stage(idx)


def _band_bias(group):
    _, dil = ATT_GROUPS[group]
    steps = NBACK + np.arange(NBACK)[:, None] - np.arange(2 * NBACK)[None, :]
    valid = (steps >= 0) & (steps <= NBACK)
    slopes = _alibi_slopes()[group]
    bias = -(slopes[:, None, None] * dil) * steps[None]
    return np.where(valid[None], bias, -np.inf)


def _band_attention(qkv_cls, group):
    _, dil = ATT_GROUPS[group]
    n_seq, _, length, _ = qkv_cls.shape
    nsub = ATT_BQ // NBACK
    blocks = length // ATT_BQ
    pairs = nsub * ATT_HPG
    prev = lambda i: jnp.maximum(i * nsub - 1, 0)
    cur = lambda col: pl.BlockSpec((1, 1, ATT_BQ, ATT_W), lambda n, r, i: (n, r, i, col))
    halo = lambda col: pl.BlockSpec((1, 1, NBACK, ATT_W), lambda n, r, i: (n, r, prev(i), col))
    out_sds = jax.ShapeDtypeStruct((n_seq, dil, length, ATT_W), BF16)
    return pl.pallas_call(
        _band_body,
        grid=(n_seq, dil, blocks),
        in_specs=[cur(0), cur(1), halo(1), cur(2), halo(2),
                  _resident((ATT_HPG, NBACK, 2 * NBACK), lambda n, r, i: (0, 0, 0))],
        out_specs=[cur(0), pl.BlockSpec((1, 1, ATT_BQ, LANES), lambda n, r, i: (n, r, i, 0))],
        out_shape=[out_sds, jax.ShapeDtypeStruct((n_seq, dil, length, LANES), F32)],
        scratch_shapes=[pltpu.VMEM((pairs, NBACK, 2 * NBACK), F32),
                        pltpu.VMEM((pairs, NBACK, 2 * NBACK), BF16)],
        compiler_params=_params("arbitrary", "arbitrary", "arbitrary"),
        name="band_attention",
    )(qkv_cls, qkv_cls, qkv_cls, qkv_cls, qkv_cls, jnp.asarray(_band_bias(group), F32))


SA_PASSES = tuple((0, (t,)) for t in range(T_SAMPLE)) + ((1, tuple(range(T_SAMPLE))),
                                                        (2, tuple(range(T_SAMPLE))))
SA_MAX_LB = max(w for w, _ in ATT_GROUPS)


def _att_sample_tables():
    slopes = _alibi_slopes()
    sels, biases = [], []
    bias_new = np.full((N_GROUPS, T_SAMPLE * ATT_HPG, LANES), -np.inf)
    cs = np.zeros((len(SA_PASSES), SUBLANES, LANES))
    lane = np.arange(LANES)
    for g, (lb, dil) in enumerate(ATT_GROUPS):
        passes = [p for p in SA_PASSES if p[0] == g]
        sel = np.zeros((ATT_HPG, T_SAMPLE, len(passes), ATT_HPG, LANES))
        for pi, (_, t_set) in enumerate(passes):
            for t in t_set:
                owns = np.ones(LANES, bool) if dil == 1 else (lane % dil == t)
                for h in range(ATT_HPG):
                    sel[h, t, pi, h, owns] = 1.0
        sels.append(sel.reshape(ATT_HPG * T_SAMPLE, len(passes) * ATT_HPG * LANES))
        pos = np.arange(lb)
        bias = np.full((T_SAMPLE, ATT_HPG, lb), -np.inf)
        for t in range(T_SAMPLE):
            if dil == 1:
                j = lb + t - pos
                valid = j <= NBACK
            else:
                j = NBACK - pos // dil
                valid = (pos % dil) == t
            bias[t] = np.where(valid[None, :], -(slopes[g][:, None] * dil) * j[None, :], -np.inf)
            for h in range(ATT_HPG):
                for t2 in range(T_SAMPLE):
                    if (dil == 1 and t2 <= t) or t2 == t:
                        bias_new[g, t * ATT_HPG + h, t2] = -slopes[g][h] * dil * (t - t2)
        biases.append(bias)
    for pi, (g, t_set) in enumerate(SA_PASSES):
        dil = ATT_GROUPS[g][1]
        for t in t_set:
            cs[pi, t] = 1.0 if dil == 1 else (lane % dil == t)
    bd = (np.arange(ATT_W)[None, :] // ATT_DH == np.arange(ATT_HPG)[:, None]).astype(np.float64)
    return sels, biases, bias_new, cs, bd


SA_SEQ_PER_STEP = 2


def _sa_new_row(y_ref, j, t, col):
    row = (pl.program_id(0) % (SUBLANES // SA_SEQ_PER_STEP)) * SA_SEQ_PER_STEP + j
    lane0 = (t * 3 * N_GROUPS + col) * ATT_W
    return y_ref[pl.ds(row, 1), lane0:lane0 + ATT_W]


def _sa_new_rows(y_ref, j, col, n_rows):
    rows = [_sa_new_row(y_ref, j, t, col) for t in range(T_SAMPLE)]
    return jnp.concatenate(rows + [jnp.zeros((n_rows - T_SAMPLE, ATT_W), F32)], 0)


def _att_scores_body(qd_ref, y_ref, c0_ref, c1_ref, c2_ref,
                     sel0_ref, sel1_ref, sel2_ref, b0_ref, b1_ref, b2_ref, bn_ref, bd_ref,
                     p0_ref, p1_ref, p2_ref, pn_ref, s_scr, e_scr):
    caches = (c0_ref, c1_ref, c2_ref)
    sels = (sel0_ref, sel1_ref, sel2_ref)
    biases = (b0_ref, b1_ref, b2_ref)
    p_refs = (p0_ref, p1_ref, p2_ref)
    bd = bd_ref[...]
    qsels, s_news = {}, {}
    for j in range(SA_SEQ_PER_STEP):
        for g in range(N_GROUPS):
            qsels[j, g] = jnp.dot(qd_ref[j, g].astype(BF16), sels[g][...],
                                  preferred_element_type=F32)
            qbd = jnp.concatenate(
                [jnp.broadcast_to(_sa_new_row(y_ref, j, t, g), (ATT_HPG, ATT_W)) * bd
                 for t in range(T_SAMPLE)], 0)
            kn = _sa_new_rows(y_ref, j, N_GROUPS + g, LANES)
            s_news[j, g] = lax.dot_general(qbd.astype(BF16), kn.astype(BF16), NT_DIMS,
                                           preferred_element_type=F32) * ATT_SCALE + bn_ref[g]
    for j in range(SA_SEQ_PER_STEP):
        e_new, dens, lses = {}, {}, {}
        pid = 0
        for g, (lb, dil) in enumerate(ATT_GROUPS):
            chunks = lb // LANES
            passes = [p for p in SA_PASSES if p[0] == g]
            qsel, s_new = qsels[j, g], s_news[j, g]
            pass_of = {}
            for local, (_, t_set) in enumerate(passes):
                for h in range(ATT_HPG):
                    col = (local * ATT_HPG + h) * LANES
                    qs = qsel[:, col:col + LANES]
                    for c in range(chunks):
                        lanes = slice(c * LANES, (c + 1) * LANES)
                        prod = caches[g][j, 0, h, :, lanes] * qs
                        part = prod.reshape(ATT_DH // SUBLANES, SUBLANES, LANES).sum(0)
                        s_scr[pid, h:h + 1, lanes] = part.sum(0, keepdims=True)
                for t in t_set:
                    pass_of[t] = pid
                pid += 1
            for t in range(T_SAMPLE):
                sc = s_scr[pass_of[t], :, :lb] * ATT_SCALE + biases[g][t]
                sn = s_new[t * ATT_HPG:(t + 1) * ATT_HPG]
                m = jnp.maximum(jnp.max(sc, -1, keepdims=True), jnp.max(sn, -1, keepdims=True))
                e = jnp.exp(sc - m)
                en = jnp.exp(sn - m)
                den = jnp.sum(e, -1, keepdims=True) + jnp.sum(en, -1, keepdims=True)
                e_scr[g * T_SAMPLE + t, :, :lb] = e
                e_new[g, t], dens[g, t], lses[g, t] = en, den, m + jnp.log(den)
        wts = {}
        for t in range(T_SAMPLE):
            ls = [lses[g, t] for g in range(N_GROUPS)]
            m = jnp.maximum(jnp.maximum(ls[0], ls[1]), ls[2])
            ws = [jnp.exp(l - m) for l in ls]
            tot = ws[0] + ws[1] + ws[2]
            for g in range(N_GROUPS):
                wts[g, t] = ws[g] / (tot * dens[g, t])
        for g, (lb, dil) in enumerate(ATT_GROUPS):
            pn_ref[j, g] = jnp.concatenate(
                [e_new[g, t] * wts[g, t] for t in range(T_SAMPLE)], 0)
            for local, (_, t_set) in enumerate([p for p in SA_PASSES if p[0] == g]):
                pc = None
                for t in t_set:
                    term = e_scr[g * T_SAMPLE + t, :, :lb] * wts[g, t]
                    pc = term if pc is None else pc + term
                p_refs[g][j, local] = pc


def _att_values_body(p0_ref, p1_ref, p2_ref, pn_ref, y_ref, c0_ref, c1_ref, c2_ref,
                     cs_ref, bd_ref, o_ref):
    caches = (c0_ref, c1_ref, c2_ref)
    p_refs = (p0_ref, p1_ref, p2_ref)
    bd = bd_ref[...]
    wfs = {}
    for j in range(SA_SEQ_PER_STEP):
        for pid, (g, _) in enumerate(SA_PASSES):
            local = pid - [p[0] for p in SA_PASSES].index(g)
            folded = []
            for h in range(ATT_HPG):
                acc = None
                for c in range(ATT_GROUPS[g][0] // LANES):
                    lanes = slice(c * LANES, (c + 1) * LANES)
                    term = caches[g][j, 0, h, :, lanes] * p_refs[g][j, local, h:h + 1, lanes]
                    acc = term if acc is None else acc + term
                folded.append(acc)
            wfs[j, pid] = jnp.concatenate(folded, 0).astype(BF16)
    for j in range(SA_SEQ_PER_STEP):
        o = jnp.zeros((SUBLANES, ATT_W), F32)
        o_new = jnp.zeros((T_SAMPLE * ATT_HPG, ATT_W), F32)
        for g in range(N_GROUPS):
            vn = _sa_new_rows(y_ref, j, 2 * N_GROUPS + g, LANES)
            o_new = o_new + jnp.dot(pn_ref[j, g].astype(BF16), vn.astype(BF16),
                                    preferred_element_type=F32)
        for pid in range(len(SA_PASSES)):
            o = o + lax.dot_general(cs_ref[pid].astype(BF16), wfs[j, pid], NT_DIMS,
                                    preferred_element_type=F32)
        extra = [jnp.sum(o_new[t * ATT_HPG:(t + 1) * ATT_HPG] * bd, 0, keepdims=True)
                 for t in range(T_SAMPLE)]
        o_ref[j] = o[:T_SAMPLE] + jnp.concatenate(extra, 0)


def _sa_new_spec():
    per_block = SUBLANES // SA_SEQ_PER_STEP
    return pl.BlockSpec((SUBLANES, T_SAMPLE * 3 * N_GROUPS * ATT_W), lambda i: (i // per_block, 0))


def _sa_cache_specs(caches, kv):
    views = [c.transpose(0, 2, 3, 4, 1) for c in caches]
    specs = [pl.BlockSpec((SA_SEQ_PER_STEP, 1, ATT_HPG, ATT_DH, lb), lambda i: (i, kv, 0, 0, 0))
             for lb, _ in ATT_GROUPS]
    return views, specs


def _sa_const_specs(tables):
    return [_resident(t.shape, lambda i, nd=t.ndim: (0,) * nd) for t in tables]


def _sa_prob_layout():
    shapes, specs = [], []
    for g, (lb, _) in enumerate(ATT_GROUPS):
        n_pass = len([p for p in SA_PASSES if p[0] == g])
        shapes.append(jax.ShapeDtypeStruct((N_SAMPLE, n_pass, ATT_HPG, lb), F32))
        specs.append(pl.BlockSpec((SA_SEQ_PER_STEP, n_pass, ATT_HPG, lb), lambda i: (i, 0, 0, 0)))
    shapes.append(jax.ShapeDtypeStruct((N_SAMPLE, N_GROUPS, T_SAMPLE * ATT_HPG, LANES), F32))
    specs.append(pl.BlockSpec((SA_SEQ_PER_STEP, N_GROUPS, T_SAMPLE * ATT_HPG, LANES),
                              lambda i: (i, 0, 0, 0)))
    return shapes, specs


def _att_scores_part(y, caches):
    sels, biases, bias_new, _, bd = _att_sample_tables()
    q = y.reshape(N_SAMPLE, T_SAMPLE, 3 * N_GROUPS, ATT_HPG, ATT_DH)[:, :, :N_GROUPS]
    qd = q.transpose(0, 2, 4, 3, 1).reshape(N_SAMPLE, N_GROUPS, ATT_DH, ATT_HPG * T_SAMPLE)
    views, cache_specs = _sa_cache_specs(caches, 0)
    tables = ([jnp.asarray(s, BF16) for s in sels] + [jnp.asarray(b, F32) for b in biases]
              + [jnp.asarray(bias_new, F32), jnp.asarray(bd, F32)])
    out_shape, out_specs = _sa_prob_layout()
    return _Part(
        _att_scores_body,
        (qd, y, *views, *tables),
        [pl.BlockSpec((SA_SEQ_PER_STEP, N_GROUPS, ATT_DH, ATT_HPG * T_SAMPLE),
                      lambda i: (i, 0, 0, 0)), _sa_new_spec()]
        + cache_specs + _sa_const_specs(tables),
        out_shape, out_specs,
        [pltpu.VMEM((len(SA_PASSES), ATT_HPG, SA_MAX_LB), F32),
         pltpu.VMEM((N_GROUPS * T_SAMPLE, ATT_HPG, SA_MAX_LB), F32)])


def _att_values_part(probs, y, caches):
    _, _, _, cs, bd = _att_sample_tables()
    views, cache_specs = _sa_cache_specs(caches, 1)
    tables = [jnp.asarray(cs, F32), jnp.asarray(bd, F32)]
    _, prob_specs = _sa_prob_layout()
    return _Part(
        _att_values_body,
        (*probs, y, *views, *tables),
        prob_specs + [_sa_new_spec()] + cache_specs + _sa_const_specs(tables),
        [jax.ShapeDtypeStruct((N_SAMPLE, T_SAMPLE, ATT_W), F32)],
        [pl.BlockSpec((SA_SEQ_PER_STEP, T_SAMPLE, ATT_W), lambda i: (i, 0, 0))])


def kernel(x_prompt, x_sample, c_prompt, c_sample, state_ret, cache_kv_w128, cache_kv_w512,
           cache_kv_w2048, w_ada, b_ada, ln_g, ln_b, w_ffn_in, w_ffn_out, w_ret_in, ret_gn_g,
           ret_gn_b, w_ret_out, w_att_in, w_att_out):
    d = D_MODEL
    n_p, seq, _ = x_prompt.shape
    w_ffn_in_b = {(0, 0): w_ffn_in[0, 0].astype(BF16)}
    w_ffn_out_b = {(0, 0): w_ffn_out[0, 0].astype(BF16)}
    w_ret_in_b = w_ret_in.astype(BF16)
    later_ffn = [(0, 1), (1, 0), (1, 1)]
    cast_items = ([(w_ffn_in, lw) for lw in later_ffn] + [(w_ffn_out, lw) for lw in later_ffn]
                  + [(w_ret_out, ()), (w_att_in, ()), (w_att_out, ())])

    c_all = jnp.concatenate(
        [c_sample, c_prompt, jnp.zeros((ADA_ROWS - N_SAMPLE - n_p, d), F32)], axis=0)
    ada = _ada_table(c_all, w_ada, b_ada)

    prompt = _Trunk(n_p * seq, 512, False, seq // 512)
    prompt_wide = _Trunk(n_p * seq, 1024, False, seq // 1024)
    sample = _Trunk(N_SAMPLE * T_SAMPLE, N_SAMPLE, True, 1)
    xp = x_prompt.reshape(n_p * seq, d)
    xs = x_sample.transpose(1, 0, 2).reshape(T_SAMPLE * N_SAMPLE, d)

    steps = RET_S_STEPS
    tile_s = n_p * seq // steps
    prompt_s = _Trunk(n_p * seq, tile_s, False, seq // tile_s)

    def ffn_part(x, trunk, layer, which):
        sub = 0 if which == 0 else 2
        return _ffn_part(x, ada, w_ffn_in_b[layer, which], w_ffn_out_b[layer, which],
                         ln_g[layer, sub], ln_b[layer, sub], trunk, layer, sub)

    def ffn(x, trunk, layer, which):
        return _run([ffn_part(x, trunk, layer, which)], trunk.tokens // trunk.tile, "ffn")[0][0]

    xs = ffn(xs, sample, 0, 0)
    qkvg_s = _inproj(xs, ada, w_ret_in_b, sample, 0, F32)

    (xp,), (gated_s, ret_s), cast = _run(
        [ffn_part(xp, prompt_s, 0, 0),
         _ret_sample_part(qkvg_s.reshape(T_SAMPLE, N_SAMPLE, -1), state_ret, ret_gn_g, ret_gn_b),
         _cast_part(cast_items, steps)],
        steps, "ffn_ret_sample")
    for k, lw in enumerate(later_ffn):
        w_ffn_in_b[lw], w_ffn_out_b[lw] = cast[k], cast[len(later_ffn) + k]
    w_ret_out_b, w_att_in_b, w_att_out_b = cast[2 * len(later_ffn):]

    xs = _outproj(gated_s.reshape(T_SAMPLE * N_SAMPLE, -1), xs, ada, w_ret_out_b,
                  ln_g[0, 1], ln_b[0, 1], sample, 0)
    xs = ffn(xs, sample, 0, 1)
    xs = ffn(xs, sample, 1, 0)
    qkv_s = _inproj(xs, ada, w_att_in_b, sample, 1, F32, tiles_to_columns=True)
    caches = (cache_kv_w128, cache_kv_w512, cache_kv_w2048)

    qkvg_p = _inproj(xp, ada, w_ret_in_b, prompt, 0, BF16)
    gated_p, ret_p = _ret_prompt(qkvg_p, ret_gn_g, ret_gn_b, n_p, seq)
    xp = _outproj(gated_p, xp, ada, w_ret_out_b, ln_g[0, 1], ln_b[0, 1], prompt_wide, 0)

    probs, (xp,) = _run([_att_scores_part(qkv_s, caches), ffn_part(xp, prompt_s, 0, 1)],
                        steps, "att_scores_ffn")
    (xp,), (att_s,) = _run([ffn_part(xp, prompt_s, 1, 0), _att_values_part(probs, qkv_s, caches)],
                           steps, "ffn_att_values")

    keep_p = [min(w, seq) for w, _ in ATT_GROUPS]
    *qkv_cls, kv_tail = _att_inproj_prompt(xp, ada, w_att_in_b, prompt, 1, max(keep_p))
    band = [_band_attention(qkv_cls[g], g) for g in range(N_GROUPS)]
    xp = _att_outproj_prompt([o for o, _ in band], [l for _, l in band], xp, ada, w_att_out_b,
                             ln_g[1, 1], ln_b[1, 1], prompt_wide, 1)
    kv_tail = kv_tail.reshape(n_p, max(keep_p), 2, N_GROUPS, ATT_HPG, ATT_DH)
    kv_p = [kv_tail[:, max(keep_p) - keep_p[g]:, :, g] for g in range(N_GROUPS)]

    att_s = att_s.transpose(1, 0, 2).reshape(T_SAMPLE * N_SAMPLE, ATT_W)
    xs = _outproj(att_s, xs, ada, w_att_out_b, ln_g[1, 1], ln_b[1, 1], sample, 1)
    new_s = qkv_s.reshape(N_SAMPLE, T_SAMPLE, 3, N_GROUPS, ATT_HPG, ATT_DH)
    kv_s = [new_s[:, :, 1:, g] for g in range(N_GROUPS)]

    xp = ffn(xp, prompt, 1, 1)
    xs = ffn(xs, sample, 1, 1)

    y_prompt = xp.reshape(n_p, seq, d)
    y_sample = xs.reshape(T_SAMPLE, N_SAMPLE, d).transpose(1, 0, 2)
    return (y_prompt, y_sample, ret_p, ret_s, kv_p[0], kv_s[0], kv_p[1], kv_s[1], kv_p[2], kv_s[2])
```

```python
import functools

import numpy as np
import jax
import jax.numpy as jnp
from jax import lax
from jax.experimental import pallas as pl
from jax.experimental.pallas import tpu as pltpu

F32 = jnp.float32
BF16 = jnp.bfloat16

D_MODEL = 1024
DEPTH = 2
D_FF = 2816
RET_HEADS = 4
RET_DK = 256
RET_DV = 512
RET_CHUNK = 128
ATT_GROUPS = ((128, 1), (512, 4), (2048, 16))
N_GROUPS = 3
ATT_HPG = 8
ATT_DH = 64
ATT_W = ATT_HPG * ATT_DH
NBACK = 128
LSE_REP = 128 // ATT_HPG
ALPHA = (2 * DEPTH) ** 0.25
NORM_EPS = 1e-5
ATT_SCALE = ATT_DH ** -0.5

V7X_VMEM_LIMIT_BYTES = 60000 * 1024
LANES = 128
SUBLANES = 8

N_SAMPLE = 128
T_SAMPLE = 4
ADA_ROWS = 136
PROMPT_ROW_BLOCK = N_SAMPLE // SUBLANES

NT_DIMS = (((1,), (1,)), ((), ()))
TN_DIMS = (((0,), (0,)), ((), ()))


def _params(*semantics):
    return pltpu.CompilerParams(dimension_semantics=semantics,
                                vmem_limit_bytes=V7X_VMEM_LIMIT_BYTES)


def _resident(block, index_map):
    return pl.BlockSpec(block, index_map, pipeline_mode=pl.Buffered(1))


def _layer_norm(z, g, b):
    mu = jnp.mean(z, -1, keepdims=True)
    zc = z - mu
    var = jnp.mean(zc * zc, -1, keepdims=True)
    return zc * lax.rsqrt(var + NORM_EPS) * g + b


def _log_gamma():
    return np.log1p(-(2.0 ** (-5.0 - np.arange(RET_HEADS, dtype=np.float64))))


def _alibi_slopes():
    h = np.arange(1, N_GROUPS * ATT_HPG + 1, dtype=np.float64)
    return (2.0 ** (-8.0 * h / (N_GROUPS * ATT_HPG))).reshape(N_GROUPS, ATT_HPG)


ADA_SLABS = 3


def _ada_body(c_ref, w_ref, b_ref, o_ref):
    s = jax.nn.silu(c_ref[...]).astype(BF16)
    for k in range(ADA_SLABS):
        w = w_ref[0, :, k * D_MODEL:(k + 1) * D_MODEL].astype(BF16)
        o_ref[0, k] = jnp.dot(s, w, preferred_element_type=F32) + b_ref[0, k]


def _ada_table(c_all, w_ada, b_ada):
    d = D_MODEL
    return pl.pallas_call(
        _ada_body,
        grid=(DEPTH, 9 // ADA_SLABS),
        in_specs=[
            _resident((ADA_ROWS, d), lambda i, j: (0, 0)),
            pl.BlockSpec((1, d, ADA_SLABS * d), lambda i, j: (i, 0, j)),
            pl.BlockSpec((1, ADA_SLABS, 1, d), lambda i, j: (i, j, 0, 0)),
        ],
        out_specs=pl.BlockSpec((1, ADA_SLABS, ADA_ROWS, d), lambda i, j: (i, j, 0, 0)),
        out_shape=jax.ShapeDtypeStruct((DEPTH, 9, ADA_ROWS, d), F32),
        compiler_params=_params("arbitrary", "arbitrary"),
        name="ada_table",
    )(c_all, w_ada, b_ada.reshape(DEPTH, 9, 1, d))


class _Trunk:
    def __init__(self, tokens, tile, per_row, tiles_per_seq):
        self.tokens = tokens
        self.tile = tile
        self.per_row = per_row
        self.tiles_per_seq = tiles_per_seq

    def mod_spec(self, layer, sub):
        if self.per_row:
            return pl.BlockSpec((1, 3, self.tile, D_MODEL), lambda i: (layer, sub, 0, 0))
        return pl.BlockSpec((1, 3, SUBLANES, D_MODEL),
                            lambda i: (layer, sub, PROMPT_ROW_BLOCK, 0))

    def mod_rows(self, mod_ref, k):
        if self.per_row:
            return mod_ref[0, k]
        n = pl.program_id(0) // self.tiles_per_seq
        return mod_ref[0, k, pl.ds(n, 1), :]


def _ffn_body(x_ref, mod_ref, win_ref, wout_ref, g_ref, b_ref, o_ref, *, trunk):
    x = x_ref[...]
    shift = trunk.mod_rows(mod_ref, 0)
    scale = trunk.mod_rows(mod_ref, 1)
    gate = trunk.mod_rows(mod_ref, 2)
    u = (x * (1.0 + scale) + shift).astype(BF16)
    h = jnp.dot(u, win_ref[...], preferred_element_type=F32)
    a = h[:, :D_FF]
    b = h[:, D_FF:]
    act = (jax.nn.silu(a) * b).astype(BF16)
    y = jnp.dot(act, wout_ref[...], preferred_element_type=F32)
    z = ALPHA * x + (0.5 * (1.0 + gate)) * y
    o_ref[...] = _layer_norm(z, g_ref[...], b_ref[...])


class _Part:
    def __init__(self, body, args, in_specs, out_shape, out_specs, scratch=()):
        self.body, self.args, self.in_specs = body, list(args), list(in_specs)
        self.out_shape, self.out_specs, self.scratch = list(out_shape), list(out_specs), list(scratch)


def _run(parts, steps, name):
    n_in = [len(p.args) for p in parts]
    n_out = [len(p.out_shape) for p in parts]
    n_scr = [len(p.scratch) for p in parts]

    def body(*refs):
        ins, outs, scr = refs[:sum(n_in)], refs[sum(n_in):sum(n_in) + sum(n_out)], refs[sum(n_in) + sum(n_out):]
        a = b = c = 0
        for k, p in enumerate(parts):
            p.body(*ins[a:a + n_in[k]], *outs[b:b + n_out[k]], *scr[c:c + n_scr[k]])
            a, b, c = a + n_in[k], b + n_out[k], c + n_scr[k]

    res = pl.pallas_call(
        body,
        grid=(steps,),
        in_specs=[s for p in parts for s in p.in_specs],
        out_specs=[s for p in parts for s in p.out_specs],
        out_shape=[s for p in parts for s in p.out_shape],
        scratch_shapes=[s for p in parts for s in p.scratch],
        compiler_params=_params("arbitrary"),
        name=name,
    )(*[a for p in parts for a in p.args])
    out, b = [], 0
    for k in range(len(parts)):
        out.append(res[b:b + n_out[k]])
        b += n_out[k]
    return out


def _cast_body(*refs):
    n = len(refs) // 2
    for src, dst in zip(refs[:n], refs[n:]):
        dst[...] = src[...].reshape(dst.shape).astype(dst.dtype)


def _cast_part(items, steps):
    args, in_specs, out_shape, out_specs = [], [], [], []
    for arr, lead in items:
        rows, cols = arr.shape[-2:]
        n_blocks = steps
        while rows % n_blocks or (rows // n_blocks) % (2 * SUBLANES):
            n_blocks //= 2
        blk = rows // n_blocks
        args.append(arr)
        in_specs.append(pl.BlockSpec(
            (1,) * len(lead) + (blk, cols),
            lambda i, lead=lead, last=n_blocks - 1: lead + (jnp.minimum(i, last), 0)))
        out_shape.append(jax.ShapeDtypeStruct((rows, cols), BF16))
        out_specs.append(pl.BlockSpec((blk, cols),
                                      lambda i, last=n_blocks - 1: (jnp.minimum(i, last), 0)))
    return _Part(_cast_body, args, in_specs, out_shape, out_specs)


def _ffn_part(x, ada, w_in, w_out, ln_g, ln_b, trunk, layer, sub):
    d = D_MODEL
    tm = trunk.tile
    return _Part(
        functools.partial(_ffn_body, trunk=trunk),
        (x, ada, w_in, w_out, ln_g.reshape(1, d), ln_b.reshape(1, d)),
        [
            pl.BlockSpec((tm, d), lambda i: (i, 0)),
            trunk.mod_spec(layer, sub),
            _resident((d, 2 * D_FF), lambda i: (0, 0)),
            _resident((D_FF, d), lambda i: (0, 0)),
            _resident((1, d), lambda i: (0, 0)),
            _resident((1, d), lambda i: (0, 0)),
        ],
        [jax.ShapeDtypeStruct((trunk.tokens, d), F32)],
        [pl.BlockSpec((tm, d), lambda i: (i, 0))])


def _inproj_body(x_ref, mod_ref, w_ref, o_ref, *, trunk):
    x = x_ref[...]
    u = (x * (1.0 + trunk.mod_rows(mod_ref, 1)) + trunk.mod_rows(mod_ref, 0)).astype(BF16)
    o_ref[...] = jnp.dot(u, w_ref[...], preferred_element_type=F32).astype(o_ref.dtype)


def _inproj(x, ada, w, trunk, layer, out_dtype, tiles_to_columns=False):
    d = D_MODEL
    tm = trunk.tile
    n_out = w.shape[1]
    tiles = trunk.tokens // tm
    if tiles_to_columns:
        out_spec = pl.BlockSpec((tm, n_out), lambda i: (0, i))
        out_shape = jax.ShapeDtypeStruct((tm, tiles * n_out), out_dtype)
    else:
        out_spec = pl.BlockSpec((tm, n_out), lambda i: (i, 0))
        out_shape = jax.ShapeDtypeStruct((trunk.tokens, n_out), out_dtype)
    return pl.pallas_call(
        functools.partial(_inproj_body, trunk=trunk),
        grid=(tiles,),
        in_specs=[
            pl.BlockSpec((tm, d), lambda i: (i, 0)),
            trunk.mod_spec(layer, 1),
            _resident((d, n_out), lambda i: (0, 0)),
        ],
        out_specs=out_spec,
        out_shape=out_shape,
        compiler_params=_params("arbitrary"),
        name="inproj",
    )(x, ada, w)


def _att_inproj_body(x_ref, mod_ref, w_ref, a0_ref, a1_ref, a2_ref, tail_ref, y_scr, *, trunk):
    x = x_ref[...]
    u = (x * (1.0 + trunk.mod_rows(mod_ref, 1)) + trunk.mod_rows(mod_ref, 0)).astype(BF16)
    slabs = ATT_W // LANES
    a_refs = (a0_ref, a1_ref, a2_ref)
    for part in range(3):
        for g in range(N_GROUPS):
            col0 = (part * N_GROUPS + g) * ATT_W
            for c in range(ATT_W // ATT_SLAB):
                lo = col0 + c * ATT_SLAB
                y = jnp.dot(u, w_ref[:, lo:lo + ATT_SLAB], preferred_element_type=F32)
                if part > 0:
                    tail_ref[0, :, lo - N_GROUPS * ATT_W:lo - N_GROUPS * ATT_W + ATT_SLAB] = y
                for half in range(ATT_SLAB // LANES):
                    y_scr[lo // LANES + half] = y[:, half * LANES:(half + 1) * LANES]
            dil = ATT_GROUPS[g][1]
            rows = trunk.tile // dil
            for r in range(dil):
                sel = pl.ds(r, rows, stride=dil) if dil > 1 else slice(None)
                for cb in range(slabs):
                    dst = (part * slabs + cb) * LANES
                    a_refs[g][0, r, :, dst:dst + LANES] = (
                        y_scr[col0 // LANES + cb, sel, :].astype(BF16))


def _att_inproj_prompt(x, ada, w, trunk, layer, tail_rows):
    d = D_MODEL
    tm = trunk.tile
    n_out = w.shape[1]
    tps = trunk.tiles_per_seq
    n_seq = trunk.tokens // (tps * tm)
    first = tps - tail_rows // tm
    out_shape, out_specs = [], []
    for _, dil in ATT_GROUPS:
        out_shape.append(jax.ShapeDtypeStruct((n_seq, dil, tps * tm // dil, 3 * ATT_W), BF16))
        out_specs.append(pl.BlockSpec((1, dil, tm // dil, 3 * ATT_W),
                                      lambda i: (i // tps, 0, i % tps, 0)))
    kv_w = n_out - N_GROUPS * ATT_W
    out_shape.append(jax.ShapeDtypeStruct((n_seq, tail_rows, kv_w), F32))
    out_specs.append(pl.BlockSpec((1, tm, kv_w),
                                  lambda i: (i // tps, jnp.maximum(i % tps - first, 0), 0)))
    return pl.pallas_call(
        functools.partial(_att_inproj_body, trunk=trunk),
        grid=(trunk.tokens // tm,),
        in_specs=[
            pl.BlockSpec((tm, d), lambda i: (i, 0)),
            trunk.mod_spec(layer, 1),
            _resident((d, n_out), lambda i: (0, 0)),
        ],
        out_specs=out_specs,
        out_shape=out_shape,
        scratch_shapes=[pltpu.VMEM((n_out // LANES, tm, LANES), F32)],
        compiler_params=_params("arbitrary"),
        name="att_inproj",
    )(x, ada, w)


def _outproj_tail(a, x_ref, mod_ref, w_ref, g_ref, b_ref, o_ref, trunk):
    x = x_ref[...]
    gate = trunk.mod_rows(mod_ref, 2)
    y = jnp.dot(a.astype(BF16), w_ref[...], preferred_element_type=F32)
    z = ALPHA * x + (1.0 + gate) * y
    o_ref[...] = _layer_norm(z, g_ref[...], b_ref[...])


def _outproj_body(a_ref, x_ref, mod_ref, w_ref, g_ref, b_ref, o_ref, *, trunk):
    _outproj_tail(a_ref[...], x_ref, mod_ref, w_ref, g_ref, b_ref, o_ref, trunk)


def _outproj(a, x, ada, w, ln_g, ln_b, trunk, layer):
    d = D_MODEL
    tm = trunk.tile
    k_in = w.shape[0]
    return pl.pallas_call(
        functools.partial(_outproj_body, trunk=trunk),
        grid=(trunk.tokens // tm,),
        in_specs=[
            pl.BlockSpec((tm, k_in), lambda i: (i, 0)),
            pl.BlockSpec((tm, d), lambda i: (i, 0)),
            trunk.mod_spec(layer, 1),
            _resident((k_in, d), lambda i: (0, 0)),
            _resident((1, d), lambda i: (0, 0)),
            _resident((1, d), lambda i: (0, 0)),
        ],
        out_specs=pl.BlockSpec((tm, d), lambda i: (i, 0)),
        out_shape=jax.ShapeDtypeStruct((trunk.tokens, d), F32),
        compiler_params=_params("arbitrary"),
        name="outproj",
    )(a, x, ada, w, ln_g.reshape(1, d), ln_b.reshape(1, d))


def _att_outproj_body(o0_ref, o1_ref, o2_ref, l0_ref, l1_ref, l2_ref, e_ref, x_ref, mod_ref, w_ref,
                      g_ref, b_ref, out_ref, tok_scr, *, trunk):
    def token_major(ref, g, first_slab):
        dil = ATT_GROUPS[g][1]
        slabs = ref.shape[-1] // LANES
        if dil == 1:
            return ref[0, 0].astype(F32)
        rows = trunk.tile // dil
        for r in range(dil):
            for cb in range(slabs):
                tok_scr[first_slab + cb, pl.ds(r, rows, stride=dil), :] = (
                    ref[0, r, :, cb * LANES:(cb + 1) * LANES].astype(F32))
        return jnp.concatenate([tok_scr[first_slab + cb] for cb in range(slabs)], axis=1)

    o_slabs = ATT_W // LANES
    os = [token_major(r, g, g * o_slabs) for g, r in enumerate((o0_ref, o1_ref, o2_ref))]
    ls = [token_major(r, g, N_GROUPS * o_slabs + g)
          for g, r in enumerate((l0_ref, l1_ref, l2_ref))]
    m = jnp.maximum(jnp.maximum(ls[0], ls[1]), ls[2])
    ws = [jnp.exp(l - m) for l in ls]
    inv = 1.0 / (ws[0] + ws[1] + ws[2])
    a = None
    for g in range(N_GROUPS):
        alpha = ws[g] * inv
        hi = alpha.astype(BF16)
        lo = (alpha - hi.astype(F32)).astype(BF16)
        spread = (jnp.dot(hi, e_ref[...], preferred_element_type=F32)
                  + jnp.dot(lo, e_ref[...], preferred_element_type=F32))
        term = spread * os[g]
        a = term if a is None else a + term
    _outproj_tail(a, x_ref, mod_ref, w_ref, g_ref, b_ref, out_ref, trunk)


def _att_outproj_prompt(outs, lses, x, ada, w, ln_g, ln_b, trunk, layer):
    d = D_MODEL
    tm = trunk.tile
    tps = trunk.tiles_per_seq
    k_in = w.shape[0]
    cls_specs = lambda width: [
        pl.BlockSpec((1, dil, tm // dil, width), lambda i: (i // tps, 0, i % tps, 0))
        for _, dil in ATT_GROUPS]
    e = (np.arange(LANES)[:, None] // LSE_REP == np.arange(ATT_W)[None, :] // ATT_DH) / LSE_REP
    return pl.pallas_call(
        functools.partial(_att_outproj_body, trunk=trunk),
        grid=(trunk.tokens // tm,),
        in_specs=cls_specs(ATT_W) + cls_specs(LANES) + [
            _resident((LANES, ATT_W), lambda i: (0, 0)),
            pl.BlockSpec((tm, d), lambda i: (i, 0)),
            trunk.mod_spec(layer, 1),
            _resident((k_in, d), lambda i: (0, 0)),
            _resident((1, d), lambda i: (0, 0)),
            _resident((1, d), lambda i: (0, 0)),
        ],
        out_specs=pl.BlockSpec((tm, d), lambda i: (i, 0)),
        out_shape=jax.ShapeDtypeStruct((trunk.tokens, d), F32),
        scratch_shapes=[pltpu.VMEM((N_GROUPS * (ATT_W // LANES + 1), tm, LANES), F32)],
        compiler_params=_params("arbitrary"),
        name="att_outproj",
    )(*outs, *lses, jnp.asarray(e, BF16), x, ada, w, ln_g.reshape(1, d), ln_b.reshape(1, d))


def _group_norm_gate(o, g, gn_g, gn_b):
    mu = jnp.mean(o, -1, keepdims=True)
    oc = o - mu
    var = jnp.mean(oc * oc, -1, keepdims=True)
    on = oc * lax.rsqrt(var + NORM_EPS) * gn_g + gn_b
    return jax.nn.silu(g) * on


RET_TILE = 1024


def _ret_decay_tables(chunk):
    lg = _log_gamma()
    pos = np.arange(chunk, dtype=np.float64)
    diff = pos[:, None] - pos[None, :]
    inner = np.where(diff >= 0, np.exp(np.maximum(diff, 0.0)[None] * lg[:, None, None]), 0.0)
    qd = np.exp((pos[None, :] + 1.0) * lg[:, None])
    kd = np.exp((chunk - 1.0 - pos[None, :]) * lg[:, None])
    cd = np.exp(chunk * lg)
    return inner, qd, kd, cd


def _ret_prompt_body(q_ref, k_ref, v_ref, g_ref, inner_ref, qd_ref, kd_ref, gng_ref, gnb_ref,
                     o_ref, sfin_ref, s_scr, *, cdec):
    t = pl.program_id(1)

    @pl.when(t == 0)
    def _():
        s_scr[...] = jnp.zeros_like(s_scr)

    for c in range(RET_TILE // RET_CHUNK):
        rows = slice(c * RET_CHUNK, (c + 1) * RET_CHUNK)
        for h in range(RET_HEADS):
            qk = slice(h * RET_DK, (h + 1) * RET_DK)
            vv = slice(h * RET_DV, (h + 1) * RET_DV)
            q = q_ref[rows, qk]
            k = k_ref[rows, qk] * (RET_DK ** -0.5)
            v = v_ref[rows, vv]
            scores = lax.dot_general(q, k, NT_DIMS, preferred_element_type=F32) * inner_ref[h]
            s_old = s_scr[h]
            o = (jnp.dot(scores.astype(BF16), v, preferred_element_type=F32)
                 + jnp.dot(q, s_old.astype(BF16), preferred_element_type=F32) * qd_ref[h])
            kdec = (k.astype(F32) * kd_ref[h]).astype(BF16)
            upd = lax.dot_general(kdec, v, TN_DIMS, preferred_element_type=F32)
            s_scr[h] = s_old * cdec[h] + upd
            gated = _group_norm_gate(o, g_ref[rows, vv].astype(F32), gng_ref[:, vv], gnb_ref[:, vv])
            o_ref[rows, vv] = gated.astype(o_ref.dtype)

    @pl.when(t == pl.num_programs(1) - 1)
    def _():
        sfin_ref[0] = s_scr[...]


def _ret_prompt(qkvg, gn_g, gn_b, n_seq, seq):
    inner, qd, kd, cd = _ret_decay_tables(RET_CHUNK)
    inner = jnp.asarray(inner, F32)
    qd_b = jnp.asarray(np.broadcast_to(qd[:, :, None], (RET_HEADS, RET_CHUNK, RET_DV)), F32)
    kd_b = jnp.asarray(np.broadcast_to(kd[:, :, None], (RET_HEADS, RET_CHUNK, RET_DK)), F32)
    cdec = tuple(float(np.float32(c)) for c in cd)
    tiles = seq // RET_TILE
    vw = RET_HEADS * RET_DV
    qw = RET_HEADS * RET_DK
    return pl.pallas_call(
        functools.partial(_ret_prompt_body, cdec=cdec),
        grid=(n_seq, tiles),
        in_specs=[
            pl.BlockSpec((RET_TILE, qw), lambda n, t: (n * tiles + t, 0)),
            pl.BlockSpec((RET_TILE, qw), lambda n, t: (n * tiles + t, 1)),
            pl.BlockSpec((RET_TILE, vw), lambda n, t: (n * tiles + t, 1)),
            pl.BlockSpec((RET_TILE, vw), lambda n, t: (n * tiles + t, 2)),
            _resident((RET_HEADS, RET_CHUNK, RET_CHUNK), lambda n, t: (0, 0, 0)),
            _resident((RET_HEADS, RET_CHUNK, RET_DV), lambda n, t: (0, 0, 0)),
            _resident((RET_HEADS, RET_CHUNK, RET_DK), lambda n, t: (0, 0, 0)),
            _resident((1, vw), lambda n, t: (0, 0)),
            _resident((1, vw), lambda n, t: (0, 0)),
        ],
        out_specs=[
            pl.BlockSpec((RET_TILE, vw), lambda n, t: (n * tiles + t, 0)),
            pl.BlockSpec((1, RET_HEADS, RET_DK, RET_DV), lambda n, t: (n, 0, 0, 0)),
        ],
        out_shape=[
            jax.ShapeDtypeStruct((n_seq * seq, vw), BF16),
            jax.ShapeDtypeStruct((n_seq, RET_HEADS, RET_DK, RET_DV), F32),
        ],
        scratch_shapes=[pltpu.VMEM((RET_HEADS, RET_DK, RET_DV), F32)],
        compiler_params=_params("arbitrary", "arbitrary"),
        name="retention_prompt",
    )(qkvg, qkvg, qkvg, qkvg, inner, qd_b, kd_b, gn_g.reshape(1, vw), gn_b.reshape(1, vw))


RET_SEQ_GROUP = SUBLANES
RET_S_ROWS = T_SAMPLE * RET_SEQ_GROUP


def _ret_sample_body(q_ref, k_ref, v_ref, g_ref, s_ref, w_ref, qd_ref, kd_ref, cd_ref,
                     gng_ref, gnb_ref, o_ref, so_ref):
    q = q_ref[...].reshape(RET_S_ROWS, RET_DK).astype(BF16)
    k = k_ref[...].reshape(RET_S_ROWS, RET_DK) * (RET_DK ** -0.5)
    v = v_ref[...].reshape(RET_S_ROWS, RET_DV).astype(BF16)
    g = g_ref[...].reshape(RET_S_ROWS, RET_DV)
    scores = lax.dot_general(q, k.astype(BF16), NT_DIMS, preferred_element_type=F32) * w_ref[0]
    o_intra = jnp.dot(scores.astype(BF16), v, preferred_element_type=F32)
    kdec = k * kd_ref[0]
    seq_of_row = lax.broadcasted_iota(jnp.int32, (RET_S_ROWS, 1), 0) % RET_SEQ_GROUP
    o_inter = jnp.zeros((RET_S_ROWS, RET_DV), F32)
    for j in range(RET_SEQ_GROUP):
        mine = seq_of_row == j
        s_old = s_ref[j, 0]
        r = jnp.dot(q, s_old.astype(BF16), preferred_element_type=F32)
        o_inter = jnp.where(mine, r, o_inter)
        kj = jnp.where(mine, kdec, 0.0).astype(BF16)
        upd = lax.dot_general(kj, v, TN_DIMS, preferred_element_type=F32)
        so_ref[j, 0] = s_old * cd_ref[0] + upd
    o = o_intra + o_inter * qd_ref[0]
    gated = _group_norm_gate(o, g, gng_ref[...], gnb_ref[...])
    o_ref[...] = gated.reshape(T_SAMPLE, RET_SEQ_GROUP, RET_DV).astype(o_ref.dtype)


RET_S_STEPS = (N_SAMPLE // RET_SEQ_GROUP) * RET_HEADS


def _ret_sample_part(qkvg, state, gn_g, gn_b):
    inner, qd, kd, cd = _ret_decay_tables(T_SAMPLE)
    row_t = np.arange(RET_S_ROWS) // RET_SEQ_GROUP
    row_j = np.arange(RET_S_ROWS) % RET_SEQ_GROUP
    same = (row_j[:, None] == row_j[None, :])
    w = np.where(same[None], inner[:, row_t[:, None], row_t[None, :]], 0.0)
    w = jnp.asarray(w, F32)
    qd_b = jnp.asarray(np.broadcast_to(qd[:, row_t, None], (RET_HEADS, RET_S_ROWS, RET_DV)), F32)
    kd_b = jnp.asarray(np.broadcast_to(kd[:, row_t, None], (RET_HEADS, RET_S_ROWS, RET_DK)), F32)
    cd_b = jnp.asarray(np.broadcast_to(cd[:, None, None], (RET_HEADS, 1, RET_DV)), F32)
    vw = RET_HEADS * RET_DV
    nq = RET_HEADS
    tg = (T_SAMPLE, RET_SEQ_GROUP)
    nh = RET_HEADS
    return _Part(
        _ret_sample_body,
        (qkvg, qkvg, qkvg, qkvg, state, w, qd_b, kd_b, cd_b,
         gn_g.reshape(1, vw), gn_b.reshape(1, vw)),
        [
            pl.BlockSpec(tg + (RET_DK,), lambda i: (0, i // nh, i % nh)),
            pl.BlockSpec(tg + (RET_DK,), lambda i: (0, i // nh, nq + i % nh)),
            pl.BlockSpec(tg + (RET_DV,), lambda i: (0, i // nh, nq + i % nh)),
            pl.BlockSpec(tg + (RET_DV,), lambda i: (0, i // nh, 2 * nq + i % nh)),
            pl.BlockSpec((RET_SEQ_GROUP, 1, RET_DK, RET_DV), lambda i: (i // nh, i % nh, 0, 0)),
            pl.BlockSpec((1, RET_S_ROWS, RET_S_ROWS), lambda i: (i % nh, 0, 0)),
            pl.BlockSpec((1, RET_S_ROWS, RET_DV), lambda i: (i % nh, 0, 0)),
            pl.BlockSpec((1, RET_S_ROWS, RET_DK), lambda i: (i % nh, 0, 0)),
            pl.BlockSpec((1, 1, RET_DV), lambda i: (i % nh, 0, 0)),
            pl.BlockSpec((1, RET_DV), lambda i: (0, i % nh)),
            pl.BlockSpec((1, RET_DV), lambda i: (0, i % nh)),
        ],
        [
            jax.ShapeDtypeStruct((T_SAMPLE, N_SAMPLE, vw), BF16),
            jax.ShapeDtypeStruct(state.shape, F32),
        ],
        [
            pl.BlockSpec(tg + (RET_DV,), lambda i: (0, i // nh, i % nh)),
            pl.BlockSpec((RET_SEQ_GROUP, 1, RET_DK, RET_DV), lambda i: (i // nh, i % nh, 0, 0)),
        ])


ATT_BQ = 1024
ATT_SLAB = 256
HEADS_PER_SLAB = ATT_SLAB // ATT_DH


def _band_window(cur_ref, prev_ref, s):
    if s == 0:
        return jnp.concatenate([prev_ref[0, 0], cur_ref[0, 0, 0:NBACK, :]], axis=0)
    return cur_ref[0, 0, (s - 1) * NBACK:(s + 1) * NBACK, :]


def _band_body(q_ref, kc_ref, kp_ref, vc_ref, vp_ref, bias_ref, o_ref, lse_ref, s_scr, p_scr):
    i = pl.program_id(2)
    ki = lax.broadcasted_iota(jnp.int32, (NBACK, 2 * NBACK), 1)
    has_prev = ki >= jnp.where(i > 0, 0, NBACK)
    lane_head = lax.broadcasted_iota(jnp.int32, (1, ATT_SLAB), 1) // ATT_DH
    qmasks = [jnp.where(lane_head == hh, ATT_SCALE, 0.0).astype(BF16)
              for hh in range(HEADS_PER_SLAB)]
    nsub = q_ref.shape[2] // NBACK
    pairs = nsub * ATT_HPG
    windows = {}

    def window(cur_ref, prev_ref, s):
        key = (id(cur_ref), s)
        if key not in windows:
            windows[key] = _band_window(cur_ref, prev_ref, s)
        return windows[key]

    def place(idx):
        s, h = divmod(idx, ATT_HPG)
        sl, hh = divmod(h, HEADS_PER_SLAB)
        return s, h, hh, slice(sl * ATT_SLAB, (sl + 1) * ATT_SLAB)

    def scores(idx):
        s, _, hh, lanes = place(idx)
        q = q_ref[0, 0, s * NBACK:(s + 1) * NBACK, lanes]
        s_scr[idx] = lax.dot_general(q * qmasks[hh], window(kc_ref, kp_ref, s)[:, lanes], NT_DIMS,
                                     preferred_element_type=F32)

    def softmax(idx):
        s, h, _, _ = place(idx)
        sc = s_scr[idx] + bias_ref[h]
        if s == 0:
            sc = jnp.where(has_prev, sc, -jnp.inf)
        m = jnp.max(sc, -1, keepdims=True)
        e = jnp.exp(sc - m)
        den = jnp.sum(e, -1, keepdims=True)
        p_scr[idx] = (e / den).astype(BF16)
        lse_ref[0, 0, s * NBACK:(s + 1) * NBACK, h * LSE_REP:(h + 1) * LSE_REP] = jnp.broadcast_to(
            m + jnp.log(den), (NBACK, LSE_REP))

    def values(idx):
        s, h, hh, lanes = place(idx)
        oh = jnp.dot(p_scr[idx], window(vc_ref, vp_ref, s)[:, lanes], preferred_element_type=F32)
        o_ref[0, 0, s * NBACK:(s + 1) * NBACK, h * ATT_DH:(h + 1) * ATT_DH] = (
            oh[:, hh * ATT_DH:(hh + 1) * ATT_DH].astype(o_ref.dtype))

    for stage in (scores, softmax, values):
        for idx in range(pairs):
            stage(idx)


def _band_bias(group):
    _, dil = ATT_GROUPS[group]
    steps = NBACK + np.arange(NBACK)[:, None] - np.arange(2 * NBACK)[None, :]
    valid = (steps >= 0) & (steps <= NBACK)
    slopes = _alibi_slopes()[group]
    bias = -(slopes[:, None, None] * dil) * steps[None]
    return np.where(valid[None], bias, -np.inf)


def _band_attention(qkv_cls, group):
    _, dil = ATT_GROUPS[group]
    n_seq, _, length, _ = qkv_cls.shape
    bq = min(ATT_BQ, length)
    nsub = bq // NBACK
    blocks = length // bq
    pairs = nsub * ATT_HPG
    prev = lambda i: jnp.maximum(i * nsub - 1, 0)
    cur = lambda col: pl.BlockSpec((1, 1, bq, ATT_W), lambda n, r, i: (n, r, i, col))
    halo = lambda col: pl.BlockSpec((1, 1, NBACK, ATT_W), lambda n, r, i: (n, r, prev(i), col))
    out_sds = jax.ShapeDtypeStruct((n_seq, dil, length, ATT_W), BF16)
    return pl.pallas_call(
        _band_body,
        grid=(n_seq, dil, blocks),
        in_specs=[cur(0), cur(1), halo(1), cur(2), halo(2),
                  _resident((ATT_HPG, NBACK, 2 * NBACK), lambda n, r, i: (0, 0, 0))],
        out_specs=[cur(0), pl.BlockSpec((1, 1, bq, LANES), lambda n, r, i: (n, r, i, 0))],
        out_shape=[out_sds, jax.ShapeDtypeStruct((n_seq, dil, length, LANES), F32)],
        scratch_shapes=[pltpu.VMEM((pairs, NBACK, 2 * NBACK), F32),
                        pltpu.VMEM((pairs, NBACK, 2 * NBACK), BF16)],
        compiler_params=_params("arbitrary", "arbitrary", "arbitrary"),
        name="band_attention",
    )(qkv_cls, qkv_cls, qkv_cls, qkv_cls, qkv_cls, jnp.asarray(_band_bias(group), F32))


SA_PASSES = tuple((0, (t,)) for t in range(T_SAMPLE)) + ((1, tuple(range(T_SAMPLE))),
                                                        (2, tuple(range(T_SAMPLE))))
SA_MAX_LB = max(w for w, _ in ATT_GROUPS)


def _att_sample_tables():
    slopes = _alibi_slopes()
    sels, biases = [], []
    bias_new = np.full((N_GROUPS, T_SAMPLE * ATT_HPG, LANES), -np.inf)
    cs = np.zeros((len(SA_PASSES), SUBLANES, LANES))
    lane = np.arange(LANES)
    for g, (lb, dil) in enumerate(ATT_GROUPS):
        passes = [p for p in SA_PASSES if p[0] == g]
        sel = np.zeros((ATT_HPG, T_SAMPLE, len(passes), ATT_HPG, LANES))
        for pi, (_, t_set) in enumerate(passes):
            for t in t_set:
                owns = np.ones(LANES, bool) if dil == 1 else (lane % dil == t)
                for h in range(ATT_HPG):
                    sel[h, t, pi, h, owns] = 1.0
        sels.append(sel.reshape(ATT_HPG * T_SAMPLE, len(passes) * ATT_HPG * LANES))
        pos = np.arange(lb)
        bias = np.full((T_SAMPLE, ATT_HPG, lb), -np.inf)
        for t in range(T_SAMPLE):
            if dil == 1:
                j = lb + t - pos
                valid = j <= NBACK
            else:
                j = NBACK - pos // dil
                valid = (pos % dil) == t
            bias[t] = np.where(valid[None, :], -(slopes[g][:, None] * dil) * j[None, :], -np.inf)
            for h in range(ATT_HPG):
                for t2 in range(T_SAMPLE):
                    if (dil == 1 and t2 <= t) or t2 == t:
                        bias_new[g, t * ATT_HPG + h, t2] = -slopes[g][h] * dil * (t - t2)
        biases.append(bias)
    for pi, (g, t_set) in enumerate(SA_PASSES):
        dil = ATT_GROUPS[g][1]
        for t in t_set:
            cs[pi, t] = 1.0 if dil == 1 else (lane % dil == t)
    bd = (np.arange(ATT_W)[None, :] // ATT_DH == np.arange(ATT_HPG)[:, None]).astype(np.float64)
    return sels, biases, bias_new, cs, bd


SA_SEQ_PER_STEP = 2


def _sa_new_row(y_ref, j, t, col):
    row = (pl.program_id(0) % (SUBLANES // SA_SEQ_PER_STEP)) * SA_SEQ_PER_STEP + j
    lane0 = (t * 3 * N_GROUPS + col) * ATT_W
    return y_ref[pl.ds(row, 1), lane0:lane0 + ATT_W]


def _sa_new_rows(y_ref, j, col, n_rows):
    rows = [_sa_new_row(y_ref, j, t, col) for t in range(T_SAMPLE)]
    return jnp.concatenate(rows + [jnp.zeros((n_rows - T_SAMPLE, ATT_W), F32)], 0)


def _att_scores_body(qd_ref, y_ref, c0_ref, c1_ref, c2_ref,
                     sel0_ref, sel1_ref, sel2_ref, b0_ref, b1_ref, b2_ref, bn_ref, bd_ref,
                     p0_ref, p1_ref, p2_ref, pn_ref, s_scr, e_scr):
    caches = (c0_ref, c1_ref, c2_ref)
    sels = (sel0_ref, sel1_ref, sel2_ref)
    biases = (b0_ref, b1_ref, b2_ref)
    p_refs = (p0_ref, p1_ref, p2_ref)
    bd = bd_ref[...]
    qsels, s_news = {}, {}
    for j in range(SA_SEQ_PER_STEP):
        for g in range(N_GROUPS):
            qsels[j, g] = jnp.dot(qd_ref[j, g].astype(BF16), sels[g][...],
                                  preferred_element_type=F32)
            qbd = jnp.concatenate(
                [jnp.broadcast_to(_sa_new_row(y_ref, j, t, g), (ATT_HPG, ATT_W)) * bd
                 for t in range(T_SAMPLE)], 0)
            kn = _sa_new_rows(y_ref, j, N_GROUPS + g, LANES)
            s_news[j, g] = lax.dot_general(qbd.astype(BF16), kn.astype(BF16), NT_DIMS,
                                           preferred_element_type=F32) * ATT_SCALE + bn_ref[g]
    for j in range(SA_SEQ_PER_STEP):
        e_new, dens, lses = {}, {}, {}
        pid = 0
        for g, (lb, dil) in enumerate(ATT_GROUPS):
            chunks = lb // LANES
            passes = [p for p in SA_PASSES if p[0] == g]
            qsel, s_new = qsels[j, g], s_news[j, g]
            pass_of = {}
            for local, (_, t_set) in enumerate(passes):
                for h in range(ATT_HPG):
                    col = (local * ATT_HPG + h) * LANES
                    qs = qsel[:, col:col + LANES]
                    for c in range(chunks):
                        lanes = slice(c * LANES, (c + 1) * LANES)
                        prod = caches[g][j, 0, h, :, lanes] * qs
                        part = prod.reshape(ATT_DH // SUBLANES, SUBLANES, LANES).sum(0)
                        s_scr[pid, h:h + 1, lanes] = part.sum(0, keepdims=True)
                for t in t_set:
                    pass_of[t] = pid
                pid += 1
            for t in range(T_SAMPLE):
                sc = s_scr[pass_of[t], :, :lb] * ATT_SCALE + biases[g][t]
                sn = s_new[t * ATT_HPG:(t + 1) * ATT_HPG]
                m = jnp.maximum(jnp.max(sc, -1, keepdims=True), jnp.max(sn, -1, keepdims=True))
                e = jnp.exp(sc - m)
                en = jnp.exp(sn - m)
                den = jnp.sum(e, -1, keepdims=True) + jnp.sum(en, -1, keepdims=True)
                e_scr[g * T_SAMPLE + t, :, :lb] = e
                e_new[g, t], dens[g, t], lses[g, t] = en, den, m + jnp.log(den)
        wts = {}
        for t in range(T_SAMPLE):
            ls = [lses[g, t] for g in range(N_GROUPS)]
            m = jnp.maximum(jnp.maximum(ls[0], ls[1]), ls[2])
            ws = [jnp.exp(l - m) for l in ls]
            tot = ws[0] + ws[1] + ws[2]
            for g in range(N_GROUPS):
                wts[g, t] = ws[g] / (tot * dens[g, t])
        for g, (lb, dil) in enumerate(ATT_GROUPS):
            pn_ref[j, g] = jnp.concatenate(
                [e_new[g, t] * wts[g, t] for t in range(T_SAMPLE)], 0)
            for local, (_, t_set) in enumerate([p for p in SA_PASSES if p[0] == g]):
                pc = None
                for t in t_set:
                    term = e_scr[g * T_SAMPLE + t, :, :lb] * wts[g, t]
                    pc = term if pc is None else pc + term
                p_refs[g][j, local] = pc


def _att_values_body(p0_ref, p1_ref, p2_ref, pn_ref, y_ref, c0_ref, c1_ref, c2_ref,
                     cs_ref, bd_ref, o_ref):
    caches = (c0_ref, c1_ref, c2_ref)
    p_refs = (p0_ref, p1_ref, p2_ref)
    bd = bd_ref[...]
    wfs = {}
    for j in range(SA_SEQ_PER_STEP):
        for pid, (g, _) in enumerate(SA_PASSES):
            local = pid - [p[0] for p in SA_PASSES].index(g)
            folded = []
            for h in range(ATT_HPG):
                acc = None
                for c in range(ATT_GROUPS[g][0] // LANES):
                    lanes = slice(c * LANES, (c + 1) * LANES)
                    term = caches[g][j, 0, h, :, lanes] * p_refs[g][j, local, h:h + 1, lanes]
                    acc = term if acc is None else acc + term
                folded.append(acc)
            wfs[j, pid] = jnp.concatenate(folded, 0).astype(BF16)
    for j in range(SA_SEQ_PER_STEP):
        o = jnp.zeros((SUBLANES, ATT_W), F32)
        o_new = jnp.zeros((T_SAMPLE * ATT_HPG, ATT_W), F32)
        for g in range(N_GROUPS):
            vn = _sa_new_rows(y_ref, j, 2 * N_GROUPS + g, LANES)
            o_new = o_new + jnp.dot(pn_ref[j, g].astype(BF16), vn.astype(BF16),
                                    preferred_element_type=F32)
        for pid in range(len(SA_PASSES)):
            o = o + lax.dot_general(cs_ref[pid].astype(BF16), wfs[j, pid], NT_DIMS,
                                    preferred_element_type=F32)
        extra = [jnp.sum(o_new[t * ATT_HPG:(t + 1) * ATT_HPG] * bd, 0, keepdims=True)
                 for t in range(T_SAMPLE)]
        o_ref[j] = o[:T_SAMPLE] + jnp.concatenate(extra, 0)


def _sa_new_spec():
    per_block = SUBLANES // SA_SEQ_PER_STEP
    return pl.BlockSpec((SUBLANES, T_SAMPLE * 3 * N_GROUPS * ATT_W), lambda i: (i // per_block, 0))


def _sa_cache_specs(caches, kv):
    views = [c.transpose(0, 2, 3, 4, 1) for c in caches]
    specs = [pl.BlockSpec((SA_SEQ_PER_STEP, 1, ATT_HPG, ATT_DH, lb), lambda i: (i, kv, 0, 0, 0))
             for lb, _ in ATT_GROUPS]
    return views, specs


def _sa_const_specs(tables):
    return [_resident(t.shape, lambda i, nd=t.ndim: (0,) * nd) for t in tables]


def _sa_prob_layout():
    shapes, specs = [], []
    for g, (lb, _) in enumerate(ATT_GROUPS):
        n_pass = len([p for p in SA_PASSES if p[0] == g])
        shapes.append(jax.ShapeDtypeStruct((N_SAMPLE, n_pass, ATT_HPG, lb), F32))
        specs.append(pl.BlockSpec((SA_SEQ_PER_STEP, n_pass, ATT_HPG, lb), lambda i: (i, 0, 0, 0)))
    shapes.append(jax.ShapeDtypeStruct((N_SAMPLE, N_GROUPS, T_SAMPLE * ATT_HPG, LANES), F32))
    specs.append(pl.BlockSpec((SA_SEQ_PER_STEP, N_GROUPS, T_SAMPLE * ATT_HPG, LANES),
                              lambda i: (i, 0, 0, 0)))
    return shapes, specs


def _att_scores_part(y, caches):
    sels, biases, bias_new, _, bd = _att_sample_tables()
    q = y.reshape(N_SAMPLE, T_SAMPLE, 3 * N_GROUPS, ATT_HPG, ATT_DH)[:, :, :N_GROUPS]
    qd = q.transpose(0, 2, 4, 3, 1).reshape(N_SAMPLE, N_GROUPS, ATT_DH, ATT_HPG * T_SAMPLE)
    views, cache_specs = _sa_cache_specs(caches, 0)
    tables = ([jnp.asarray(s, BF16) for s in sels] + [jnp.asarray(b, F32) for b in biases]
              + [jnp.asarray(bias_new, F32), jnp.asarray(bd, F32)])
    out_shape, out_specs = _sa_prob_layout()
    return _Part(
        _att_scores_body,
        (qd, y, *views, *tables),
        [pl.BlockSpec((SA_SEQ_PER_STEP, N_GROUPS, ATT_DH, ATT_HPG * T_SAMPLE),
                      lambda i: (i, 0, 0, 0)), _sa_new_spec()]
        + cache_specs + _sa_const_specs(tables),
        out_shape, out_specs,
        [pltpu.VMEM((len(SA_PASSES), ATT_HPG, SA_MAX_LB), F32),
         pltpu.VMEM((N_GROUPS * T_SAMPLE, ATT_HPG, SA_MAX_LB), F32)])


def _att_values_part(probs, y, caches):
    _, _, _, cs, bd = _att_sample_tables()
    views, cache_specs = _sa_cache_specs(caches, 1)
    tables = [jnp.asarray(cs, F32), jnp.asarray(bd, F32)]
    _, prob_specs = _sa_prob_layout()
    return _Part(
        _att_values_body,
        (*probs, y, *views, *tables),
        prob_specs + [_sa_new_spec()] + cache_specs + _sa_const_specs(tables),
        [jax.ShapeDtypeStruct((N_SAMPLE, T_SAMPLE, ATT_W), F32)],
        [pl.BlockSpec((SA_SEQ_PER_STEP, T_SAMPLE, ATT_W), lambda i: (i, 0, 0))])


def kernel(x_prompt, x_sample, c_prompt, c_sample, state_ret, cache_kv_w128, cache_kv_w512,
           cache_kv_w2048, w_ada, b_ada, ln_g, ln_b, w_ffn_in, w_ffn_out, w_ret_in, ret_gn_g,
           ret_gn_b, w_ret_out, w_att_in, w_att_out):
    d = D_MODEL
    n_p, seq, _ = x_prompt.shape
    w_ffn_in_b = {(0, 0): w_ffn_in[0, 0].astype(BF16)}
    w_ffn_out_b = {(0, 0): w_ffn_out[0, 0].astype(BF16)}
    w_ret_in_b = w_ret_in.astype(BF16)
    later_ffn = [(0, 1), (1, 0), (1, 1)]
    cast_items = ([(w_ffn_in, lw) for lw in later_ffn] + [(w_ffn_out, lw) for lw in later_ffn]
                  + [(w_ret_out, ()), (w_att_in, ()), (w_att_out, ())])

    c_all = jnp.concatenate(
        [c_sample, c_prompt, jnp.zeros((ADA_ROWS - N_SAMPLE - n_p, d), F32)], axis=0)
    ada = _ada_table(c_all, w_ada, b_ada)

    prompt = _Trunk(n_p * seq, 512, False, seq // 512)
    prompt_wide = _Trunk(n_p * seq, 1024, False, seq // 1024)
    sample = _Trunk(N_SAMPLE * T_SAMPLE, N_SAMPLE, True, 1)
    xp = x_prompt.reshape(n_p * seq, d)
    xs = x_sample.transpose(1, 0, 2).reshape(T_SAMPLE * N_SAMPLE, d)

    steps = RET_S_STEPS
    tile_s = n_p * seq // steps
    prompt_s = _Trunk(n_p * seq, tile_s, False, seq // tile_s)

    def ffn_part(x, trunk, layer, which):
        sub = 0 if which == 0 else 2
        return _ffn_part(x, ada, w_ffn_in_b[layer, which], w_ffn_out_b[layer, which],
                         ln_g[layer, sub], ln_b[layer, sub], trunk, layer, sub)

    def ffn(x, trunk, layer, which):
        return _run([ffn_part(x, trunk, layer, which)], trunk.tokens // trunk.tile, "ffn")[0][0]

    xs = ffn(xs, sample, 0, 0)
    qkvg_s = _inproj(xs, ada, w_ret_in_b, sample, 0, F32)

    (xp,), (gated_s, ret_s), cast = _run(
        [ffn_part(xp, prompt_s, 0, 0),
         _ret_sample_part(qkvg_s.reshape(T_SAMPLE, N_SAMPLE, -1), state_ret, ret_gn_g, ret_gn_b),
         _cast_part(cast_items, steps)],
        steps, "ffn_ret_sample")
    for k, lw in enumerate(later_ffn):
        w_ffn_in_b[lw], w_ffn_out_b[lw] = cast[k], cast[len(later_ffn) + k]
    w_ret_out_b, w_att_in_b, w_att_out_b = cast[2 * len(later_ffn):]

    xs = _outproj(gated_s.reshape(T_SAMPLE * N_SAMPLE, -1), xs, ada, w_ret_out_b,
                  ln_g[0, 1], ln_b[0, 1], sample, 0)
    xs = ffn(xs, sample, 0, 1)
    xs = ffn(xs, sample, 1, 0)
    qkv_s = _inproj(xs, ada, w_att_in_b, sample, 1, F32, tiles_to_columns=True)
    caches = (cache_kv_w128, cache_kv_w512, cache_kv_w2048)

    qkvg_p = _inproj(xp, ada, w_ret_in_b, prompt, 0, BF16)
    gated_p, ret_p = _ret_prompt(qkvg_p, ret_gn_g, ret_gn_b, n_p, seq)
    xp = _outproj(gated_p, xp, ada, w_ret_out_b, ln_g[0, 1], ln_b[0, 1], prompt_wide, 0)

    probs, (xp,) = _run([_att_scores_part(qkv_s, caches), ffn_part(xp, prompt_s, 0, 1)],
                        steps, "att_scores_ffn")
    (xp,), (att_s,) = _run([ffn_part(xp, prompt_s, 1, 0), _att_values_part(probs, qkv_s, caches)],
                           steps, "ffn_att_values")

    keep_p = [min(w, seq) for w, _ in ATT_GROUPS]
    *qkv_cls, kv_tail = _att_inproj_prompt(xp, ada, w_att_in_b, prompt, 1, max(keep_p))
    band = [_band_attention(qkv_cls[g], g) for g in range(N_GROUPS)]
    xp = _att_outproj_prompt([o for o, _ in band], [l for _, l in band], xp, ada, w_att_out_b,
                             ln_g[1, 1], ln_b[1, 1], prompt_wide, 1)
    kv_tail = kv_tail.reshape(n_p, max(keep_p), 2, N_GROUPS, ATT_HPG, ATT_DH)
    kv_p = [kv_tail[:, max(keep_p) - keep_p[g]:, :, g] for g in range(N_GROUPS)]

    att_s = att_s.transpose(1, 0, 2).reshape(T_SAMPLE * N_SAMPLE, ATT_W)
    xs = _outproj(att_s, xs, ada, w_att_out_b, ln_g[1, 1], ln_b[1, 1], sample, 1)
    new_s = qkv_s.reshape(N_SAMPLE, T_SAMPLE, 3, N_GROUPS, ATT_HPG, ATT_DH)
    kv_s = [new_s[:, :, 1:, g] for g in range(N_GROUPS)]

    xp = ffn(xp, prompt, 1, 1)
    xs = ffn(xs, sample, 1, 1)

    y_prompt = xp.reshape(n_p, seq, d)
    y_sample = xs.reshape(T_SAMPLE, N_SAMPLE, d).transpose(1, 0, 2)
    return (y_prompt, y_sample, ret_p, ret_s, kv_p[0], kv_s[0], kv_p[1], kv_s[1], kv_p[2], kv_s[2])
```

```python
import functools

import numpy as np
import jax
import jax.numpy as jnp
from jax import lax
from jax.experimental import pallas as pl
from jax.experimental.pallas import tpu as pltpu

F32 = jnp.float32
BF16 = jnp.bfloat16

D_MODEL = 1024
DEPTH = 2
D_FF = 2816
RET_HEADS = 4
RET_DK = 256
RET_DV = 512
RET_CHUNK = 256
ATT_GROUPS = ((128, 1), (512, 4), (2048, 16))
N_GROUPS = 3
ATT_HPG = 8
ATT_DH = 64
ATT_W = ATT_HPG * ATT_DH
NBACK = 128
LSE_REP = 128 // ATT_HPG
ALPHA = (2 * DEPTH) ** 0.25
NORM_EPS = 1e-5
ATT_SCALE = ATT_DH ** -0.5

V7X_VMEM_LIMIT_BYTES = 60000 * 1024
LANES = 128
SUBLANES = 8

N_SAMPLE = 128
T_SAMPLE = 4
ADA_ROWS = 136
PROMPT_ROW_BLOCK = N_SAMPLE // SUBLANES

NT_DIMS = (((1,), (1,)), ((), ()))
TN_DIMS = (((0,), (0,)), ((), ()))


def _params(*semantics):
    return pltpu.CompilerParams(dimension_semantics=semantics,
                                vmem_limit_bytes=V7X_VMEM_LIMIT_BYTES)


def _resident(block, index_map):
    return pl.BlockSpec(block, index_map, pipeline_mode=pl.Buffered(1))


def _layer_norm(z, g, b):
    mu = jnp.mean(z, -1, keepdims=True)
    zc = z - mu
    var = jnp.mean(zc * zc, -1, keepdims=True)
    return zc * lax.rsqrt(var + NORM_EPS) * g + b


def _log_gamma():
    return np.log1p(-(2.0 ** (-5.0 - np.arange(RET_HEADS, dtype=np.float64))))


def _alibi_slopes():
    h = np.arange(1, N_GROUPS * ATT_HPG + 1, dtype=np.float64)
    return (2.0 ** (-8.0 * h / (N_GROUPS * ATT_HPG))).reshape(N_GROUPS, ATT_HPG)


ADA_SLABS = 3


def _ada_body(c_ref, w_ref, b_ref, o_ref):
    s = jax.nn.silu(c_ref[...]).astype(BF16)
    for k in range(ADA_SLABS):
        w = w_ref[0, :, k * D_MODEL:(k + 1) * D_MODEL].astype(BF16)
        o_ref[0, k] = jnp.dot(s, w, preferred_element_type=F32) + b_ref[0, k]


def _ada_table(c_all, w_ada, b_ada):
    d = D_MODEL
    return pl.pallas_call(
        _ada_body,
        grid=(DEPTH, 9 // ADA_SLABS),
        in_specs=[
            _resident((ADA_ROWS, d), lambda i, j: (0, 0)),
            pl.BlockSpec((1, d, ADA_SLABS * d), lambda i, j: (i, 0, j)),
            pl.BlockSpec((1, ADA_SLABS, 1, d), lambda i, j: (i, j, 0, 0)),
        ],
        out_specs=pl.BlockSpec((1, ADA_SLABS, ADA_ROWS, d), lambda i, j: (i, j, 0, 0)),
        out_shape=jax.ShapeDtypeStruct((DEPTH, 9, ADA_ROWS, d), F32),
        compiler_params=_params("arbitrary", "arbitrary"),
        name="ada_table",
    )(c_all, w_ada, b_ada.reshape(DEPTH, 9, 1, d))


class _Trunk:
    def __init__(self, tokens, tile, per_row, tiles_per_seq):
        self.tokens = tokens
        self.tile = tile
        self.per_row = per_row
        self.tiles_per_seq = tiles_per_seq

    def mod_spec(self, layer, sub):
        if self.per_row:
            return pl.BlockSpec((1, 3, self.tile, D_MODEL), lambda i: (layer, sub, 0, 0))
        return pl.BlockSpec((1, 3, SUBLANES, D_MODEL),
                            lambda i: (layer, sub, PROMPT_ROW_BLOCK, 0))

    def mod_rows(self, mod_ref, k):
        if self.per_row:
            return mod_ref[0, k]
        n = pl.program_id(0) // self.tiles_per_seq
        return mod_ref[0, k, pl.ds(n, 1), :]


def _ffn_body(x_ref, mod_ref, win_ref, wout_ref, g_ref, b_ref, o_ref, *, trunk):
    x = x_ref[...]
    shift = trunk.mod_rows(mod_ref, 0)
    scale = trunk.mod_rows(mod_ref, 1)
    gate = trunk.mod_rows(mod_ref, 2)
    u = (x * (1.0 + scale) + shift).astype(BF16)
    h = jnp.dot(u, win_ref[...], preferred_element_type=F32)
    a = h[:, :D_FF]
    b = h[:, D_FF:]
    act = (jax.nn.silu(a) * b).astype(BF16)
    y = jnp.dot(act, wout_ref[...], preferred_element_type=F32)
    z = ALPHA * x + (0.5 * (1.0 + gate)) * y
    o_ref[...] = _layer_norm(z, g_ref[...], b_ref[...])


class _Part:
    def __init__(self, body, args, in_specs, out_shape, out_specs, scratch=()):
        self.body, self.args, self.in_specs = body, list(args), list(in_specs)
        self.out_shape, self.out_specs, self.scratch = list(out_shape), list(out_specs), list(scratch)


def _run(parts, steps, name):
    n_in = [len(p.args) for p in parts]
    n_out = [len(p.out_shape) for p in parts]
    n_scr = [len(p.scratch) for p in parts]

    def body(*refs):
        ins, outs, scr = refs[:sum(n_in)], refs[sum(n_in):sum(n_in) + sum(n_out)], refs[sum(n_in) + sum(n_out):]
        a = b = c = 0
        for k, p in enumerate(parts):
            p.body(*ins[a:a + n_in[k]], *outs[b:b + n_out[k]], *scr[c:c + n_scr[k]])
            a, b, c = a + n_in[k], b + n_out[k], c + n_scr[k]

    res = pl.pallas_call(
        body,
        grid=(steps,),
        in_specs=[s for p in parts for s in p.in_specs],
        out_specs=[s for p in parts for s in p.out_specs],
        out_shape=[s for p in parts for s in p.out_shape],
        scratch_shapes=[s for p in parts for s in p.scratch],
        compiler_params=_params("arbitrary"),
        name=name,
    )(*[a for p in parts for a in p.args])
    out, b = [], 0
    for k in range(len(parts)):
        out.append(res[b:b + n_out[k]])
        b += n_out[k]
    return out


def _cast_body(*refs):
    n = len(refs) // 2
    for src, dst in zip(refs[:n], refs[n:]):
        dst[...] = src[...].reshape(dst.shape).astype(dst.dtype)


def _cast_part(items, steps):
    args, in_specs, out_shape, out_specs = [], [], [], []
    for arr, lead in items:
        rows, cols = arr.shape[-2:]
        n_blocks = steps
        while rows % n_blocks or (rows // n_blocks) % (2 * SUBLANES):
            n_blocks //= 2
        blk = rows // n_blocks
        args.append(arr)
        in_specs.append(pl.BlockSpec(
            (1,) * len(lead) + (blk, cols),
            lambda i, lead=lead, last=n_blocks - 1: lead + (jnp.minimum(i, last), 0)))
        out_shape.append(jax.ShapeDtypeStruct((rows, cols), BF16))
        out_specs.append(pl.BlockSpec((blk, cols),
                                      lambda i, last=n_blocks - 1: (jnp.minimum(i, last), 0)))
    return _Part(_cast_body, args, in_specs, out_shape, out_specs)


def _ffn_part(x, ada, w_in, w_out, ln_g, ln_b, trunk, layer, sub):
    d = D_MODEL
    tm = trunk.tile
    return _Part(
        functools.partial(_ffn_body, trunk=trunk),
        (x, ada, w_in, w_out, ln_g.reshape(1, d), ln_b.reshape(1, d)),
        [
            pl.BlockSpec((tm, d), lambda i: (i, 0)),
            trunk.mod_spec(layer, sub),
            _resident((d, 2 * D_FF), lambda i: (0, 0)),
            _resident((D_FF, d), lambda i: (0, 0)),
            _resident((1, d), lambda i: (0, 0)),
            _resident((1, d), lambda i: (0, 0)),
        ],
        [jax.ShapeDtypeStruct((trunk.tokens, d), F32)],
        [pl.BlockSpec((tm, d), lambda i: (i, 0))])


def _inproj_body(x_ref, mod_ref, w_ref, o_ref, *, trunk):
    x = x_ref[...]
    u = (x * (1.0 + trunk.mod_rows(mod_ref, 1)) + trunk.mod_rows(mod_ref, 0)).astype(BF16)
    o_ref[...] = jnp.dot(u, w_ref[...], preferred_element_type=F32).astype(o_ref.dtype)


def _inproj(x, ada, w, trunk, layer, out_dtype, tiles_to_columns=False):
    d = D_MODEL
    tm = trunk.tile
    n_out = w.shape[1]
    tiles = trunk.tokens // tm
    if tiles_to_columns:
        out_spec = pl.BlockSpec((tm, n_out), lambda i: (0, i))
        out_shape = jax.ShapeDtypeStruct((tm, tiles * n_out), out_dtype)
    else:
        out_spec = pl.BlockSpec((tm, n_out), lambda i: (i, 0))
        out_shape = jax.ShapeDtypeStruct((trunk.tokens, n_out), out_dtype)
    return pl.pallas_call(
        functools.partial(_inproj_body, trunk=trunk),
        grid=(tiles,),
        in_specs=[
            pl.BlockSpec((tm, d), lambda i: (i, 0)),
            trunk.mod_spec(layer, 1),
            _resident((d, n_out), lambda i: (0, 0)),
        ],
        out_specs=out_spec,
        out_shape=out_shape,
        compiler_params=_params("arbitrary"),
        name="inproj",
    )(x, ada, w)


def _att_inproj_body(x_ref, mod_ref, w_ref, a0_ref, a1_ref, a2_ref, tail_ref, y_scr, *, trunk):
    x = x_ref[...]
    u = (x * (1.0 + trunk.mod_rows(mod_ref, 1)) + trunk.mod_rows(mod_ref, 0)).astype(BF16)
    slabs = ATT_W // LANES
    a_refs = (a0_ref, a1_ref, a2_ref)
    for part in range(3):
        for g in range(N_GROUPS):
            col0 = (part * N_GROUPS + g) * ATT_W
            for c in range(ATT_W // ATT_SLAB):
                lo = col0 + c * ATT_SLAB
                y = jnp.dot(u, w_ref[:, lo:lo + ATT_SLAB], preferred_element_type=F32)
                if part > 0:
                    tail_ref[0, :, lo - N_GROUPS * ATT_W:lo - N_GROUPS * ATT_W + ATT_SLAB] = y
                for half in range(ATT_SLAB // LANES):
                    y_scr[lo // LANES + half] = y[:, half * LANES:(half + 1) * LANES]
            dil = ATT_GROUPS[g][1]
            rows = trunk.tile // dil
            for r in range(dil):
                sel = pl.ds(r, rows, stride=dil) if dil > 1 else slice(None)
                for cb in range(slabs):
                    dst = (part * slabs + cb) * LANES
                    a_refs[g][0, r, :, dst:dst + LANES] = (
                        y_scr[col0 // LANES + cb, sel, :].astype(BF16))


def _att_inproj_prompt(x, ada, w, trunk, layer, tail_rows):
    d = D_MODEL
    tm = trunk.tile
    n_out = w.shape[1]
    tps = trunk.tiles_per_seq
    n_seq = trunk.tokens // (tps * tm)
    first = tps - tail_rows // tm
    out_shape, out_specs = [], []
    for _, dil in ATT_GROUPS:
        out_shape.append(jax.ShapeDtypeStruct((n_seq, dil, tps * tm // dil, 3 * ATT_W), BF16))
        out_specs.append(pl.BlockSpec((1, dil, tm // dil, 3 * ATT_W),
                                      lambda i: (i // tps, 0, i % tps, 0)))
    kv_w = n_out - N_GROUPS * ATT_W
    out_shape.append(jax.ShapeDtypeStruct((n_seq, tail_rows, kv_w), F32))
    out_specs.append(pl.BlockSpec((1, tm, kv_w),
                                  lambda i: (i // tps, jnp.maximum(i % tps - first, 0), 0)))
    return pl.pallas_call(
        functools.partial(_att_inproj_body, trunk=trunk),
        grid=(trunk.tokens // tm,),
        in_specs=[
            pl.BlockSpec((tm, d), lambda i: (i, 0)),
            trunk.mod_spec(layer, 1),
            _resident((d, n_out), lambda i: (0, 0)),
        ],
        out_specs=out_specs,
        out_shape=out_shape,
        scratch_shapes=[pltpu.VMEM((n_out // LANES, tm, LANES), F32)],
        compiler_params=_params("arbitrary"),
        name="att_inproj",
    )(x, ada, w)


def _outproj_tail(a, x_ref, mod_ref, w_ref, g_ref, b_ref, o_ref, trunk):
    x = x_ref[...]
    gate = trunk.mod_rows(mod_ref, 2)
    y = jnp.dot(a.astype(BF16), w_ref[...], preferred_element_type=F32)
    z = ALPHA * x + (1.0 + gate) * y
    o_ref[...] = _layer_norm(z, g_ref[...], b_ref[...])


def _outproj_body(a_ref, x_ref, mod_ref, w_ref, g_ref, b_ref, o_ref, *, trunk):
    _outproj_tail(a_ref[...], x_ref, mod_ref, w_ref, g_ref, b_ref, o_ref, trunk)


def _outproj(a, x, ada, w, ln_g, ln_b, trunk, layer):
    d = D_MODEL
    tm = trunk.tile
    k_in = w.shape[0]
    return pl.pallas_call(
        functools.partial(_outproj_body, trunk=trunk),
        grid=(trunk.tokens // tm,),
        in_specs=[
            pl.BlockSpec((tm, k_in), lambda i: (i, 0)),
            pl.BlockSpec((tm, d), lambda i: (i, 0)),
            trunk.mod_spec(layer, 1),
            _resident((k_in, d), lambda i: (0, 0)),
            _resident((1, d), lambda i: (0, 0)),
            _resident((1, d), lambda i: (0, 0)),
        ],
        out_specs=pl.BlockSpec((tm, d), lambda i: (i, 0)),
        out_shape=jax.ShapeDtypeStruct((trunk.tokens, d), F32),
        compiler_params=_params("arbitrary"),
        name="outproj",
    )(a, x, ada, w, ln_g.reshape(1, d), ln_b.reshape(1, d))


def _att_outproj_body(o0_ref, o1_ref, o2_ref, l0_ref, l1_ref, l2_ref, e_ref, x_ref, mod_ref, w_ref,
                      g_ref, b_ref, out_ref, tok_scr, *, trunk):
    def token_major(ref, g, first_slab):
        dil = ATT_GROUPS[g][1]
        slabs = ref.shape[-1] // LANES
        if dil == 1:
            return ref[0, 0].astype(F32)
        rows = trunk.tile // dil
        for r in range(dil):
            for cb in range(slabs):
                tok_scr[first_slab + cb, pl.ds(r, rows, stride=dil), :] = (
                    ref[0, r, :, cb * LANES:(cb + 1) * LANES].astype(F32))
        return jnp.concatenate([tok_scr[first_slab + cb] for cb in range(slabs)], axis=1)

    o_slabs = ATT_W // LANES
    os = [token_major(r, g, g * o_slabs) for g, r in enumerate((o0_ref, o1_ref, o2_ref))]
    ls = [token_major(r, g, N_GROUPS * o_slabs + g)
          for g, r in enumerate((l0_ref, l1_ref, l2_ref))]
    m = jnp.maximum(jnp.maximum(ls[0], ls[1]), ls[2])
    ws = [jnp.exp(l - m) for l in ls]
    inv = 1.0 / (ws[0] + ws[1] + ws[2])
    a = None
    for g in range(N_GROUPS):
        alpha = ws[g] * inv
        hi = alpha.astype(BF16)
        lo = (alpha - hi.astype(F32)).astype(BF16)
        spread = (jnp.dot(hi, e_ref[...], preferred_element_type=F32)
                  + jnp.dot(lo, e_ref[...], preferred_element_type=F32))
        term = spread * os[g]
        a = term if a is None else a + term
    _outproj_tail(a, x_ref, mod_ref, w_ref, g_ref, b_ref, out_ref, trunk)


def _att_outproj_prompt(outs, lses, x, ada, w, ln_g, ln_b, trunk, layer):
    d = D_MODEL
    tm = trunk.tile
    tps = trunk.tiles_per_seq
    k_in = w.shape[0]
    cls_specs = lambda width: [
        pl.BlockSpec((1, dil, tm // dil, width), lambda i: (i // tps, 0, i % tps, 0))
        for _, dil in ATT_GROUPS]
    e = (np.arange(LANES)[:, None] // LSE_REP == np.arange(ATT_W)[None, :] // ATT_DH) / LSE_REP
    return pl.pallas_call(
        functools.partial(_att_outproj_body, trunk=trunk),
        grid=(trunk.tokens // tm,),
        in_specs=cls_specs(ATT_W) + cls_specs(LANES) + [
            _resident((LANES, ATT_W), lambda i: (0, 0)),
            pl.BlockSpec((tm, d), lambda i: (i, 0)),
            trunk.mod_spec(layer, 1),
            _resident((k_in, d), lambda i: (0, 0)),
            _resident((1, d), lambda i: (0, 0)),
            _resident((1, d), lambda i: (0, 0)),
        ],
        out_specs=pl.BlockSpec((tm, d), lambda i: (i, 0)),
        out_shape=jax.ShapeDtypeStruct((trunk.tokens, d), F32),
        scratch_shapes=[pltpu.VMEM((N_GROUPS * (ATT_W // LANES + 1), tm, LANES), F32)],
        compiler_params=_params("arbitrary"),
        name="att_outproj",
    )(*outs, *lses, jnp.asarray(e, BF16), x, ada, w, ln_g.reshape(1, d), ln_b.reshape(1, d))


def _group_norm_gate(o, g, gn_g, gn_b):
    mu = jnp.mean(o, -1, keepdims=True)
    oc = o - mu
    var = jnp.mean(oc * oc, -1, keepdims=True)
    on = oc * lax.rsqrt(var + NORM_EPS) * gn_g + gn_b
    return jax.nn.silu(g) * on


RET_TILE = 1024


def _ret_decay_tables(chunk):
    lg = _log_gamma()
    pos = np.arange(chunk, dtype=np.float64)
    diff = pos[:, None] - pos[None, :]
    inner = np.where(diff >= 0, np.exp(np.maximum(diff, 0.0)[None] * lg[:, None, None]), 0.0)
    qd = np.exp((pos[None, :] + 1.0) * lg[:, None])
    kd = np.exp((chunk - 1.0 - pos[None, :]) * lg[:, None])
    cd = np.exp(chunk * lg)
    return inner, qd, kd, cd


def _ret_prompt_body(q_ref, k_ref, v_ref, g_ref, inner_ref, qd_ref, kd_ref, gng_ref, gnb_ref,
                     o_ref, sfin_ref, s_scr, *, cdec):
    t = pl.program_id(1)

    @pl.when(t == 0)
    def _():
        s_scr[...] = jnp.zeros_like(s_scr)

    for c in range(RET_TILE // RET_CHUNK):
        rows = slice(c * RET_CHUNK, (c + 1) * RET_CHUNK)
        for h in range(RET_HEADS):
            qk = slice(h * RET_DK, (h + 1) * RET_DK)
            vv = slice(h * RET_DV, (h + 1) * RET_DV)
            q = q_ref[rows, qk]
            k = k_ref[rows, qk] * (RET_DK ** -0.5)
            v = v_ref[rows, vv]
            scores = lax.dot_general(q, k, NT_DIMS, preferred_element_type=F32) * inner_ref[h]
            s_old = s_scr[h]
            o = (jnp.dot(scores.astype(BF16), v, preferred_element_type=F32)
                 + jnp.dot(q, s_old.astype(BF16), preferred_element_type=F32) * qd_ref[h])
            kdec = (k.astype(F32) * kd_ref[h]).astype(BF16)
            upd = lax.dot_general(kdec, v, TN_DIMS, preferred_element_type=F32)
            s_scr[h] = s_old * cdec[h] + upd
            gated = _group_norm_gate(o, g_ref[rows, vv].astype(F32), gng_ref[:, vv], gnb_ref[:, vv])
            o_ref[rows, vv] = gated.astype(o_ref.dtype)

    @pl.when(t == pl.num_programs(1) - 1)
    def _():
        sfin_ref[0] = s_scr[...]


def _ret_prompt(qkvg, gn_g, gn_b, n_seq, seq):
    inner, qd, kd, cd = _ret_decay_tables(RET_CHUNK)
    inner = jnp.asarray(inner, F32)
    qd_b = jnp.asarray(np.broadcast_to(qd[:, :, None], (RET_HEADS, RET_CHUNK, RET_DV)), F32)
    kd_b = jnp.asarray(np.broadcast_to(kd[:, :, None], (RET_HEADS, RET_CHUNK, RET_DK)), F32)
    cdec = tuple(float(np.float32(c)) for c in cd)
    tiles = seq // RET_TILE
    vw = RET_HEADS * RET_DV
    qw = RET_HEADS * RET_DK
    return pl.pallas_call(
        functools.partial(_ret_prompt_body, cdec=cdec),
        grid=(n_seq, tiles),
        in_specs=[
            pl.BlockSpec((RET_TILE, qw), lambda n, t: (n * tiles + t, 0)),
            pl.BlockSpec((RET_TILE, qw), lambda n, t: (n * tiles + t, 1)),
            pl.BlockSpec((RET_TILE, vw), lambda n, t: (n * tiles + t, 1)),
            pl.BlockSpec((RET_TILE, vw), lambda n, t: (n * tiles + t, 2)),
            _resident((RET_HEADS, RET_CHUNK, RET_CHUNK), lambda n, t: (0, 0, 0)),
            _resident((RET_HEADS, RET_CHUNK, RET_DV), lambda n, t: (0, 0, 0)),
            _resident((RET_HEADS, RET_CHUNK, RET_DK), lambda n, t: (0, 0, 0)),
            _resident((1, vw), lambda n, t: (0, 0)),
            _resident((1, vw), lambda n, t: (0, 0)),
        ],
        out_specs=[
            pl.BlockSpec((RET_TILE, vw), lambda n, t: (n * tiles + t, 0)),
            pl.BlockSpec((1, RET_HEADS, RET_DK, RET_DV), lambda n, t: (n, 0, 0, 0)),
        ],
        out_shape=[
            jax.ShapeDtypeStruct((n_seq * seq, vw), BF16),
            jax.ShapeDtypeStruct((n_seq, RET_HEADS, RET_DK, RET_DV), F32),
        ],
        scratch_shapes=[pltpu.VMEM((RET_HEADS, RET_DK, RET_DV), F32)],
        compiler_params=_params("arbitrary", "arbitrary"),
        name="retention_prompt",
    )(qkvg, qkvg, qkvg, qkvg, inner, qd_b, kd_b, gn_g.reshape(1, vw), gn_b.reshape(1, vw))


RET_SEQ_GROUP = SUBLANES
RET_S_ROWS = T_SAMPLE * RET_SEQ_GROUP


def _ret_sample_body(q_ref, k_ref, v_ref, g_ref, s_ref, w_ref, qd_ref, kd_ref, cd_ref,
                     gng_ref, gnb_ref, o_ref, so_ref):
    q = q_ref[...].reshape(RET_S_ROWS, RET_DK).astype(BF16)
    k = k_ref[...].reshape(RET_S_ROWS, RET_DK) * (RET_DK ** -0.5)
    v = v_ref[...].reshape(RET_S_ROWS, RET_DV).astype(BF16)
    g = g_ref[...].reshape(RET_S_ROWS, RET_DV)
    scores = lax.dot_general(q, k.astype(BF16), NT_DIMS, preferred_element_type=F32) * w_ref[0]
    o_intra = jnp.dot(scores.astype(BF16), v, preferred_element_type=F32)
    kdec = k * kd_ref[0]
    seq_of_row = lax.broadcasted_iota(jnp.int32, (RET_S_ROWS, 1), 0) % RET_SEQ_GROUP
    o_inter = jnp.zeros((RET_S_ROWS, RET_DV), F32)
    for j in range(RET_SEQ_GROUP):
        mine = seq_of_row == j
        s_old = s_ref[j, 0]
        r = jnp.dot(q, s_old.astype(BF16), preferred_element_type=F32)
        o_inter = jnp.where(mine, r, o_inter)
        kj = jnp.where(mine, kdec, 0.0).astype(BF16)
        upd = lax.dot_general(kj, v, TN_DIMS, preferred_element_type=F32)
        so_ref[j, 0] = s_old * cd_ref[0] + upd
    o = o_intra + o_inter * qd_ref[0]
    gated = _group_norm_gate(o, g, gng_ref[...], gnb_ref[...])
    o_ref[...] = gated.reshape(T_SAMPLE, RET_SEQ_GROUP, RET_DV).astype(o_ref.dtype)


RET_S_STEPS = (N_SAMPLE // RET_SEQ_GROUP) * RET_HEADS


def _ret_sample_part(qkvg, state, gn_g, gn_b):
    inner, qd, kd, cd = _ret_decay_tables(T_SAMPLE)
    row_t = np.arange(RET_S_ROWS) // RET_SEQ_GROUP
    row_j = np.arange(RET_S_ROWS) % RET_SEQ_GROUP
    same = (row_j[:, None] == row_j[None, :])
    w = np.where(same[None], inner[:, row_t[:, None], row_t[None, :]], 0.0)
    w = jnp.asarray(w, F32)
    qd_b = jnp.asarray(np.broadcast_to(qd[:, row_t, None], (RET_HEADS, RET_S_ROWS, RET_DV)), F32)
    kd_b = jnp.asarray(np.broadcast_to(kd[:, row_t, None], (RET_HEADS, RET_S_ROWS, RET_DK)), F32)
    cd_b = jnp.asarray(np.broadcast_to(cd[:, None, None], (RET_HEADS, 1, RET_DV)), F32)
    vw = RET_HEADS * RET_DV
    nq = RET_HEADS
    tg = (T_SAMPLE, RET_SEQ_GROUP)
    nh = RET_HEADS
    return _Part(
        _ret_sample_body,
        (qkvg, qkvg, qkvg, qkvg, state, w, qd_b, kd_b, cd_b,
         gn_g.reshape(1, vw), gn_b.reshape(1, vw)),
        [
            pl.BlockSpec(tg + (RET_DK,), lambda i: (0, i // nh, i % nh)),
            pl.BlockSpec(tg + (RET_DK,), lambda i: (0, i // nh, nq + i % nh)),
            pl.BlockSpec(tg + (RET_DV,), lambda i: (0, i // nh, nq + i % nh)),
            pl.BlockSpec(tg + (RET_DV,), lambda i: (0, i // nh, 2 * nq + i % nh)),
            pl.BlockSpec((RET_SEQ_GROUP, 1, RET_DK, RET_DV), lambda i: (i // nh, i % nh, 0, 0)),
            pl.BlockSpec((1, RET_S_ROWS, RET_S_ROWS), lambda i: (i % nh, 0, 0)),
            pl.BlockSpec((1, RET_S_ROWS, RET_DV), lambda i: (i % nh, 0, 0)),
            pl.BlockSpec((1, RET_S_ROWS, RET_DK), lambda i: (i % nh, 0, 0)),
            pl.BlockSpec((1, 1, RET_DV), lambda i: (i % nh, 0, 0)),
            pl.BlockSpec((1, RET_DV), lambda i: (0, i % nh)),
            pl.BlockSpec((1, RET_DV), lambda i: (0, i % nh)),
        ],
        [
            jax.ShapeDtypeStruct((T_SAMPLE, N_SAMPLE, vw), BF16),
            jax.ShapeDtypeStruct(state.shape, F32),
        ],
        [
            pl.BlockSpec(tg + (RET_DV,), lambda i: (0, i // nh, i % nh)),
            pl.BlockSpec((RET_SEQ_GROUP, 1, RET_DK, RET_DV), lambda i: (i // nh, i % nh, 0, 0)),
        ])


ATT_BQ = 1024
ATT_SLAB = 256
HEADS_PER_SLAB = ATT_SLAB // ATT_DH


def _band_window(cur_ref, prev_ref, s):
    if s == 0:
        return jnp.concatenate([prev_ref[0, 0], cur_ref[0, 0, 0:NBACK, :]], axis=0)
    return cur_ref[0, 0, (s - 1) * NBACK:(s + 1) * NBACK, :]


def _band_body(q_ref, kc_ref, kp_ref, vc_ref, vp_ref, bias_ref, o_ref, lse_ref, s_scr, p_scr):
    i = pl.program_id(2)
    ki = lax.broadcasted_iota(jnp.int32, (NBACK, 2 * NBACK), 1)
    has_prev = ki >= jnp.where(i > 0, 0, NBACK)
    lane_head = lax.broadcasted_iota(jnp.int32, (1, ATT_SLAB), 1) // ATT_DH
    qmasks = [jnp.where(lane_head == hh, ATT_SCALE, 0.0).astype(BF16)
              for hh in range(HEADS_PER_SLAB)]
    nsub = q_ref.shape[2] // NBACK
    pairs = nsub * ATT_HPG
    windows = {}

    def window(cur_ref, prev_ref, s):
        key = (id(cur_ref), s)
        if key not in windows:
            windows[key] = _band_window(cur_ref, prev_ref, s)
        return windows[key]

    def place(idx):
        s, h = divmod(idx, ATT_HPG)
        sl, hh = divmod(h, HEADS_PER_SLAB)
        return s, h, hh, slice(sl * ATT_SLAB, (sl + 1) * ATT_SLAB)

    def scores(idx):
        s, _, hh, lanes = place(idx)
        q = q_ref[0, 0, s * NBACK:(s + 1) * NBACK, lanes]
        s_scr[idx] = lax.dot_general(q * qmasks[hh], window(kc_ref, kp_ref, s)[:, lanes], NT_DIMS,
                                     preferred_element_type=F32)

    def softmax(idx):
        s, h, _, _ = place(idx)
        sc = s_scr[idx] + bias_ref[h]
        if s == 0:
            sc = jnp.where(has_prev, sc, -jnp.inf)
        m = jnp.max(sc, -1, keepdims=True)
        e = jnp.exp(sc - m)
        den = jnp.sum(e, -1, keepdims=True)
        p_scr[idx] = (e / den).astype(BF16)
        lse_ref[0, 0, s * NBACK:(s + 1) * NBACK, h * LSE_REP:(h + 1) * LSE_REP] = jnp.broadcast_to(
            m + jnp.log(den), (NBACK, LSE_REP))

    def values(idx):
        s, h, hh, lanes = place(idx)
        oh = jnp.dot(p_scr[idx], window(vc_ref, vp_ref, s)[:, lanes], preferred_element_type=F32)
        o_ref[0, 0, s * NBACK:(s + 1) * NBACK, h * ATT_DH:(h + 1) * ATT_DH] = (
            oh[:, hh * ATT_DH:(hh + 1) * ATT_DH].astype(o_ref.dtype))

    for stage in (scores, softmax, values):
        for idx in range(pairs):
            stage(idx)


def _band_bias(group):
    _, dil = ATT_GROUPS[group]
    steps = NBACK + np.arange(NBACK)[:, None] - np.arange(2 * NBACK)[None, :]
    valid = (steps >= 0) & (steps <= NBACK)
    slopes = _alibi_slopes()[group]
    bias = -(slopes[:, None, None] * dil) * steps[None]
    return np.where(valid[None], bias, -np.inf)


def _band_attention(qkv_cls, group):
    _, dil = ATT_GROUPS[group]
    n_seq, _, length, _ = qkv_cls.shape
    bq = min(ATT_BQ, length)
    nsub = bq // NBACK
    blocks = length // bq
    pairs = nsub * ATT_HPG
    prev = lambda i: jnp.maximum(i * nsub - 1, 0)
    cur = lambda col: pl.BlockSpec((1, 1, bq, ATT_W), lambda n, r, i: (n, r, i, col))
    halo = lambda col: pl.BlockSpec((1, 1, NBACK, ATT_W), lambda n, r, i: (n, r, prev(i), col))
    out_sds = jax.ShapeDtypeStruct((n_seq, dil, length, ATT_W), BF16)
    return pl.pallas_call(
        _band_body,
        grid=(n_seq, dil, blocks),
        in_specs=[cur(0), cur(1), halo(1), cur(2), halo(2),
                  _resident((ATT_HPG, NBACK, 2 * NBACK), lambda n, r, i: (0, 0, 0))],
        out_specs=[cur(0), pl.BlockSpec((1, 1, bq, LANES), lambda n, r, i: (n, r, i, 0))],
        out_shape=[out_sds, jax.ShapeDtypeStruct((n_seq, dil, length, LANES), F32)],
        scratch_shapes=[pltpu.VMEM((pairs, NBACK, 2 * NBACK), F32),
                        pltpu.VMEM((pairs, NBACK, 2 * NBACK), BF16)],
        compiler_params=_params("arbitrary", "arbitrary", "arbitrary"),
        name="band_attention",
    )(qkv_cls, qkv_cls, qkv_cls, qkv_cls, qkv_cls, jnp.asarray(_band_bias(group), F32))


SA_PASSES = tuple((0, (t,)) for t in range(T_SAMPLE)) + ((1, tuple(range(T_SAMPLE))),
                                                        (2, tuple(range(T_SAMPLE))))
SA_MAX_LB = max(w for w, _ in ATT_GROUPS)


def _att_sample_tables():
    slopes = _alibi_slopes()
    sels, biases = [], []
    bias_new = np.full((N_GROUPS, T_SAMPLE * ATT_HPG, LANES), -np.inf)
    cs = np.zeros((len(SA_PASSES), SUBLANES, LANES))
    lane = np.arange(LANES)
    for g, (lb, dil) in enumerate(ATT_GROUPS):
        passes = [p for p in SA_PASSES if p[0] == g]
        sel = np.zeros((ATT_HPG, T_SAMPLE, len(passes), ATT_HPG, LANES))
        for pi, (_, t_set) in enumerate(passes):
            for t in t_set:
                owns = np.ones(LANES, bool) if dil == 1 else (lane % dil == t)
                for h in range(ATT_HPG):
                    sel[h, t, pi, h, owns] = 1.0
        sels.append(sel.reshape(ATT_HPG * T_SAMPLE, len(passes) * ATT_HPG * LANES))
        pos = np.arange(lb)
        bias = np.full((T_SAMPLE, ATT_HPG, lb), -np.inf)
        for t in range(T_SAMPLE):
            if dil == 1:
                j = lb + t - pos
                valid = j <= NBACK
            else:
                j = NBACK - pos // dil
                valid = (pos % dil) == t
            bias[t] = np.where(valid[None, :], -(slopes[g][:, None] * dil) * j[None, :], -np.inf)
            for h in range(ATT_HPG):
                for t2 in range(T_SAMPLE):
                    if (dil == 1 and t2 <= t) or t2 == t:
                        bias_new[g, t * ATT_HPG + h, t2] = -slopes[g][h] * dil * (t - t2)
        biases.append(bias)
    for pi, (g, t_set) in enumerate(SA_PASSES):
        dil = ATT_GROUPS[g][1]
        for t in t_set:
            cs[pi, t] = 1.0 if dil == 1 else (lane % dil == t)
    bd = (np.arange(ATT_W)[None, :] // ATT_DH == np.arange(ATT_HPG)[:, None]).astype(np.float64)
    return sels, biases, bias_new, cs, bd


SA_SEQ_PER_STEP = 2


def _sa_new_row(y_ref, j, t, col):
    row = (pl.program_id(0) % (SUBLANES // SA_SEQ_PER_STEP)) * SA_SEQ_PER_STEP + j
    lane0 = (t * 3 * N_GROUPS + col) * ATT_W
    return y_ref[pl.ds(row, 1), lane0:lane0 + ATT_W]


def _sa_new_rows(y_ref, j, col, n_rows):
    rows = [_sa_new_row(y_ref, j, t, col) for t in range(T_SAMPLE)]
    return jnp.concatenate(rows + [jnp.zeros((n_rows - T_SAMPLE, ATT_W), F32)], 0)


def _att_scores_body(qd_ref, y_ref, c0_ref, c1_ref, c2_ref,
                     sel0_ref, sel1_ref, sel2_ref, b0_ref, b1_ref, b2_ref, bn_ref, bd_ref,
                     p0_ref, p1_ref, p2_ref, pn_ref, s_scr, e_scr):
    caches = (c0_ref, c1_ref, c2_ref)
    sels = (sel0_ref, sel1_ref, sel2_ref)
    biases = (b0_ref, b1_ref, b2_ref)
    p_refs = (p0_ref, p1_ref, p2_ref)
    bd = bd_ref[...]
    qsels, s_news = {}, {}
    for j in range(SA_SEQ_PER_STEP):
        for g in range(N_GROUPS):
            qsels[j, g] = jnp.dot(qd_ref[j, g].astype(BF16), sels[g][...],
                                  preferred_element_type=F32)
            qbd = jnp.concatenate(
                [jnp.broadcast_to(_sa_new_row(y_ref, j, t, g), (ATT_HPG, ATT_W)) * bd
                 for t in range(T_SAMPLE)], 0)
            kn = _sa_new_rows(y_ref, j, N_GROUPS + g, LANES)
            s_news[j, g] = lax.dot_general(qbd.astype(BF16), kn.astype(BF16), NT_DIMS,
                                           preferred_element_type=F32) * ATT_SCALE + bn_ref[g]
    for j in range(SA_SEQ_PER_STEP):
        e_new, dens, lses = {}, {}, {}
        pid = 0
        for g, (lb, dil) in enumerate(ATT_GROUPS):
            chunks = lb // LANES
            passes = [p for p in SA_PASSES if p[0] == g]
            qsel, s_new = qsels[j, g], s_news[j, g]
            pass_of = {}
            for local, (_, t_set) in enumerate(passes):
                for h in range(ATT_HPG):
                    col = (local * ATT_HPG + h) * LANES
                    qs = qsel[:, col:col + LANES]
                    for c in range(chunks):
                        lanes = slice(c * LANES, (c + 1) * LANES)
                        prod = caches[g][j, 0, h, :, lanes] * qs
                        part = prod.reshape(ATT_DH // SUBLANES, SUBLANES, LANES).sum(0)
                        s_scr[pid, h:h + 1, lanes] = part.sum(0, keepdims=True)
                for t in t_set:
                    pass_of[t] = pid
                pid += 1
            for t in range(T_SAMPLE):
                sc = s_scr[pass_of[t], :, :lb] * ATT_SCALE + biases[g][t]
                sn = s_new[t * ATT_HPG:(t + 1) * ATT_HPG]
                m = jnp.maximum(jnp.max(sc, -1, keepdims=True), jnp.max(sn, -1, keepdims=True))
                e = jnp.exp(sc - m)
                en = jnp.exp(sn - m)
                den = jnp.sum(e, -1, keepdims=True) + jnp.sum(en, -1, keepdims=True)
                e_scr[g * T_SAMPLE + t, :, :lb] = e
                e_new[g, t], dens[g, t], lses[g, t] = en, den, m + jnp.log(den)
        wts = {}
        for t in range(T_SAMPLE):
            ls = [lses[g, t] for g in range(N_GROUPS)]
            m = jnp.maximum(jnp.maximum(ls[0], ls[1]), ls[2])
            ws = [jnp.exp(l - m) for l in ls]
            tot = ws[0] + ws[1] + ws[2]
            for g in range(N_GROUPS):
                wts[g, t] = ws[g] / (tot * dens[g, t])
        for g, (lb, dil) in enumerate(ATT_GROUPS):
            pn_ref[j, g] = jnp.concatenate(
                [e_new[g, t] * wts[g, t] for t in range(T_SAMPLE)], 0)
            for local, (_, t_set) in enumerate([p for p in SA_PASSES if p[0] == g]):
                pc = None
                for t in t_set:
                    term = e_scr[g * T_SAMPLE + t, :, :lb] * wts[g, t]
                    pc = term if pc is None else pc + term
                p_refs[g][j, local] = pc


def _att_values_body(p0_ref, p1_ref, p2_ref, pn_ref, y_ref, c0_ref, c1_ref, c2_ref,
                     cs_ref, bd_ref, o_ref):
    caches = (c0_ref, c1_ref, c2_ref)
    p_refs = (p0_ref, p1_ref, p2_ref)
    bd = bd_ref[...]
    wfs = {}
    for j in range(SA_SEQ_PER_STEP):
        for pid, (g, _) in enumerate(SA_PASSES):
            local = pid - [p[0] for p in SA_PASSES].index(g)
            folded = []
            for h in range(ATT_HPG):
                acc = None
                for c in range(ATT_GROUPS[g][0] // LANES):
                    lanes = slice(c * LANES, (c + 1) * LANES)
                    term = caches[g][j, 0, h, :, lanes] * p_refs[g][j, local, h:h + 1, lanes]
                    acc = term if acc is None else acc + term
                folded.append(acc)
            wfs[j, pid] = jnp.concatenate(folded, 0).astype(BF16)
    for j in range(SA_SEQ_PER_STEP):
        o = jnp.zeros((SUBLANES, ATT_W), F32)
        o_new = jnp.zeros((T_SAMPLE * ATT_HPG, ATT_W), F32)
        for g in range(N_GROUPS):
            vn = _sa_new_rows(y_ref, j, 2 * N_GROUPS + g, LANES)
            o_new = o_new + jnp.dot(pn_ref[j, g].astype(BF16), vn.astype(BF16),
                                    preferred_element_type=F32)
        for pid in range(len(SA_PASSES)):
            o = o + lax.dot_general(cs_ref[pid].astype(BF16), wfs[j, pid], NT_DIMS,
                                    preferred_element_type=F32)
        extra = [jnp.sum(o_new[t * ATT_HPG:(t + 1) * ATT_HPG] * bd, 0, keepdims=True)
                 for t in range(T_SAMPLE)]
        o_ref[j] = o[:T_SAMPLE] + jnp.concatenate(extra, 0)


def _sa_new_spec():
    per_block = SUBLANES // SA_SEQ_PER_STEP
    return pl.BlockSpec((SUBLANES, T_SAMPLE * 3 * N_GROUPS * ATT_W), lambda i: (i // per_block, 0))


def _sa_cache_specs(caches, kv):
    views = [c.transpose(0, 2, 3, 4, 1) for c in caches]
    specs = [pl.BlockSpec((SA_SEQ_PER_STEP, 1, ATT_HPG, ATT_DH, lb), lambda i: (i, kv, 0, 0, 0))
             for lb, _ in ATT_GROUPS]
    return views, specs


def _sa_const_specs(tables):
    return [_resident(t.shape, lambda i, nd=t.ndim: (0,) * nd) for t in tables]


def _sa_prob_layout():
    shapes, specs = [], []
    for g, (lb, _) in enumerate(ATT_GROUPS):
        n_pass = len([p for p in SA_PASSES if p[0] == g])
        shapes.append(jax.ShapeDtypeStruct((N_SAMPLE, n_pass, ATT_HPG, lb), F32))
        specs.append(pl.BlockSpec((SA_SEQ_PER_STEP, n_pass, ATT_HPG, lb), lambda i: (i, 0, 0, 0)))
    shapes.append(jax.ShapeDtypeStruct((N_SAMPLE, N_GROUPS, T_SAMPLE * ATT_HPG, LANES), F32))
    specs.append(pl.BlockSpec((SA_SEQ_PER_STEP, N_GROUPS, T_SAMPLE * ATT_HPG, LANES),
                              lambda i: (i, 0, 0, 0)))
    return shapes, specs


def _att_scores_part(y, caches):
    sels, biases, bias_new, _, bd = _att_sample_tables()
    q = y.reshape(N_SAMPLE, T_SAMPLE, 3 * N_GROUPS, ATT_HPG, ATT_DH)[:, :, :N_GROUPS]
    qd = q.transpose(0, 2, 4, 3, 1).reshape(N_SAMPLE, N_GROUPS, ATT_DH, ATT_HPG * T_SAMPLE)
    views, cache_specs = _sa_cache_specs(caches, 0)
    tables = ([jnp.asarray(s, BF16) for s in sels] + [jnp.asarray(b, F32) for b in biases]
              + [jnp.asarray(bias_new, F32), jnp.asarray(bd, F32)])
    out_shape, out_specs = _sa_prob_layout()
    return _Part(
        _att_scores_body,
        (qd, y, *views, *tables),
        [pl.BlockSpec((SA_SEQ_PER_STEP, N_GROUPS, ATT_DH, ATT_HPG * T_SAMPLE),
                      lambda i: (i, 0, 0, 0)), _sa_new_spec()]
        + cache_specs + _sa_const_specs(tables),
        out_shape, out_specs,
        [pltpu.VMEM((len(SA_PASSES), ATT_HPG, SA_MAX_LB), F32),
         pltpu.VMEM((N_GROUPS * T_SAMPLE, ATT_HPG, SA_MAX_LB), F32)])


def _att_values_part(probs, y, caches):
    _, _, _, cs, bd = _att_sample_tables()
    views, cache_specs = _sa_cache_specs(caches, 1)
    tables = [jnp.asarray(cs, F32), jnp.asarray(bd, F32)]
    _, prob_specs = _sa_prob_layout()
    return _Part(
        _att_values_body,
        (*probs, y, *views, *tables),
        prob_specs + [_sa_new_spec()] + cache_specs + _sa_const_specs(tables),
        [jax.ShapeDtypeStruct((N_SAMPLE, T_SAMPLE, ATT_W), F32)],
        [pl.BlockSpec((SA_SEQ_PER_STEP, T_SAMPLE, ATT_W), lambda i: (i, 0, 0))])


def kernel(x_prompt, x_sample, c_prompt, c_sample, state_ret, cache_kv_w128, cache_kv_w512,
           cache_kv_w2048, w_ada, b_ada, ln_g, ln_b, w_ffn_in, w_ffn_out, w_ret_in, ret_gn_g,
           ret_gn_b, w_ret_out, w_att_in, w_att_out):
    d = D_MODEL
    n_p, seq, _ = x_prompt.shape
    w_ffn_in_b = {(0, 0): w_ffn_in[0, 0].astype(BF16)}
    w_ffn_out_b = {(0, 0): w_ffn_out[0, 0].astype(BF16)}
    w_ret_in_b = w_ret_in.astype(BF16)
    later_ffn = [(0, 1), (1, 0), (1, 1)]
    cast_items = ([(w_ffn_in, lw) for lw in later_ffn] + [(w_ffn_out, lw) for lw in later_ffn]
                  + [(w_ret_out, ()), (w_att_in, ()), (w_att_out, ())])

    c_all = jnp.concatenate(
        [c_sample, c_prompt, jnp.zeros((ADA_ROWS - N_SAMPLE - n_p, d), F32)], axis=0)
    ada = _ada_table(c_all, w_ada, b_ada)

    prompt = _Trunk(n_p * seq, 512, False, seq // 512)
    prompt_wide = _Trunk(n_p * seq, 1024, False, seq // 1024)
    sample = _Trunk(N_SAMPLE * T_SAMPLE, N_SAMPLE, True, 1)
    xp = x_prompt.reshape(n_p * seq, d)
    xs = x_sample.transpose(1, 0, 2).reshape(T_SAMPLE * N_SAMPLE, d)

    steps = RET_S_STEPS
    tile_s = n_p * seq // steps
    prompt_s = _Trunk(n_p * seq, tile_s, False, seq // tile_s)

    def ffn_part(x, trunk, layer, which):
        sub = 0 if which == 0 else 2
        return _ffn_part(x, ada, w_ffn_in_b[layer, which], w_ffn_out_b[layer, which],
                         ln_g[layer, sub], ln_b[layer, sub], trunk, layer, sub)

    def ffn(x, trunk, layer, which):
        return _run([ffn_part(x, trunk, layer, which)], trunk.tokens // trunk.tile, "ffn")[0][0]

    xs = ffn(xs, sample, 0, 0)
    qkvg_s = _inproj(xs, ada, w_ret_in_b, sample, 0, F32)

    (xp,), (gated_s, ret_s), cast = _run(
        [ffn_part(xp, prompt_s, 0, 0),
         _ret_sample_part(qkvg_s.reshape(T_SAMPLE, N_SAMPLE, -1), state_ret, ret_gn_g, ret_gn_b),
         _cast_part(cast_items, steps)],
        steps, "ffn_ret_sample")
    for k, lw in enumerate(later_ffn):
        w_ffn_in_b[lw], w_ffn_out_b[lw] = cast[k], cast[len(later_ffn) + k]
    w_ret_out_b, w_att_in_b, w_att_out_b = cast[2 * len(later_ffn):]

    xs = _outproj(gated_s.reshape(T_SAMPLE * N_SAMPLE, -1), xs, ada, w_ret_out_b,
                  ln_g[0, 1], ln_b[0, 1], sample, 0)
    xs = ffn(xs, sample, 0, 1)
    xs = ffn(xs, sample, 1, 0)
    qkv_s = _inproj(xs, ada, w_att_in_b, sample, 1, F32, tiles_to_columns=True)
    caches = (cache_kv_w128, cache_kv_w512, cache_kv_w2048)

    qkvg_p = _inproj(xp, ada, w_ret_in_b, prompt, 0, BF16)
    gated_p, ret_p = _ret_prompt(qkvg_p, ret_gn_g, ret_gn_b, n_p, seq)
    xp = _outproj(gated_p, xp, ada, w_ret_out_b, ln_g[0, 1], ln_b[0, 1], prompt_wide, 0)

    probs, (xp,) = _run([_att_scores_part(qkv_s, caches), ffn_part(xp, prompt_s, 0, 1)],
                        steps, "att_scores_ffn")
    (xp,), (att_s,) = _run([ffn_part(xp, prompt_s, 1, 0), _att_values_part(probs, qkv_s, caches)],
                           steps, "ffn_att_values")

    keep_p = [min(w, seq) for w, _ in ATT_GROUPS]
    *qkv_cls, kv_tail = _att_inproj_prompt(xp, ada, w_att_in_b, prompt, 1, max(keep_p))
    band = [_band_attention(qkv_cls[g], g) for g in range(N_GROUPS)]
    xp = _att_outproj_prompt([o for o, _ in band], [l for _, l in band], xp, ada, w_att_out_b,
                             ln_g[1, 1], ln_b[1, 1], prompt_wide, 1)
    kv_tail = kv_tail.reshape(n_p, max(keep_p), 2, N_GROUPS, ATT_HPG, ATT_DH)
    kv_p = [kv_tail[:, max(keep_p) - keep_p[g]:, :, g] for g in range(N_GROUPS)]

    att_s = att_s.transpose(1, 0, 2).reshape(T_SAMPLE * N_SAMPLE, ATT_W)
    xs = _outproj(att_s, xs, ada, w_att_out_b, ln_g[1, 1], ln_b[1, 1], sample, 1)
    new_s = qkv_s.reshape(N_SAMPLE, T_SAMPLE, 3, N_GROUPS, ATT_HPG, ATT_DH)
    kv_s = [new_s[:, :, 1:, g] for g in range(N_GROUPS)]

    xp = ffn(xp, prompt, 1, 1)
    xs = ffn(xs, sample, 1, 1)

    y_prompt = xp.reshape(n_p, seq, d)
    y_sample = xs.reshape(T_SAMPLE, N_SAMPLE, d).transpose(1, 0, 2)
    return (y_prompt, y_sample, ret_p, ret_s, kv_p[0], kv_s[0], kv_p[1], kv_s[1], kv_p[2], kv_s[2])
```
